```python
import math
import jax, jax.numpy as jnp
from jax import lax
import numpy as np

D_MODEL = 1024
BATCH = 8
SEQ = 8192
DEPTH = 1

GRID_W = 64
CTX_LEN = 256
MIX_WIDTH = D_MODEL
HY_WIDTH = MIX_WIDTH // 2
RET_WIDTH = MIX_WIDTH - HY_WIDTH
RET_HEADS = 8
RET_HEAD_DIM = RET_WIDTH // RET_HEADS
RET_CHUNK = 128
HY_ORDER = 2
HY_PROJ = (HY_ORDER + 1) * HY_WIDTH
IN_COLS = HY_PROJ + 4 * RET_WIDTH
SHORT_CONV_W = 3
HY_BANDS = 16
HY_EMB = 1 + 2 * HY_BANDS
HY_FILT_HID = 64
HY_DECAY_TARGET = 1e-2
HY_FAST_PCT = 0.3
HY_SLOW_PCT = 1.5
D_FF = 4 * D_MODEL
ROPE_BASE = 10000.0
ROPE_PAIRS_AXIS = RET_HEAD_DIM // 4
NORM_EPS = 1e-6

kernel_name = "hymba_hyena_retnet_dit_block"

F32 = jnp.float32


def _rmsnorm(x, g):
    xf = x.astype(F32)
    y = xf * lax.rsqrt(jnp.mean(jnp.square(xf), axis=-1, keepdims=True) + NORM_EPS)
    return (y * g.astype(F32)).astype(x.dtype)


def _modulate(h, shift, scale):
    return h * (1.0 + scale) + shift


def _rope_2d(L):
    rows = L // GRID_W
    r, col = jnp.meshgrid(jnp.arange(rows, dtype=F32), jnp.arange(GRID_W, dtype=F32), indexing="ij")
    inv = ROPE_BASE ** (-jnp.arange(ROPE_PAIRS_AXIS, dtype=F32) / ROPE_PAIRS_AXIS)
    ang = jnp.concatenate([r.reshape(-1, 1) * inv, col.reshape(-1, 1) * inv], axis=-1)
    return jnp.cos(ang), jnp.sin(ang)


def _apply_rope(t, rope):
    cos, sin = rope
    t2 = t.reshape(t.shape[:-1] + (RET_HEAD_DIM // 2, 2))
    a, b = t2[..., 0], t2[..., 1]
    cs, sn = cos[None, :, None, :], sin[None, :, None, :]
    return jnp.stack([a * cs - b * sn, a * sn + b * cs], axis=-1).reshape(t.shape)


def _short_conv(u, w, b):
    up = jnp.pad(u, ((0, 0), (1, 1), (0, 0)))
    return up[:, :-2] * w[0] + up[:, 1:-1] * w[1] + up[:, 2:] * w[2] + b


def _hyena_filters(L, w1, b1, fr1, w2, b2, fr2, w3):
    t = jnp.linspace(0.0, 1.0, L, dtype=F32)[:, None]
    w = (2.0 * math.pi / L) * jnp.arange(L, dtype=F32)[:, None]
    bands = jnp.linspace(1e-4, HY_BANDS - 1, HY_BANDS, dtype=F32)[None, :]
    z = jnp.concatenate([t, jnp.cos(bands * w), -jnp.sin(bands * w)], axis=-1)
    h = jnp.sin(fr1.astype(F32) * (z @ w1.astype(F32) + b1.astype(F32)))
    h = jnp.sin(fr2.astype(F32) * (h @ w2.astype(F32) + b2.astype(F32)))
    h = (h @ w3.astype(F32)).reshape(L, 2, HY_WIDTH)
    deltas = jnp.abs(jnp.linspace(math.log(HY_DECAY_TARGET) / HY_SLOW_PCT,
                                  math.log(HY_DECAY_TARGET) / HY_FAST_PCT, HY_WIDTH, dtype=F32))
    h = h * jnp.exp(-t * deltas)[:, None, :]
    return h / (jnp.sum(jnp.abs(h), axis=(0, 1), keepdims=True) + 1e-6)


def _bidir_long_conv(v, h, bias):
    L = v.shape[1]
    k = jnp.concatenate([h[:, 0], jnp.zeros((1, HY_WIDTH), F32), h[:0:-1, 1]], axis=0)
    vf = jnp.fft.rfft(v.astype(F32), n=2 * L, axis=1)
    kf = jnp.fft.rfft(k, n=2 * L, axis=0)
    y = jnp.fft.irfft(vf * kf[None], n=2 * L, axis=1)[:, :L]
    return (y + v.astype(F32) * bias.astype(F32)).astype(v.dtype)


def _ret_heads(t):
    B, L, _ = t.shape
    return t.reshape(B, L, RET_HEADS, RET_HEAD_DIM)


def _ret_kv(u):
    _, k, v, _ = jnp.split(u[..., HY_PROJ:], 4, axis=-1)
    k = _ret_heads(k).astype(F32) * (RET_HEAD_DIM ** -0.5)
    v = _ret_heads(v).astype(F32)
    return k.transpose(0, 2, 1, 3), v.transpose(0, 2, 1, 3)


def _ret_final_state(k, v, log_gamma):
    Lc = k.shape[2]
    w = jnp.exp(log_gamma[:, None] * (Lc - 1 - jnp.arange(Lc, dtype=F32))[None, :])
    return jnp.einsum("bhmd,bhme->bhde", k * w[None, :, :, None], v)


def _retention_chunkwise(q, k, v, log_gamma, s0):
    B, H, L, dk = q.shape
    dv = v.shape[-1]
    C = RET_CHUNK
    N = L // C
    qc = q.reshape(B, H, N, C, dk)
    kc = k.reshape(B, H, N, C, dk)
    vc = v.reshape(B, H, N, C, dv)
    idx = jnp.arange(C, dtype=F32)
    lg = log_gamma[:, None]
    diff = idx[:, None] - idx[None, :]
    dmask = jnp.where(diff >= 0, jnp.exp(lg[:, :, None] * jnp.maximum(diff, 0.0)[None]), 0.0)
    scores = jnp.einsum("bhncd,bhnmd->bhncm", qc, kc) * dmask[None, :, None]
    out_inner = jnp.einsum("bhncm,bhnme->bhnce", scores, vc)
    w_k = jnp.exp(lg * (C - 1 - idx)[None, :])
    t = jnp.einsum("bhnmd,bhnme->bhnde", kc * w_k[None, :, None, :, None], vc)
    decay_chunk = jnp.exp(log_gamma * C)[None, :, None, None]

    def step(s, t_n):
        return decay_chunk * s + t_n, s

    _, s_prev = lax.scan(step, s0, jnp.moveaxis(t, 2, 0))
    s_prev = jnp.moveaxis(s_prev, 0, 2)
    w_q = jnp.exp(lg * (idx + 1.0)[None, :])
    out_cross = jnp.einsum("bhncd,bhnde->bhnce", qc * w_q[None, :, None, :, None], s_prev)
    return (out_inner + out_cross).reshape(B, H, L, dv)


def _token_mixers(u, rope, s0_f, s0_b, conv_w, conv_b, f_w1, f_b1, f_fr1, f_w2, f_b2, f_fr2, f_w3,
                  hy_bias, log_gamma, gn_g):
    B, L, _ = u.shape
    uh = _short_conv(u[..., :HY_PROJ], conv_w, conv_b)
    x0, x1, v = jnp.split(uh, 3, axis=-1)
    h = _hyena_filters(L, f_w1, f_b1, f_fr1, f_w2, f_b2, f_fr2, f_w3)
    y_hy = _bidir_long_conv(v * x1, h, hy_bias) * x0
    q, k, vr, g = jnp.split(u[..., HY_PROJ:], 4, axis=-1)
    q = _ret_heads(q)
    k = _ret_heads(k) * (RET_HEAD_DIM ** -0.5)
    if rope is not None:
        q = _apply_rope(q, rope)
        k = _apply_rope(k, rope)
    q = q.astype(F32).transpose(0, 2, 1, 3)
    k = k.astype(F32).transpose(0, 2, 1, 3)
    vr = _ret_heads(vr).astype(F32).transpose(0, 2, 1, 3)
    o_f = _retention_chunkwise(q, k, vr, log_gamma[0], s0_f)
    o_b = jnp.flip(_retention_chunkwise(jnp.flip(q, 2), jnp.flip(k, 2), jnp.flip(vr, 2), log_gamma[1], s0_b), 2)
    o = (o_f + o_b).transpose(0, 2, 1, 3)
    mu = jnp.mean(o, axis=-1, keepdims=True)
    var = jnp.mean(jnp.square(o - mu), axis=-1, keepdims=True)
    o = ((o - mu) * lax.rsqrt(var + NORM_EPS)).reshape(B, L, RET_WIDTH) * gn_g.astype(F32)
    y_ret = (o * jax.nn.silu(g.astype(F32))).astype(u.dtype)
    return jnp.concatenate([y_hy, y_ret], axis=-1)


def _sq_relu_mlp(h, w1, w2):
    return jnp.square(jax.nn.relu(h @ w1)) @ w2


def setup_inputs(seed: int = 0) -> dict:
    key = jax.random.key(seed)
    ks = jax.random.split(key, 28)

    def nrm(k, shape, scale):
        return scale * jax.random.normal(k, shape, F32)

    g0 = 1.0 - 2.0 ** (-5.0 - np.arange(RET_HEADS))
    logit0 = jnp.asarray(np.log(g0) - np.log1p(-g0), dtype=F32)
    return {
        "x": nrm(ks[0], (BATCH, SEQ, D_MODEL), 1.0),
        "c": nrm(ks[1], (BATCH, D_MODEL), 1.0),
        "ctx": nrm(ks[2], (BATCH, CTX_LEN, D_MODEL), 1.0),
        "c_ctx": nrm(ks[3], (D_MODEL,), 1.0),
        "w_ada": nrm(ks[4], (DEPTH, D_MODEL, 6 * D_MODEL), 0.2 * D_MODEL ** -0.5),
        "b_ada": nrm(ks[5], (DEPTH, 6 * D_MODEL), 0.02),
        "norm1_g": 1.0 + nrm(ks[6], (DEPTH, D_MODEL), 0.02),
        "w_in": nrm(ks[7], (DEPTH, D_MODEL, IN_COLS), D_MODEL ** -0.5),
        "hy_conv_w": nrm(ks[8], (DEPTH, SHORT_CONV_W, HY_PROJ), SHORT_CONV_W ** -0.5),
        "hy_conv_b": nrm(ks[9], (DEPTH, HY_PROJ), 0.02),
        "hy_f_w1": nrm(ks[10], (DEPTH, HY_EMB, HY_FILT_HID), HY_EMB ** -0.5),
        "hy_f_b1": nrm(ks[11], (DEPTH, HY_FILT_HID), 0.02),
        "hy_f_freq1": 1.0 + nrm(ks[12], (DEPTH, HY_FILT_HID), 0.02),
        "hy_f_w2": nrm(ks[13], (DEPTH, HY_FILT_HID, HY_FILT_HID), HY_FILT_HID ** -0.5),
        "hy_f_b2": nrm(ks[14], (DEPTH, HY_FILT_HID), 0.02),
        "hy_f_freq2": 1.0 + nrm(ks[15], (DEPTH, HY_FILT_HID), 0.02),
        "hy_f_w3": nrm(ks[16], (DEPTH, HY_FILT_HID, 2 * HY_WIDTH), HY_FILT_HID ** -0.5),
        "hy_bias": nrm(ks[17], (DEPTH, HY_WIDTH), 0.5),
        "ret_decay_logit": logit0[None, None, :] + nrm(ks[18], (DEPTH, 2, RET_HEADS), 0.1),
        "ret_gn_g": 1.0 + nrm(ks[19], (DEPTH, RET_WIDTH), 0.02),
        "w_out": nrm(ks[20], (DEPTH, MIX_WIDTH, D_MODEL), MIX_WIDTH ** -0.5),
        "norm2_g": 1.0 + nrm(ks[21], (DEPTH, D_MODEL), 0.02),
        "w_mlp1": nrm(ks[22], (DEPTH, D_MODEL, D_FF), D_MODEL ** -0.5),
        "w_mlp2": nrm(ks[23], (DEPTH, D_FF, D_MODEL), D_FF ** -0.5),
        "norm_f_g": 1.0 + nrm(ks[24], (D_MODEL,), 0.02),
    }


def reference(x, c, ctx, c_ctx, w_ada, b_ada, norm1_g, w_in, hy_conv_w, hy_conv_b, hy_f_w1, hy_f_b1,
              hy_f_freq1, hy_f_w2, hy_f_b2, hy_f_freq2, hy_f_w3, hy_bias, ret_decay_logit, ret_gn_g,
              w_out, norm2_g, w_mlp1, w_mlp2, norm_f_g):
    L = x.shape[1]
    rope = _rope_2d(L)
    for l in range(DEPTH):
        mx = jnp.split(jax.nn.silu(c) @ w_ada[l] + b_ada[l], 6, axis=-1)
        mc = jnp.split(jax.nn.silu(c_ctx) @ w_ada[l] + b_ada[l], 6, axis=-1)
        log_gamma = jax.nn.log_sigmoid(ret_decay_logit[l].astype(F32))
        mixer_w = (hy_conv_w[l], hy_conv_b[l], hy_f_w1[l], hy_f_b1[l], hy_f_freq1[l], hy_f_w2[l],
                   hy_f_b2[l], hy_f_freq2[l], hy_f_w3[l], hy_bias[l], log_gamma, ret_gn_g[l])
        hx = _modulate(_rmsnorm(x, norm1_g[l]), mx[0][:, None], mx[1][:, None])
        hc = _modulate(_rmsnorm(ctx, norm1_g[l]), mc[0], mc[1])
        ux = hx @ w_in[l]
        uc = hc @ w_in[l]
        kc, vc = _ret_kv(uc)
        s_f = _ret_final_state(kc, vc, log_gamma[0])
        s_b = _ret_final_state(jnp.flip(kc, 2), jnp.flip(vc, 2), log_gamma[1])
        mix_x = _token_mixers(ux, rope, s_f, s_b, *mixer_w)
        x_new = x + mx[2][:, None] * (mix_x @ w_out[l])
        hx2 = _modulate(_rmsnorm(x_new, norm2_g[l]), mx[3][:, None], mx[4][:, None])
        x_new = x_new + mx[5][:, None] * _sq_relu_mlp(hx2, w_mlp1[l], w_mlp2[l])
        if l < DEPTH - 1:
            zeros = jnp.zeros_like(s_f)
            mix_c = _token_mixers(uc, None, zeros, zeros, *mixer_w)
            ctx = ctx + mc[2] * (mix_c @ w_out[l])
            hc2 = _modulate(_rmsnorm(ctx, norm2_g[l]), mc[3], mc[4])
            ctx = ctx + mc[5] * _sq_relu_mlp(hc2, w_mlp1[l], w_mlp2[l])
        x = x_new
    return _rmsnorm(x, norm_f_g)
```

```python
import functools
import math

import numpy as np
import jax
import jax.numpy as jnp
from jax import lax
from jax.experimental import pallas as pl
from jax.experimental.pallas import tpu as pltpu

F32 = jnp.float32
BF16 = jnp.bfloat16
HIGHEST = lax.Precision.HIGHEST

GRID_W = 64
HY_WIDTH = 512
RET_WIDTH = 512
RET_HEADS = 8
RET_HEAD_DIM = RET_WIDTH // RET_HEADS
HY_PROJ = 3 * HY_WIDTH
HY_BANDS = 16
HY_FILT_HID = 64
HY_DECAY_TARGET = 1e-2
HY_FAST_PCT = 0.3
HY_SLOW_PCT = 1.5
ROPE_BASE = 10000.0
ROPE_PAIRS_AXIS = RET_HEAD_DIM // 4
NORM_EPS = 1e-6

LANES = 128
HEADS_PER_BLOCK = LANES // RET_HEAD_DIM
RET_BLOCKS = RET_WIDTH // LANES
RET_CHUNK = 256
DFT_N2 = 128
DFT_KPAD = 80
VMEM_LIMIT = 56 * 1024 * 1024


def _cparams(sem):
    return pltpu.CompilerParams(dimension_semantics=sem, vmem_limit_bytes=VMEM_LIMIT)


def _const_spec(shape):
    nd = len(shape)
    return pl.BlockSpec(shape, lambda *_: (0,) * nd)


def _silu(x):
    return x * jax.nn.sigmoid(x)


def _rms(x):
    return x * lax.rsqrt(jnp.mean(x * x, axis=-1, keepdims=True) + NORM_EPS)


def _ada_kernel(c_ref, w_ref, b_ref, o_ref):
    s = _silu(c_ref[...])
    o_ref[...] = jnp.dot(s, w_ref[...], preferred_element_type=F32, precision=HIGHEST) + b_ref[...]


def _ada(cvec, w, b):
    R, D = cvec.shape
    N = w.shape[1]
    return pl.pallas_call(
        _ada_kernel,
        grid=(N // D,),
        in_specs=[_const_spec((R, D)), pl.BlockSpec((D, D), lambda j: (0, j)),
                  pl.BlockSpec((1, D), lambda j: (0, j))],
        out_specs=pl.BlockSpec((R, D), lambda j: (0, j)),
        out_shape=jax.ShapeDtypeStruct((R, N), F32),
        compiler_params=_cparams(("arbitrary",)),
        name="ada",
    )(cvec, w, b)


def _inproj_kernel(x_ref, g_ref, sh_ref, sc_ref, w_ref, o_ref):
    h = _rms(x_ref[0]) * g_ref[...]
    h = h * (1.0 + sc_ref[0]) + sh_ref[0]
    o_ref[0] = jnp.dot(h.astype(BF16), w_ref[...], preferred_element_type=F32).astype(BF16)


def _inproj(x, g, shift, scale, w, tm):
    B, L, D = x.shape
    N = w.shape[1]
    return pl.pallas_call(
        _inproj_kernel,
        grid=(B, L // tm),
        in_specs=[pl.BlockSpec((1, tm, D), lambda b, i: (b, i, 0)),
                  _const_spec((1, D)),
                  pl.BlockSpec((1, 1, D), lambda b, i: (b, 0, 0)),
                  pl.BlockSpec((1, 1, D), lambda b, i: (b, 0, 0)),
                  pl.BlockSpec((D, N), lambda b, i: (0, 0), pipeline_mode=pl.Buffered(1))],
        out_specs=pl.BlockSpec((1, tm, N), lambda b, i: (b, i, 0)),
        out_shape=jax.ShapeDtypeStruct((B, L, N), BF16),
        compiler_params=_cparams(("parallel", "parallel")),
        name="inproj",
    )(x, g, shift, scale, w)


def _block_diag_mask():
    r = lax.broadcasted_iota(jnp.int32, (LANES, LANES), 0) // RET_HEAD_DIM
    c = lax.broadcasted_iota(jnp.int32, (LANES, LANES), 1) // RET_HEAD_DIM
    return r == c


def _kt_v(k, v):
    return lax.dot_general(k, v, (((0,), (0,)), ((), ())), preferred_element_type=F32)


def _ctx_state_kernel(k_ref, v_ref, lgf_ref, lgb_ref, sf_ref, sb_ref):
    k = k_ref[0].astype(F32) * (RET_HEAD_DIM ** -0.5)
    v = v_ref[0]
    lc = k.shape[0]
    m = lax.broadcasted_iota(jnp.int32, (lc, LANES), 0).astype(F32)
    wf = jnp.exp(lgf_ref[0] * (lc - 1.0 - m))
    wb = jnp.exp(lgb_ref[0] * m)
    bd = _block_diag_mask()
    sf_ref[0, 0] = jnp.where(bd, _kt_v((k * wf).astype(BF16), v), 0.0)
    sb_ref[0, 0] = jnp.where(bd, _kt_v((k * wb).astype(BF16), v), 0.0)


def _ctx_states(uc, lgf, lgb):
    B, lc, _ = uc.shape
    kb = (HY_PROJ + RET_WIDTH) // LANES
    vb = (HY_PROJ + 2 * RET_WIDTH) // LANES
    st = jax.ShapeDtypeStruct((B, RET_BLOCKS, LANES, LANES), F32)
    sspec = pl.BlockSpec((1, 1, LANES, LANES), lambda b, p: (b, p, 0, 0))
    return pl.pallas_call(
        _ctx_state_kernel,
        grid=(B, RET_BLOCKS),
        in_specs=[pl.BlockSpec((1, lc, LANES), lambda b, p: (b, 0, kb + p)),
                  pl.BlockSpec((1, lc, LANES), lambda b, p: (b, 0, vb + p)),
                  pl.BlockSpec((1, 1, LANES), lambda b, p: (p, 0, 0)),
                  pl.BlockSpec((1, 1, LANES), lambda b, p: (p, 0, 0))],
        out_specs=[sspec, sspec],
        out_shape=[st, st],
        compiler_params=_cparams(("parallel", "parallel")),
        name="ctx_state",
    )(uc, uc, lgf, lgb)


def _filt_kernel(bands_ref, w1t_ref, w1c_ref, w1s_ref, b1_ref, fr1_ref, w2_ref, b2_ref, fr2_ref,
                 w3_ref, dl_ref, h_ref, asum_ref, *, seq_len, tl):
    i = pl.program_id(0)
    n = (lax.broadcasted_iota(jnp.int32, (tl, 1), 0) + i * tl).astype(F32)
    t = n / (seq_len - 1.0)
    w = (2.0 * math.pi / seq_len) * n
    arg = bands_ref[...] * w
    dot = functools.partial(jnp.dot, preferred_element_type=F32, precision=HIGHEST)
    pre = t * w1t_ref[...] + dot(jnp.cos(arg), w1c_ref[...]) - dot(jnp.sin(arg), w1s_ref[...]) + b1_ref[...]
    h = jnp.sin(fr1_ref[...] * pre)
    h = jnp.sin(fr2_ref[...] * (dot(h, w2_ref[...]) + b2_ref[...]))
    h = dot(h, w3_ref[...]) * jnp.exp(-t * dl_ref[...])

    @pl.when(i == 0)
    def _():
        asum_ref[...] = jnp.zeros_like(asum_ref)

    asum_ref[...] += jnp.sum(jnp.abs(h), axis=0, keepdims=True)
    h_ref[0] = h[:, :HY_WIDTH].astype(BF16)
    h_ref[1] = jnp.where(n == 0.0, 0.0, h[:, HY_WIDTH:]).astype(BF16)


def _pad2(a, rows, cols):
    return jnp.pad(a.astype(F32), ((0, rows - a.shape[0]), (0, cols - a.shape[1])))


def _filters(seq_len, w1, b1, fr1, w2, b2, fr2, w3):
    tl = 1024
    P = LANES
    bands = np.zeros((1, P), np.float32)
    bands[0, :HY_BANDS] = np.linspace(1e-4, HY_BANDS - 1, HY_BANDS, dtype=np.float32)
    deltas = np.abs(np.linspace(math.log(HY_DECAY_TARGET) / HY_SLOW_PCT,
                                math.log(HY_DECAY_TARGET) / HY_FAST_PCT, HY_WIDTH, dtype=np.float32))
    dl = np.tile(deltas, 2)[None, :]
    args = (jnp.asarray(bands),
            _pad2(w1[0:1], 1, P), _pad2(w1[1:1 + HY_BANDS], P, P), _pad2(w1[1 + HY_BANDS:], P, P),
            _pad2(b1[None], 1, P), _pad2(fr1[None], 1, P),
            _pad2(w2, P, P), _pad2(b2[None], 1, P), _pad2(fr2[None], 1, P),
            _pad2(w3, P, 2 * HY_WIDTH), jnp.asarray(dl))
    return pl.pallas_call(
        functools.partial(_filt_kernel, seq_len=seq_len, tl=tl),
        grid=(seq_len // tl,),
        in_specs=[_const_spec(a.shape) for a in args],
        out_specs=[pl.BlockSpec((2, tl, HY_WIDTH), lambda i: (0, i, 0)),
                   _const_spec((1, 2 * HY_WIDTH))],
        out_shape=[jax.ShapeDtypeStruct((2, seq_len, HY_WIDTH), BF16),
                   jax.ShapeDtypeStruct((1, 2 * HY_WIDTH), F32)],
        compiler_params=_cparams(("arbitrary",)),
        name="filt",
    )(*args)


HALO = 16


def _hy_gate_kernel(x0_ref, x1_ref, v_ref, w0_ref, w1_ref, wv_ref, b0_ref, b1_ref, bv_ref,
                    vg_ref, x0c_ref, *, rows):
    seq_len = x0_ref.shape[1]
    ridx = lax.broadcasted_iota(jnp.int32, (rows, LANES), 0)

    def conv(ref, w_ref, b_ref, r0):
        cur = ref[0, pl.ds(r0, rows), :].astype(F32)
        lo = jnp.maximum(r0 - HALO, 0)
        hi = jnp.minimum(r0 + rows, seq_len - HALO)
        before = ref[0, pl.ds(pl.multiple_of(lo, HALO), HALO), :].astype(F32)[HALO - 1:HALO]
        after = ref[0, pl.ds(pl.multiple_of(hi, HALO), HALO), :].astype(F32)[0:1]
        before = jnp.where(r0 > 0, before, 0.0)
        after = jnp.where(r0 + rows < seq_len, after, 0.0)
        prev = jnp.where(ridx == 0, before, pltpu.roll(cur, 1, 0))
        nxt = jnp.where(ridx == rows - 1, after, pltpu.roll(cur, rows - 1, 0))
        w = w_ref[...]
        return prev * w[0:1] + cur * w[1:2] + nxt * w[2:3] + b_ref[...]

    def body(c, carry):
        r0 = pl.multiple_of(c * rows, rows)
        x0 = conv(x0_ref, w0_ref, b0_ref, r0)
        x1 = conv(x1_ref, w1_ref, b1_ref, r0)
        v = conv(v_ref, wv_ref, bv_ref, r0)
        vg_ref[0, pl.ds(r0, rows), :] = (v * x1).astype(BF16)
        x0c_ref[0, pl.ds(r0, rows), :] = x0.astype(BF16)
        return carry

    lax.fori_loop(0, seq_len // rows, body, 0)


def _hy_gate(u, conv_w, conv_b):
    B, L, _ = u.shape
    nb = HY_WIDTH // LANES
    col = lambda g: pl.BlockSpec((1, L, LANES), lambda b, j: (b, 0, g * nb + j))
    wsp = lambda g: pl.BlockSpec((3, LANES), lambda b, j: (0, g * nb + j))
    bsp = lambda g: pl.BlockSpec((1, LANES), lambda b, j: (0, g * nb + j))
    osp = pl.BlockSpec((1, L, LANES), lambda b, j: (b, 0, j))
    st = jax.ShapeDtypeStruct((B, L, HY_WIDTH), BF16)
    return pl.pallas_call(
        functools.partial(_hy_gate_kernel, rows=256),
        grid=(B, nb),
        in_specs=[col(0), col(1), col(2), wsp(0), wsp(1), wsp(2), bsp(0), bsp(1), bsp(2)],
        out_specs=[osp, osp],
        out_shape=[st, st],
        compiler_params=_cparams(("parallel", "parallel")),
        name="hy_gate",
    )(u, u, u, conv_w, conv_w, conv_w, conv_b, conv_b, conv_b)


def _dft_tables(seq_len):
    n = 2 * seq_len
    n1_full = n // DFT_N2
    n1_used = n1_full // 2
    nk1 = n1_full // 2 + 1
    assert nk1 <= DFT_KPAD
    k1 = np.arange(nk1)[:, None]
    n1 = np.arange(n1_used)[None, :]
    th = 2.0 * np.pi * ((k1 * n1) % n1_full) / n1_full
    fa = np.zeros((2 * DFT_KPAD, n1_used))
    fa[:nk1] = np.cos(th)
    fa[DFT_KPAD:DFT_KPAD + nk1] = -np.sin(th)
    ck = np.full((nk1, 1), 2.0)
    ck[0, 0] = 1.0
    ck[-1, 0] = 1.0
    far = np.zeros((n1_used, DFT_KPAD))
    fai = np.zeros((n1_used, DFT_KPAD))
    far[:, :nk1] = (ck * np.cos(th) / n).T
    fai[:, :nk1] = (-ck * np.sin(th) / n).T
    k1 = np.arange(nk1)[:, None, None]
    k2 = np.arange(DFT_N2)[None, :, None]
    n2 = np.arange(DFT_N2)[None, None, :]
    ph = 2.0 * np.pi * ((n1_full * n2 * k2 + n2 * k1) % n) / n
    gr, gi = np.cos(ph), -np.sin(ph)
    g = np.concatenate([np.concatenate([gr, -gi], 2), np.concatenate([gi, gr], 2)], 1)
    grt, git = np.swapaxes(gr, 1, 2), -np.swapaxes(gi, 1, 2)
    ginv = np.concatenate([np.concatenate([grt, -git], 2), np.concatenate([git, grt], 2)], 1)
    f = lambda a: jnp.asarray(a.astype(np.float32))
    return f(fa), f(far), f(fai), f(g), f(ginv), nk1


def _dft_a_kernel(x_ref, fa_ref, re_ref, im_ref):
    a = jnp.dot(fa_ref[...], x_ref[0], preferred_element_type=F32)
    re_ref[0] = a[:DFT_KPAD].astype(BF16)
    im_ref[0] = a[DFT_KPAD:].astype(BF16)


def _dft_a(xv, fa, tn):
    B, K, M = xv.shape
    st = jax.ShapeDtypeStruct((B, DFT_KPAD, M), BF16)
    osp = pl.BlockSpec((1, DFT_KPAD, tn), lambda b, j: (b, 0, j))
    return pl.pallas_call(
        _dft_a_kernel,
        grid=(B, M // tn),
        in_specs=[pl.BlockSpec((1, K, tn), lambda b, j: (b, 0, j)), _const_spec(fa.shape)],
        out_specs=[osp, osp],
        out_shape=[st, st],
        compiler_params=_cparams(("parallel", "parallel")),
        name="dft_a",
    )(xv, fa)


def _dft_c_kernel(are_ref, aim_ref, hre_ref, him_ref, g_ref, gi_ref, asum_ref, bre_ref, bim_ref, *, nk1):
    k1 = pl.program_id(0)
    half = DFT_N2

    @pl.when(k1 < nk1)
    def _():
        g = g_ref[0].astype(BF16)
        gi = gi_ref[0].astype(BF16)

        def fwd(ar, ai):
            return jnp.dot(g, jnp.concatenate([ar, ai], axis=0), preferred_element_type=F32)

        asum = asum_ref[...]
        inv = 1.0 / (asum[:, :HY_WIDTH] + asum[:, HY_WIDTH:] + 1e-6)
        xf = fwd(hre_ref[0, 0], him_ref[0, 0])
        xb = fwd(hre_ref[1, 0], him_ref[1, 0])
        kr = (xf[:half] + xb[:half]) * inv
        ki = (xf[half:] - xb[half:]) * inv
        for b in range(are_ref.shape[0]):
            x = fwd(are_ref[b, 0], aim_ref[b, 0])
            xr, xi = x[:half], x[half:]
            y = jnp.concatenate([xr * kr - xi * ki, xr * ki + xi * kr], axis=0).astype(BF16)
            bk = jnp.dot(gi, y, preferred_element_type=F32)
            bre_ref[b, 0] = bk[:half].astype(BF16)
            bim_ref[b, 0] = bk[half:].astype(BF16)

    @pl.when(k1 >= nk1)
    def _():
        bre_ref[...] = jnp.zeros_like(bre_ref)
        bim_ref[...] = jnp.zeros_like(bim_ref)


def _dft_c(are, aim, hre, him, g, ginv, asum, nk1):
    B, _, n2, C = are.shape
    last = nk1 - 1
    dspec = lambda nb: pl.BlockSpec((nb, 1, n2, C), lambda k: (0, jnp.minimum(k, last), 0, 0))
    gspec = pl.BlockSpec((1, 2 * n2, 2 * n2), lambda k: (jnp.minimum(k, last), 0, 0))
    ospec = pl.BlockSpec((B, 1, n2, C), lambda k: (0, k, 0, 0))
    st = jax.ShapeDtypeStruct(are.shape, BF16)
    return pl.pallas_call(
        functools.partial(_dft_c_kernel, nk1=nk1),
        grid=(DFT_KPAD,),
        in_specs=[dspec(B), dspec(B), dspec(2), dspec(2), gspec, gspec, _const_spec(asum.shape)],
        out_specs=[ospec, ospec],
        out_shape=[st, st],
        compiler_params=_cparams(("parallel",)),
        name="dft_c",
    )(are, aim, hre, him, g, ginv, asum)


def _idft_a_kernel(bre_ref, bim_ref, far_ref, fai_ref, vg_ref, x0_ref, bias_ref, o_ref):
    y = (jnp.dot(far_ref[...], bre_ref[0], preferred_element_type=F32)
         + jnp.dot(fai_ref[...], bim_ref[0], preferred_element_type=F32))
    y = y + vg_ref[0].astype(F32) * bias_ref[...]
    o_ref[0] = (y * x0_ref[0].astype(F32)).astype(BF16)


def _idft_a(bre, bim, far, fai, vgv, x0v, bias_t, tn):
    B, K, M = vgv.shape
    bsp = pl.BlockSpec((1, DFT_KPAD, tn), lambda b, j: (b, 0, j))
    dsp = pl.BlockSpec((1, K, tn), lambda b, j: (b, 0, j))
    return pl.pallas_call(
        _idft_a_kernel,
        grid=(B, M // tn),
        in_specs=[bsp, bsp, _const_spec(far.shape), _const_spec(fai.shape), dsp, dsp, _const_spec(bias_t.shape)],
        out_specs=dsp,
        out_shape=jax.ShapeDtypeStruct((B, K, M), BF16),
        compiler_params=_cparams(("parallel", "parallel")),
        name="idft_a",
    )(bre, bim, far, fai, vgv, x0v, bias_t)


def _retention_kernel(q_ref, k_ref, v_ref, g_ref, cos_ref, sin_ref, lgf_ref, lgb_ref, s0f_ref, s0b_ref,
                      gn_ref, o_ref, qr_ref, kr_ref, sb_ref, dm_ref, wt_ref):
    C = RET_CHUNK
    seq_len = q_ref.shape[1]
    nc = seq_len // C
    lgf = lgf_ref[0]
    lgb = lgb_ref[0]
    lane = lax.broadcasted_iota(jnp.int32, (1, LANES), 1)
    head_of_lane = lane // RET_HEAD_DIM
    even = (lane % 2) == 0
    bd = _block_diag_mask()

    ri = lax.broadcasted_iota(jnp.int32, (C, LANES), 0).astype(F32)
    wt_ref[0] = jnp.exp(lgf * (ri + 1.0))
    wt_ref[1] = jnp.exp(lgb * (C - ri))
    wt_ref[2] = jnp.exp(lgf * (C - 1.0 - ri))
    wt_ref[3] = jnp.exp(lgb * ri)
    di = lax.broadcasted_iota(jnp.int32, (C, C), 0)
    dj = lax.broadcasted_iota(jnp.int32, (C, C), 1)
    dd = (di - dj).astype(F32)
    for h in range(HEADS_PER_BLOCK):
        lf = jnp.sum(jnp.where(lane == h * RET_HEAD_DIM, lgf, 0.0), axis=1, keepdims=True)
        lb = jnp.sum(jnp.where(lane == h * RET_HEAD_DIM, lgb, 0.0), axis=1, keepdims=True)
        dm_ref[h] = jnp.where(dd > 0, jnp.exp(lf * dd), jnp.where(dd < 0, jnp.exp(-lb * dd), 2.0))

    def chunk(c):
        return pl.ds(pl.multiple_of(c * C, C), C)

    def rope_body(c, carry):
        rows = chunk(c)
        cs = cos_ref[rows, :]
        sn = sin_ref[rows, :]
        for src, dst, scale in ((q_ref, qr_ref, 1.0), (k_ref, kr_ref, RET_HEAD_DIM ** -0.5)):
            t = src[0, rows, :].astype(F32)
            sw = jnp.where(even, pltpu.roll(t, LANES - 1, 1), pltpu.roll(t, 1, 1))
            dst[rows, :] = ((t * cs + sw * sn) * scale).astype(BF16)
        return carry

    lax.fori_loop(0, nc, rope_body, 0)

    gcf = jnp.exp(lgf * float(C))
    gcb = jnp.exp(lgb * float(C))

    def bwd_body(i, s):
        c = nc - 1 - i
        rows = chunk(c)
        sb_ref[c] = s
        kw = (kr_ref[rows, :].astype(F32) * wt_ref[3]).astype(BF16)
        return s * gcb + jnp.where(bd, _kt_v(kw, v_ref[0, rows, :]), 0.0)

    lax.fori_loop(0, nc, bwd_body, s0b_ref[0, 0])

    gn = gn_ref[...]

    def fwd_body(c, s):
        rows = chunk(c)
        q = qr_ref[rows, :]
        k = kr_ref[rows, :]
        v = v_ref[0, rows, :]
        qf = q.astype(F32)
        o = (jnp.dot((qf * wt_ref[0]).astype(BF16), s.astype(BF16), preferred_element_type=F32)
             + jnp.dot((qf * wt_ref[1]).astype(BF16), sb_ref[c].astype(BF16), preferred_element_type=F32))
        for h in range(HEADS_PER_BLOCK):
            mine = head_of_lane == h
            qm = jnp.where(mine, q, jnp.zeros_like(q))
            sc = lax.dot_general(qm, k, (((1,), (1,)), ((), ())), preferred_element_type=F32)
            oh = jnp.dot((sc * dm_ref[h]).astype(BF16), v, preferred_element_type=F32)
            o = o + jnp.where(mine, oh, 0.0)
        first = head_of_lane == 0
        inv_n = 1.0 / RET_HEAD_DIM

        def head_mean(a):
            tot = jnp.sum(a, axis=1, keepdims=True)
            s0 = jnp.sum(jnp.where(first, a, 0.0), axis=1, keepdims=True)
            return jnp.where(first, s0, tot - s0) * inv_n

        d = o - head_mean(o)
        y = d * lax.rsqrt(head_mean(d * d) + NORM_EPS) * gn
        o_ref[0, rows, :] = (y * _silu(g_ref[0, rows, :].astype(F32))).astype(BF16)
        kw = (k.astype(F32) * wt_ref[2]).astype(BF16)
        return s * gcf + jnp.where(bd, _kt_v(kw, v), 0.0)

    lax.fori_loop(0, nc, fwd_body, s0f_ref[0, 0])


def _retention(u, cos_t, sin_t, lgf, lgb, s0f, s0b, gn):
    B, L, _ = u.shape
    assert HEADS_PER_BLOCK == 2
    base = HY_PROJ // LANES
    col = lambda g: pl.BlockSpec((1, L, LANES), lambda b, p: (b, 0, base + g * RET_BLOCKS + p))
    tab = pl.BlockSpec((L, LANES), lambda b, p: (0, 0), pipeline_mode=pl.Buffered(1))
    lsp = pl.BlockSpec((1, 1, LANES), lambda b, p: (p, 0, 0))
    ssp = pl.BlockSpec((1, 1, LANES, LANES), lambda b, p: (b, p, 0, 0))
    return pl.pallas_call(
        _retention_kernel,
        grid=(B, RET_BLOCKS),
        in_specs=[col(0), col(1), col(2), col(3), tab, tab, lsp, lsp, ssp, ssp,
                  pl.BlockSpec((1, LANES), lambda b, p: (0, p))],
        out_specs=pl.BlockSpec((1, L, LANES), lambda b, p: (b, 0, p)),
        out_shape=jax.ShapeDtypeStruct((B, L, RET_WIDTH), BF16),
        scratch_shapes=[pltpu.VMEM((L, LANES), BF16), pltpu.VMEM((L, LANES), BF16),
                        pltpu.VMEM((L // RET_CHUNK, LANES, LANES), F32),
                        pltpu.VMEM((HEADS_PER_BLOCK, RET_CHUNK, RET_CHUNK), F32),
                        pltpu.VMEM((4, RET_CHUNK, LANES), F32)],
        compiler_params=_cparams(("parallel", "parallel")),
        name="retention",
    )(u, u, u, u, cos_t, sin_t, lgf, lgb, s0f, s0b, gn)


def _rope_tables(seq_len):
    rows = seq_len // GRID_W
    r, col = jnp.meshgrid(jnp.arange(rows, dtype=F32), jnp.arange(GRID_W, dtype=F32), indexing="ij")
    inv = ROPE_BASE ** (-jnp.arange(ROPE_PAIRS_AXIS, dtype=F32) / ROPE_PAIRS_AXIS)
    ang = jnp.concatenate([r.reshape(-1, 1) * inv, col.reshape(-1, 1) * inv], axis=-1)
    ang = jnp.tile(jnp.repeat(ang, 2, axis=1), (1, HEADS_PER_BLOCK))
    sign = jnp.where(jnp.arange(LANES) % 2 == 0, -1.0, 1.0).astype(F32)
    return jnp.cos(ang), jnp.sin(ang) * sign


def _out_mlp_kernel(x_ref, yh_ref, yr_ref, wo1_ref, wo2_ref, w1_ref, w2_ref, g2_ref, gf_ref, mod_ref, o_ref,
                    *, ff_chunk):
    mod = mod_ref[0]
    mix = (jnp.dot(yh_ref[0], wo1_ref[...], preferred_element_type=F32)
           + jnp.dot(yr_ref[0], wo2_ref[...], preferred_element_type=F32))
    x1 = x_ref[0] + mod[0:1] * mix
    h = (_rms(x1) * g2_ref[...] * (1.0 + mod[2:3]) + mod[1:2]).astype(BF16)
    acc = jnp.zeros_like(x1)
    for j in range(w1_ref.shape[1] // ff_chunk):
        cols = slice(j * ff_chunk, (j + 1) * ff_chunk)
        a = jnp.maximum(jnp.dot(h, w1_ref[:, cols], preferred_element_type=F32), 0.0)
        acc = acc + jnp.dot((a * a).astype(BF16), w2_ref[cols, :], preferred_element_type=F32)
    x2 = x1 + mod[3:4] * acc
    o_ref[0] = _rms(x2) * gf_ref[...]


def _out_mlp(x, yh, yr, wo1, wo2, w1, w2, g2, gf, mod, tm):
    B, L, D = x.shape
    dff = w1.shape[1]
    row = lambda n: pl.BlockSpec((1, tm, n), lambda b, i: (b, i, 0))
    wsp = lambda a: pl.BlockSpec(a.shape, lambda b, i: (0, 0), pipeline_mode=pl.Buffered(1))
    return pl.pallas_call(
        functools.partial(_out_mlp_kernel, ff_chunk=1024),
        grid=(B, L // tm),
        in_specs=[row(D), row(yh.shape[2]), row(yr.shape[2]), wsp(wo1), wsp(wo2), wsp(w1), wsp(w2),
                  _const_spec((1, D)), _const_spec((1, D)),
                  pl.BlockSpec((1, 4, D), lambda b, i: (b, 0, 0))],
        out_specs=row(D),
        out_shape=jax.ShapeDtypeStruct((B, L, D), F32),
        compiler_params=_cparams(("parallel", "parallel")),
        name="out_mlp",
    )(x, yh, yr, wo1, wo2, w1, w2, g2, gf, mod)


def kernel(x, c, ctx, c_ctx, w_ada, b_ada, norm1_g, w_in, hy_conv_w, hy_conv_b, hy_f_w1, hy_f_b1, hy_f_freq1,
           hy_f_w2, hy_f_b2, hy_f_freq2, hy_f_w3, hy_bias, ret_decay_logit, ret_gn_g, w_out, norm2_g, w_mlp1,
           w_mlp2, norm_f_g):
    B, L, D = x.shape
    depth = w_ada.shape[0]
    assert depth == 1, "single-layer block"
    l = 0
    rows = 16
    cvec = jnp.zeros((rows, D), F32).at[:B].set(c).at[B].set(c_ctx)
    mod = _ada(cvec, w_ada[l], b_ada[l][None])
    mx = mod[:B].reshape(B, 6, D)
    mc = mod[B].reshape(6, D)

    lg = jax.nn.log_sigmoid(ret_decay_logit[l].astype(F32))
    lg_lanes = jnp.repeat(lg, RET_HEAD_DIM, axis=1).reshape(2, RET_BLOCKS, 1, LANES)
    lgf, lgb = lg_lanes[0], lg_lanes[1]

    w_in_b = w_in[l].astype(BF16)
    g1 = norm1_g[l][None]
    u = _inproj(x, g1, mx[:, 0:1], mx[:, 1:2], w_in_b, tm=512)
    ones = jnp.ones((B, 1, 1), F32)
    uc = _inproj(ctx, g1, ones * mc[0], ones * mc[1], w_in_b, tm=ctx.shape[1])
    s0f, s0b = _ctx_states(uc, lgf, lgb)

    fa, far, fai, gmat, ginv, nk1 = _dft_tables(L)
    fa, far, fai = fa.astype(BF16), far.astype(BF16), fai.astype(BF16)
    taps, asum = _filters(L, hy_f_w1[l], hy_f_b1[l], hy_f_freq1[l], hy_f_w2[l], hy_f_b2[l], hy_f_freq2[l],
                          hy_f_w3[l])
    vg, x0c = _hy_gate(u, hy_conv_w[l], hy_conv_b[l][None])
    n1u = L // DFT_N2
    m = DFT_N2 * HY_WIDTH
    tn = 8192
    are, aim = _dft_a(vg.reshape(B, n1u, m), fa, tn)
    hre, him = _dft_a(taps.reshape(2, n1u, m), fa, tn)
    shp = lambda a: a.reshape(a.shape[0], DFT_KPAD, DFT_N2, HY_WIDTH)
    bre, bim = _dft_c(shp(are), shp(aim), shp(hre), shp(him), gmat, ginv, asum, nk1)
    bias_t = jnp.tile(hy_bias[l].astype(F32), tn // HY_WIDTH)[None]
    y_hy = _idft_a(bre.reshape(B, DFT_KPAD, m), bim.reshape(B, DFT_KPAD, m), far, fai,
                   vg.reshape(B, n1u, m), x0c.reshape(B, n1u, m), bias_t, tn).reshape(B, L, HY_WIDTH)

    cos_t, sin_t = _rope_tables(L)
    y_ret = _retention(u, cos_t, sin_t, lgf, lgb, s0f, s0b, ret_gn_g[l][None].astype(F32))

    w_out_b = w_out[l].astype(BF16)
    mod2 = mx[:, 2:6]
    return _out_mlp(x, y_hy, y_ret, w_out_b[:HY_WIDTH], w_out_b[HY_WIDTH:], w_mlp1[l].astype(BF16),
                    w_mlp2[l].astype(BF16), norm2_g[l][None], norm_f_g[None], mod2, tm=512)
```

```python
import functools
import math

import numpy as np
import jax
import jax.numpy as jnp
from jax import lax
from jax.experimental import pallas as pl
from jax.experimental.pallas import tpu as pltpu

F32 = jnp.float32
BF16 = jnp.bfloat16
HIGHEST = lax.Precision.HIGHEST

GRID_W = 64
HY_WIDTH = 512
RET_WIDTH = 512
RET_HEADS = 8
RET_HEAD_DIM = RET_WIDTH // RET_HEADS
HY_PROJ = 3 * HY_WIDTH
HY_BANDS = 16
HY_FILT_HID = 64
HY_DECAY_TARGET = 1e-2
HY_FAST_PCT = 0.3
HY_SLOW_PCT = 1.5
ROPE_BASE = 10000.0
ROPE_PAIRS_AXIS = RET_HEAD_DIM // 4
NORM_EPS = 1e-6

LANES = 128
HEADS_PER_BLOCK = LANES // RET_HEAD_DIM
RET_BLOCKS = RET_WIDTH // LANES
RET_CHUNK = 256
DFT_N2 = 128
DFT_KPAD = 80
VMEM_LIMIT = 56 * 1024 * 1024


def _cparams(sem):
    return pltpu.CompilerParams(dimension_semantics=sem, vmem_limit_bytes=VMEM_LIMIT)


def _const_spec(shape):
    nd = len(shape)
    return pl.BlockSpec(shape, lambda *_: (0,) * nd)


def _silu(x):
    return x * jax.nn.sigmoid(x)


def _rms(x):
    return x * lax.rsqrt(jnp.mean(x * x, axis=-1, keepdims=True) + NORM_EPS)


def _ada_kernel(c_ref, w_ref, b_ref, o_ref):
    s = _silu(c_ref[...])
    o_ref[...] = jnp.dot(s, w_ref[...], preferred_element_type=F32, precision=HIGHEST) + b_ref[...]


def _ada(cvec, w, b):
    R, D = cvec.shape
    N = w.shape[1]
    return pl.pallas_call(
        _ada_kernel,
        grid=(N // D,),
        in_specs=[_const_spec((R, D)), pl.BlockSpec((D, D), lambda j: (0, j)),
                  pl.BlockSpec((1, D), lambda j: (0, j))],
        out_specs=pl.BlockSpec((R, D), lambda j: (0, j)),
        out_shape=jax.ShapeDtypeStruct((R, N), F32),
        compiler_params=_cparams(("arbitrary",)),
        name="ada",
    )(cvec, w, b)


INPROJ_NCHUNK = 512


def _inproj_kernel(x_ref, g_ref, sh_ref, sc_ref, w_ref, *rest, rope):
    if rope:
        cos_ref, sin_ref, o_ref = rest
        even = (lax.broadcasted_iota(jnp.int32, (1, LANES), 1) % 2) == 0
    else:
        (o_ref,) = rest
    h = _rms(x_ref[0]) * g_ref[...]
    hb = (h * (1.0 + sc_ref[0]) + sh_ref[0]).astype(BF16)
    for c0 in range(0, w_ref.shape[1], INPROJ_NCHUNK):
        cols = slice(c0, c0 + INPROJ_NCHUNK)
        r = jnp.dot(hb, w_ref[:, cols], preferred_element_type=F32)
        if rope and HY_PROJ <= c0 < HY_PROJ + 2 * RET_WIDTH:
            is_key = c0 >= HY_PROJ + RET_WIDTH
            cs, sn = cos_ref[...], sin_ref[...]
            parts = []
            for b0 in range(0, INPROJ_NCHUNK, LANES):
                t = r[:, b0:b0 + LANES]
                sw = jnp.where(even, pltpu.roll(t, LANES - 1, 1), pltpu.roll(t, 1, 1))
                t = t * cs + sw * sn
                parts.append(t * (RET_HEAD_DIM ** -0.5) if is_key else t)
            r = jnp.concatenate(parts, axis=1)
        o_ref[0, :, cols] = r.astype(BF16)


def _inproj(x, g, shift, scale, w, tm, rope=None):
    B, L, D = x.shape
    N = w.shape[1]
    in_specs = [pl.BlockSpec((1, tm, D), lambda b, i: (b, i, 0)),
                _const_spec((1, D)),
                pl.BlockSpec((1, 1, D), lambda b, i: (b, 0, 0)),
                pl.BlockSpec((1, 1, D), lambda b, i: (b, 0, 0)),
                pl.BlockSpec((D, N), lambda b, i: (0, 0), pipeline_mode=pl.Buffered(1))]
    args = [x, g, shift, scale, w]
    if rope is not None:
        in_specs += [pl.BlockSpec((tm, LANES), lambda b, i: (i, 0))] * 2
        args += list(rope)
    return pl.pallas_call(
        functools.partial(_inproj_kernel, rope=rope is not None),
        grid=(B, L // tm),
        in_specs=in_specs,
        out_specs=pl.BlockSpec((1, tm, N), lambda b, i: (b, i, 0)),
        out_shape=jax.ShapeDtypeStruct((B, L, N), BF16),
        compiler_params=_cparams(("parallel", "parallel")),
        name="inproj",
    )(*args)


def _block_diag_mask():
    r = lax.broadcasted_iota(jnp.int32, (LANES, LANES), 0) // RET_HEAD_DIM
    c = lax.broadcasted_iota(jnp.int32, (LANES, LANES), 1) // RET_HEAD_DIM
    return r == c


def _kt_v(k, v):
    return lax.dot_general(k, v, (((0,), (0,)), ((), ())), preferred_element_type=F32)


def _ctx_state_kernel(k_ref, v_ref, lgf_ref, lgb_ref, sf_ref, sb_ref):
    k = k_ref[0].astype(F32) * (RET_HEAD_DIM ** -0.5)
    v = v_ref[0]
    lc = k.shape[0]
    m = lax.broadcasted_iota(jnp.int32, (lc, LANES), 0).astype(F32)
    wf = jnp.exp(lgf_ref[0] * (lc - 1.0 - m))
    wb = jnp.exp(lgb_ref[0] * m)
    bd = _block_diag_mask()
    sf_ref[0, 0] = jnp.where(bd, _kt_v((k * wf).astype(BF16), v), 0.0)
    sb_ref[0, 0] = jnp.where(bd, _kt_v((k * wb).astype(BF16), v), 0.0)


def _ctx_states(uc, lgf, lgb):
    B, lc, _ = uc.shape
    kb = (HY_PROJ + RET_WIDTH) // LANES
    vb = (HY_PROJ + 2 * RET_WIDTH) // LANES
    st = jax.ShapeDtypeStruct((B, RET_BLOCKS, LANES, LANES), F32)
    sspec = pl.BlockSpec((1, 1, LANES, LANES), lambda b, p: (b, p, 0, 0))
    return pl.pallas_call(
        _ctx_state_kernel,
        grid=(B, RET_BLOCKS),
        in_specs=[pl.BlockSpec((1, lc, LANES), lambda b, p: (b, 0, kb + p)),
                  pl.BlockSpec((1, lc, LANES), lambda b, p: (b, 0, vb + p)),
                  pl.BlockSpec((1, 1, LANES), lambda b, p: (p, 0, 0)),
                  pl.BlockSpec((1, 1, LANES), lambda b, p: (p, 0, 0))],
        out_specs=[sspec, sspec],
        out_shape=[st, st],
        compiler_params=_cparams(("parallel", "parallel")),
        name="ctx_state",
    )(uc, uc, lgf, lgb)


def _filt_kernel(bands_ref, w1t_ref, w1c_ref, w1s_ref, b1_ref, fr1_ref, w2_ref, b2_ref, fr2_ref,
                 w3_ref, dl_ref, h_ref, asum_ref, *, seq_len, tl):
    i = pl.program_id(0)
    n = (lax.broadcasted_iota(jnp.int32, (tl, 1), 0) + i * tl).astype(F32)
    t = n / (seq_len - 1.0)
    w = (2.0 * math.pi / seq_len) * n
    arg = bands_ref[...] * w
    dot = functools.partial(jnp.dot, preferred_element_type=F32, precision=HIGHEST)
    pre = t * w1t_ref[...] + dot(jnp.cos(arg), w1c_ref[...]) - dot(jnp.sin(arg), w1s_ref[...]) + b1_ref[...]
    h = jnp.sin(fr1_ref[...] * pre)
    h = jnp.sin(fr2_ref[...] * (dot(h, w2_ref[...]) + b2_ref[...]))
    h = dot(h, w3_ref[...]) * jnp.exp(-t * dl_ref[...])

    @pl.when(i == 0)
    def _():
        asum_ref[...] = jnp.zeros_like(asum_ref)

    asum_ref[...] += jnp.sum(jnp.abs(h), axis=0, keepdims=True)
    h_ref[0] = h[:, :HY_WIDTH].astype(BF16)
    h_ref[1] = jnp.where(n == 0.0, 0.0, h[:, HY_WIDTH:]).astype(BF16)


def _pad2(a, rows, cols):
    return jnp.pad(a.astype(F32), ((0, rows - a.shape[0]), (0, cols - a.shape[1])))


def _filters(seq_len, w1, b1, fr1, w2, b2, fr2, w3):
    tl = 1024
    P = LANES
    bands = np.zeros((1, P), np.float32)
    bands[0, :HY_BANDS] = np.linspace(1e-4, HY_BANDS - 1, HY_BANDS, dtype=np.float32)
    deltas = np.abs(np.linspace(math.log(HY_DECAY_TARGET) / HY_SLOW_PCT,
                                math.log(HY_DECAY_TARGET) / HY_FAST_PCT, HY_WIDTH, dtype=np.float32))
    dl = np.tile(deltas, 2)[None, :]
    args = (jnp.asarray(bands),
            _pad2(w1[0:1], 1, P), _pad2(w1[1:1 + HY_BANDS], P, P), _pad2(w1[1 + HY_BANDS:], P, P),
            _pad2(b1[None], 1, P), _pad2(fr1[None], 1, P),
            _pad2(w2, P, P), _pad2(b2[None], 1, P), _pad2(fr2[None], 1, P),
            _pad2(w3, P, 2 * HY_WIDTH), jnp.asarray(dl))
    return pl.pallas_call(
        functools.partial(_filt_kernel, seq_len=seq_len, tl=tl),
        grid=(seq_len // tl,),
        in_specs=[_const_spec(a.shape) for a in args],
        out_specs=[pl.BlockSpec((2, tl, HY_WIDTH), lambda i: (0, i, 0)),
                   _const_spec((1, 2 * HY_WIDTH))],
        out_shape=[jax.ShapeDtypeStruct((2, seq_len, HY_WIDTH), BF16),
                   jax.ShapeDtypeStruct((1, 2 * HY_WIDTH), F32)],
        compiler_params=_cparams(("arbitrary",)),
        name="filt",
    )(*args)


HALO = 16


def _hy_gate_kernel(x0_ref, x1_ref, v_ref, w0_ref, w1_ref, wv_ref, b0_ref, b1_ref, bv_ref,
                    vg_ref, x0c_ref, *, rows):
    seq_len = x0_ref.shape[1]
    ridx = lax.broadcasted_iota(jnp.int32, (rows, LANES), 0)

    def conv(ref, w_ref, b_ref, r0):
        cur = ref[0, pl.ds(r0, rows), :].astype(F32)
        lo = jnp.maximum(r0 - HALO, 0)
        hi = jnp.minimum(r0 + rows, seq_len - HALO)
        before = ref[0, pl.ds(pl.multiple_of(lo, HALO), HALO), :].astype(F32)[HALO - 1:HALO]
        after = ref[0, pl.ds(pl.multiple_of(hi, HALO), HALO), :].astype(F32)[0:1]
        before = jnp.where(r0 > 0, before, 0.0)
        after = jnp.where(r0 + rows < seq_len, after, 0.0)
        prev = jnp.where(ridx == 0, before, pltpu.roll(cur, 1, 0))
        nxt = jnp.where(ridx == rows - 1, after, pltpu.roll(cur, rows - 1, 0))
        w = w_ref[...]
        return prev * w[0:1] + cur * w[1:2] + nxt * w[2:3] + b_ref[...]

    def body(c, carry):
        r0 = pl.multiple_of(c * rows, rows)
        x0 = conv(x0_ref, w0_ref, b0_ref, r0)
        x1 = conv(x1_ref, w1_ref, b1_ref, r0)
        v = conv(v_ref, wv_ref, bv_ref, r0)
        vg_ref[0, pl.ds(r0, rows), :] = (v * x1).astype(BF16)
        x0c_ref[0, pl.ds(r0, rows), :] = x0.astype(BF16)
        return carry

    lax.fori_loop(0, seq_len // rows, body, 0)


def _hy_gate(u, conv_w, conv_b):
    B, L, _ = u.shape
    nb = HY_WIDTH // LANES
    col = lambda g: pl.BlockSpec((1, L, LANES), lambda b, j: (b, 0, g * nb + j))
    wsp = lambda g: pl.BlockSpec((3, LANES), lambda b, j: (0, g * nb + j))
    bsp = lambda g: pl.BlockSpec((1, LANES), lambda b, j: (0, g * nb + j))
    osp = pl.BlockSpec((1, L, LANES), lambda b, j: (b, 0, j))
    st = jax.ShapeDtypeStruct((B, L, HY_WIDTH), BF16)
    return pl.pallas_call(
        functools.partial(_hy_gate_kernel, rows=256),
        grid=(B, nb),
        in_specs=[col(0), col(1), col(2), wsp(0), wsp(1), wsp(2), bsp(0), bsp(1), bsp(2)],
        out_specs=[osp, osp],
        out_shape=[st, st],
        compiler_params=_cparams(("parallel", "parallel")),
        name="hy_gate",
    )(u, u, u, conv_w, conv_w, conv_w, conv_b, conv_b, conv_b)


def _dft_tables(seq_len):
    n = 2 * seq_len
    n1_full = n // DFT_N2
    n1_used = n1_full // 2
    nk1 = n1_full // 2 + 1
    assert nk1 <= DFT_KPAD
    k1 = np.arange(nk1)[:, None]
    n1 = np.arange(n1_used)[None, :]
    th = 2.0 * np.pi * ((k1 * n1) % n1_full) / n1_full
    fa = np.zeros((2 * DFT_KPAD, n1_used))
    fa[:nk1] = np.cos(th)
    fa[DFT_KPAD:DFT_KPAD + nk1] = -np.sin(th)
    ck = np.full((nk1, 1), 2.0)
    ck[0, 0] = 1.0
    ck[-1, 0] = 1.0
    far = np.zeros((n1_used, DFT_KPAD))
    fai = np.zeros((n1_used, DFT_KPAD))
    far[:, :nk1] = (ck * np.cos(th) / n).T
    fai[:, :nk1] = (-ck * np.sin(th) / n).T
    k1 = np.arange(nk1)[:, None, None]
    k2 = np.arange(DFT_N2)[None, :, None]
    n2 = np.arange(DFT_N2)[None, None, :]
    ph = 2.0 * np.pi * ((n1_full * n2 * k2 + n2 * k1) % n) / n
    gr, gi = np.cos(ph), -np.sin(ph)
    g = np.concatenate([np.concatenate([gr, -gi], 2), np.concatenate([gi, gr], 2)], 1)
    grt, git = np.swapaxes(gr, 1, 2), -np.swapaxes(gi, 1, 2)
    ginv = np.concatenate([np.concatenate([grt, -git], 2), np.concatenate([git, grt], 2)], 1)
    f = lambda a: jnp.asarray(a.astype(np.float32))
    return f(fa), f(far), f(fai), f(g), f(ginv), nk1


def _dft_a_kernel(x_ref, fa_ref, re_ref, im_ref):
    a = jnp.dot(fa_ref[...], x_ref[0], preferred_element_type=F32)
    re_ref[0] = a[:DFT_KPAD].astype(BF16)
    im_ref[0] = a[DFT_KPAD:].astype(BF16)


def _dft_a(xv, fa, tn):
    B, K, M = xv.shape
    st = jax.ShapeDtypeStruct((B, DFT_KPAD, M), BF16)
    osp = pl.BlockSpec((1, DFT_KPAD, tn), lambda b, j: (b, 0, j))
    return pl.pallas_call(
        _dft_a_kernel,
        grid=(B, M // tn),
        in_specs=[pl.BlockSpec((1, K, tn), lambda b, j: (b, 0, j)), _const_spec(fa.shape)],
        out_specs=[osp, osp],
        out_shape=[st, st],
        compiler_params=_cparams(("parallel", "parallel")),
        name="dft_a",
    )(xv, fa)


def _dft_c_kernel(are_ref, aim_ref, hre_ref, him_ref, g_ref, gi_ref, asum_ref, bre_ref, bim_ref, *, nk1):
    k1 = pl.program_id(0)
    half = DFT_N2

    @pl.when(k1 < nk1)
    def _():
        g = g_ref[0].astype(BF16)
        gi = gi_ref[0].astype(BF16)

        def fwd(ar, ai):
            return jnp.dot(g, jnp.concatenate([ar, ai], axis=0), preferred_element_type=F32)

        asum = asum_ref[...]
        inv = 1.0 / (asum[:, :HY_WIDTH] + asum[:, HY_WIDTH:] + 1e-6)
        xf = fwd(hre_ref[0, 0], him_ref[0, 0])
        xb = fwd(hre_ref[1, 0], him_ref[1, 0])
        kr = (xf[:half] + xb[:half]) * inv
        ki = (xf[half:] - xb[half:]) * inv
        for b in range(are_ref.shape[0]):
            x = fwd(are_ref[b, 0], aim_ref[b, 0])
            xr, xi = x[:half], x[half:]
            y = jnp.concatenate([xr * kr - xi * ki, xr * ki + xi * kr], axis=0).astype(BF16)
            bk = jnp.dot(gi, y, preferred_element_type=F32)
            bre_ref[b, 0] = bk[:half].astype(BF16)
            bim_ref[b, 0] = bk[half:].astype(BF16)

    @pl.when(k1 >= nk1)
    def _():
        bre_ref[...] = jnp.zeros_like(bre_ref)
        bim_ref[...] = jnp.zeros_like(bim_ref)


def _dft_c(are, aim, hre, him, g, ginv, asum, nk1):
    B, _, n2, C = are.shape
    last = nk1 - 1
    dspec = lambda nb: pl.BlockSpec((nb, 1, n2, C), lambda k: (0, jnp.minimum(k, last), 0, 0))
    gspec = pl.BlockSpec((1, 2 * n2, 2 * n2), lambda k: (jnp.minimum(k, last), 0, 0))
    ospec = pl.BlockSpec((B, 1, n2, C), lambda k: (0, k, 0, 0))
    st = jax.ShapeDtypeStruct(are.shape, BF16)
    return pl.pallas_call(
        functools.partial(_dft_c_kernel, nk1=nk1),
        grid=(DFT_KPAD,),
        in_specs=[dspec(B), dspec(B), dspec(2), dspec(2), gspec, gspec, _const_spec(asum.shape)],
        out_specs=[ospec, ospec],
        out_shape=[st, st],
        compiler_params=_cparams(("parallel",)),
        name="dft_c",
    )(are, aim, hre, him, g, ginv, asum)


def _idft_a_kernel(bre_ref, bim_ref, far_ref, fai_ref, vg_ref, x0_ref, bias_ref, o_ref):
    y = (jnp.dot(far_ref[...], bre_ref[0], preferred_element_type=F32)
         + jnp.dot(fai_ref[...], bim_ref[0], preferred_element_type=F32))
    y = y + vg_ref[0].astype(F32) * bias_ref[...]
    o_ref[0] = (y * x0_ref[0].astype(F32)).astype(BF16)


def _idft_a(bre, bim, far, fai, vgv, x0v, bias_t, tn):
    B, K, M = vgv.shape
    bsp = pl.BlockSpec((1, DFT_KPAD, tn), lambda b, j: (b, 0, j))
    dsp = pl.BlockSpec((1, K, tn), lambda b, j: (b, 0, j))
    return pl.pallas_call(
        _idft_a_kernel,
        grid=(B, M // tn),
        in_specs=[bsp, bsp, _const_spec(far.shape), _const_spec(fai.shape), dsp, dsp, _const_spec(bias_t.shape)],
        out_specs=dsp,
        out_shape=jax.ShapeDtypeStruct((B, K, M), BF16),
        compiler_params=_cparams(("parallel", "parallel")),
        name="idft_a",
    )(bre, bim, far, fai, vgv, x0v, bias_t)


def _retention_kernel(q_ref, k_ref, v_ref, g_ref, lgf_ref, lgb_ref, s0f_ref, s0b_ref, gn_ref, o_ref,
                      inc_ref, st_ref, dm_ref, wt_ref):
    C = RET_CHUNK
    seq_len = q_ref.shape[1]
    nc = seq_len // C
    lgf = lgf_ref[0]
    lgb = lgb_ref[0]
    lane = lax.broadcasted_iota(jnp.int32, (1, LANES), 1)
    head_of_lane = lane // RET_HEAD_DIM
    bd = _block_diag_mask()

    ri = lax.broadcasted_iota(jnp.int32, (C, LANES), 0).astype(F32)
    wt_ref[0] = jnp.exp(lgf * (ri + 1.0))
    wt_ref[1] = jnp.exp(lgb * (C - ri))
    wt_ref[2] = jnp.exp(lgf * (C - 1.0 - ri))
    wt_ref[3] = jnp.exp(lgb * ri)
    di = lax.broadcasted_iota(jnp.int32, (C, C), 0)
    dj = lax.broadcasted_iota(jnp.int32, (C, C), 1)
    dd = (di - dj).astype(F32)
    for h in range(HEADS_PER_BLOCK):
        lf = jnp.sum(jnp.where(lane == h * RET_HEAD_DIM, lgf, 0.0), axis=1, keepdims=True)
        lb = jnp.sum(jnp.where(lane == h * RET_HEAD_DIM, lgb, 0.0), axis=1, keepdims=True)
        dm_ref[h] = jnp.where(dd > 0, jnp.exp(lf * dd), jnp.where(dd < 0, jnp.exp(-lb * dd), 2.0))

    def chunk(c):
        return pl.ds(pl.multiple_of(c * C, C), C)

    def inc_body(c, carry):
        rows = chunk(c)
        kf = k_ref[0, rows, :].astype(F32)
        kw = jnp.concatenate([(kf * wt_ref[2]).astype(BF16), (kf * wt_ref[3]).astype(BF16)], axis=1)
        t = _kt_v(kw, v_ref[0, rows, :])
        inc_ref[c, :LANES] = jnp.where(bd, t[:LANES], 0.0)
        inc_ref[c, LANES:] = jnp.where(bd, t[LANES:], 0.0)
        return carry

    lax.fori_loop(0, nc, inc_body, 0, unroll=2)

    gcf = jnp.exp(lgf * float(C))
    gcb = jnp.exp(lgb * float(C))

    def fscan(c, s):
        st_ref[c, :LANES] = s.astype(BF16)
        return s * gcf + inc_ref[c, :LANES]

    lax.fori_loop(0, nc, fscan, s0f_ref[0, 0])

    def bscan(i, s):
        c = nc - 1 - i
        st_ref[c, LANES:] = s.astype(BF16)
        return s * gcb + inc_ref[c, LANES:]

    lax.fori_loop(0, nc, bscan, s0b_ref[0, 0])

    gn = gn_ref[...]
    first = head_of_lane == 0
    inv_n = 1.0 / RET_HEAD_DIM

    def head_mean(a):
        tot = jnp.sum(a, axis=1, keepdims=True)
        s0 = jnp.sum(jnp.where(first, a, 0.0), axis=1, keepdims=True)
        return jnp.where(first, s0, tot - s0) * inv_n

    def out_body(c, carry):
        rows = chunk(c)
        q = q_ref[0, rows, :]
        k = k_ref[0, rows, :]
        v = v_ref[0, rows, :]
        qf = q.astype(F32)
        qw = jnp.concatenate([(qf * wt_ref[0]).astype(BF16), (qf * wt_ref[1]).astype(BF16)], axis=1)
        o = jnp.dot(qw, st_ref[c], preferred_element_type=F32)
        for h in range(HEADS_PER_BLOCK):
            mine = head_of_lane == h
            qm = jnp.where(mine, q, jnp.zeros_like(q))
            sc = lax.dot_general(qm, k, (((1,), (1,)), ((), ())), preferred_element_type=F32)
            oh = jnp.dot((sc * dm_ref[h]).astype(BF16), v, preferred_element_type=F32)
            o = o + jnp.where(mine, oh, 0.0)
        d = o - head_mean(o)
        y = d * lax.rsqrt(head_mean(d * d) + NORM_EPS) * gn
        o_ref[0, rows, :] = (y * _silu(g_ref[0, rows, :].astype(F32))).astype(BF16)
        return carry

    lax.fori_loop(0, nc, out_body, 0, unroll=2)


def _retention(u, lgf, lgb, s0f, s0b, gn):
    B, L, _ = u.shape
    assert HEADS_PER_BLOCK == 2
    base = HY_PROJ // LANES
    nc = L // RET_CHUNK
    col = lambda g: pl.BlockSpec((1, L, LANES), lambda b, p: (b, 0, base + g * RET_BLOCKS + p))
    lsp = pl.BlockSpec((1, 1, LANES), lambda b, p: (p, 0, 0))
    ssp = pl.BlockSpec((1, 1, LANES, LANES), lambda b, p: (b, p, 0, 0))
    return pl.pallas_call(
        _retention_kernel,
        grid=(B, RET_BLOCKS),
        in_specs=[col(0), col(1), col(2), col(3), lsp, lsp, ssp, ssp,
                  pl.BlockSpec((1, LANES), lambda b, p: (0, p))],
        out_specs=pl.BlockSpec((1, L, LANES), lambda b, p: (b, 0, p)),
        out_shape=jax.ShapeDtypeStruct((B, L, RET_WIDTH), BF16),
        scratch_shapes=[pltpu.VMEM((nc, 2 * LANES, LANES), F32),
                        pltpu.VMEM((nc, 2 * LANES, LANES), BF16),
                        pltpu.VMEM((HEADS_PER_BLOCK, RET_CHUNK, RET_CHUNK), F32),
                        pltpu.VMEM((4, RET_CHUNK, LANES), F32)],
        compiler_params=_cparams(("parallel", "parallel")),
        name="retention",
    )(u, u, u, u, lgf, lgb, s0f, s0b, gn)


def _rope_tables(seq_len):
    rows = seq_len // GRID_W
    r, col = jnp.meshgrid(jnp.arange(rows, dtype=F32), jnp.arange(GRID_W, dtype=F32), indexing="ij")
    inv = ROPE_BASE ** (-jnp.arange(ROPE_PAIRS_AXIS, dtype=F32) / ROPE_PAIRS_AXIS)
    ang = jnp.concatenate([r.reshape(-1, 1) * inv, col.reshape(-1, 1) * inv], axis=-1)
    ang = jnp.tile(jnp.repeat(ang, 2, axis=1), (1, HEADS_PER_BLOCK))
    sign = jnp.where(jnp.arange(LANES) % 2 == 0, -1.0, 1.0).astype(F32)
    return jnp.cos(ang), jnp.sin(ang) * sign


def _out_mlp_kernel(x_ref, yh_ref, yr_ref, wo1_ref, wo2_ref, w1_ref, w2_ref, g2_ref, gf_ref, mod_ref, o_ref,
                    *, ff_chunk):
    mod = mod_ref[0]
    mix = (jnp.dot(yh_ref[0], wo1_ref[...], preferred_element_type=F32)
           + jnp.dot(yr_ref[0], wo2_ref[...], preferred_element_type=F32))
    x1 = x_ref[0] + mod[0:1] * mix
    h = (_rms(x1) * g2_ref[...] * (1.0 + mod[2:3]) + mod[1:2]).astype(BF16)
    acc = jnp.zeros_like(x1)
    for j in range(w1_ref.shape[1] // ff_chunk):
        cols = slice(j * ff_chunk, (j + 1) * ff_chunk)
        a = jnp.maximum(jnp.dot(h, w1_ref[:, cols], preferred_element_type=F32), 0.0)
        acc = acc + jnp.dot((a * a).astype(BF16), w2_ref[cols, :], preferred_element_type=F32)
    x2 = x1 + mod[3:4] * acc
    o_ref[0] = _rms(x2) * gf_ref[...]


def _out_mlp(x, yh, yr, wo1, wo2, w1, w2, g2, gf, mod, tm):
    B, L, D = x.shape
    dff = w1.shape[1]
    row = lambda n: pl.BlockSpec((1, tm, n), lambda b, i: (b, i, 0))
    wsp = lambda a: pl.BlockSpec(a.shape, lambda b, i: (0, 0), pipeline_mode=pl.Buffered(1))
    return pl.pallas_call(
        functools.partial(_out_mlp_kernel, ff_chunk=1024),
        grid=(B, L // tm),
        in_specs=[row(D), row(yh.shape[2]), row(yr.shape[2]), wsp(wo1), wsp(wo2), wsp(w1), wsp(w2),
                  _const_spec((1, D)), _const_spec((1, D)),
                  pl.BlockSpec((1, 4, D), lambda b, i: (b, 0, 0))],
        out_specs=row(D),
        out_shape=jax.ShapeDtypeStruct((B, L, D), F32),
        compiler_params=_cparams(("parallel", "parallel")),
        name="out_mlp",
    )(x, yh, yr, wo1, wo2, w1, w2, g2, gf, mod)


def kernel(x, c, ctx, c_ctx, w_ada, b_ada, norm1_g, w_in, hy_conv_w, hy_conv_b, hy_f_w1, hy_f_b1, hy_f_freq1,
           hy_f_w2, hy_f_b2, hy_f_freq2, hy_f_w3, hy_bias, ret_decay_logit, ret_gn_g, w_out, norm2_g, w_mlp1,
           w_mlp2, norm_f_g):
    B, L, D = x.shape
    depth = w_ada.shape[0]
    assert depth == 1, "single-layer block"
    l = 0
    rows = 16
    cvec = jnp.zeros((rows, D), F32).at[:B].set(c).at[B].set(c_ctx)
    mod = _ada(cvec, w_ada[l], b_ada[l][None])
    mx = mod[:B].reshape(B, 6, D)
    mc = mod[B].reshape(6, D)

    lg = jax.nn.log_sigmoid(ret_decay_logit[l].astype(F32))
    lg_lanes = jnp.repeat(lg, RET_HEAD_DIM, axis=1).reshape(2, RET_BLOCKS, 1, LANES)
    lgf, lgb = lg_lanes[0], lg_lanes[1]

    w_in_b = w_in[l].astype(BF16)
    g1 = norm1_g[l][None]
    u = _inproj(x, g1, mx[:, 0:1], mx[:, 1:2], w_in_b, tm=512, rope=_rope_tables(L))
    ones = jnp.ones((B, 1, 1), F32)
    uc = _inproj(ctx, g1, ones * mc[0], ones * mc[1], w_in_b, tm=ctx.shape[1])
    s0f, s0b = _ctx_states(uc, lgf, lgb)

    fa, far, fai, gmat, ginv, nk1 = _dft_tables(L)
    fa, far, fai = fa.astype(BF16), far.astype(BF16), fai.astype(BF16)
    taps, asum = _filters(L, hy_f_w1[l], hy_f_b1[l], hy_f_freq1[l], hy_f_w2[l], hy_f_b2[l], hy_f_freq2[l],
                          hy_f_w3[l])
    vg, x0c = _hy_gate(u, hy_conv_w[l], hy_conv_b[l][None])
    n1u = L // DFT_N2
    m = DFT_N2 * HY_WIDTH
    tn = 8192
    are, aim = _dft_a(vg.reshape(B, n1u, m), fa, tn)
    hre, him = _dft_a(taps.reshape(2, n1u, m), fa, tn)
    shp = lambda a: a.reshape(a.shape[0], DFT_KPAD, DFT_N2, HY_WIDTH)
    bre, bim = _dft_c(shp(are), shp(aim), shp(hre), shp(him), gmat, ginv, asum, nk1)
    bias_t = jnp.tile(hy_bias[l].astype(F32), tn // HY_WIDTH)[None]
    y_hy = _idft_a(bre.reshape(B, DFT_KPAD, m), bim.reshape(B, DFT_KPAD, m), far, fai,
                   vg.reshape(B, n1u, m), x0c.reshape(B, n1u, m), bias_t, tn).reshape(B, L, HY_WIDTH)

    y_ret = _retention(u, lgf, lgb, s0f, s0b, ret_gn_g[l][None].astype(F32))

    w_out_b = w_out[l].astype(BF16)
    mod2 = mx[:, 2:6]
    return _out_mlp(x, y_hy, y_ret, w_out_b[:HY_WIDTH], w_out_b[HY_WIDTH:], w_mlp1[l].astype(BF16),
                    w_mlp2[l].astype(BF16), norm2_g[l][None], norm_f_g[None], mod2, tm=512)
```

```python
import functools
import math

import numpy as np
import jax
import jax.numpy as jnp
from jax import lax
from jax.experimental import pallas as pl
from jax.experimental.pallas import tpu as pltpu

F32 = jnp.float32
BF16 = jnp.bfloat16
HIGHEST = lax.Precision.HIGHEST

GRID_W = 64
HY_WIDTH = 512
RET_WIDTH = 512
RET_HEADS = 8
RET_HEAD_DIM = RET_WIDTH // RET_HEADS
HY_PROJ = 3 * HY_WIDTH
HY_BANDS = 16
HY_FILT_HID = 64
HY_DECAY_TARGET = 1e-2
HY_FAST_PCT = 0.3
HY_SLOW_PCT = 1.5
ROPE_BASE = 10000.0
ROPE_PAIRS_AXIS = RET_HEAD_DIM // 4
NORM_EPS = 1e-6

LANES = 128
HEADS_PER_BLOCK = LANES // RET_HEAD_DIM
RET_BLOCKS = RET_WIDTH // LANES
RET_CHUNK = 256
DFT_N2 = 128
DFT_ROWS = 16
VMEM_LIMIT = 56 * 1024 * 1024


def _cparams(sem):
    return pltpu.CompilerParams(dimension_semantics=sem, vmem_limit_bytes=VMEM_LIMIT)


def _const_spec(shape):
    nd = len(shape)
    return pl.BlockSpec(shape, lambda *_: (0,) * nd)


def _silu(x):
    return x * jax.nn.sigmoid(x)


def _rms(x):
    return x * lax.rsqrt(jnp.mean(x * x, axis=-1, keepdims=True) + NORM_EPS)


def _ada_kernel(c_ref, w_ref, b_ref, o_ref):
    s = _silu(c_ref[...])
    o_ref[...] = jnp.dot(s, w_ref[...], preferred_element_type=F32, precision=HIGHEST) + b_ref[...]


def _ada(cvec, w, b):
    R, D = cvec.shape
    N = w.shape[1]
    return pl.pallas_call(
        _ada_kernel,
        grid=(N // D,),
        in_specs=[_const_spec((R, D)), pl.BlockSpec((D, D), lambda j: (0, j)),
                  pl.BlockSpec((1, D), lambda j: (0, j))],
        out_specs=pl.BlockSpec((R, D), lambda j: (0, j)),
        out_shape=jax.ShapeDtypeStruct((R, N), F32),
        compiler_params=_cparams(("arbitrary",)),
        name="ada",
    )(cvec, w, b)


INPROJ_NCHUNK = 512


def _inproj_kernel(x_ref, g_ref, sh_ref, sc_ref, w_ref, *rest, rope):
    if rope:
        cos_ref, sin_ref, o_ref = rest
        even = (lax.broadcasted_iota(jnp.int32, (1, LANES), 1) % 2) == 0
    else:
        (o_ref,) = rest
    h = _rms(x_ref[0]) * g_ref[...]
    hb = (h * (1.0 + sc_ref[0]) + sh_ref[0]).astype(BF16)
    for c0 in range(0, w_ref.shape[1], INPROJ_NCHUNK):
        cols = slice(c0, c0 + INPROJ_NCHUNK)
        r = jnp.dot(hb, w_ref[:, cols], preferred_element_type=F32)
        if rope and HY_PROJ <= c0 < HY_PROJ + 2 * RET_WIDTH:
            is_key = c0 >= HY_PROJ + RET_WIDTH
            cs, sn = cos_ref[...], sin_ref[...]
            parts = []
            for b0 in range(0, INPROJ_NCHUNK, LANES):
                t = r[:, b0:b0 + LANES]
                sw = jnp.where(even, pltpu.roll(t, LANES - 1, 1), pltpu.roll(t, 1, 1))
                t = t * cs + sw * sn
                parts.append(t * (RET_HEAD_DIM ** -0.5) if is_key else t)
            r = jnp.concatenate(parts, axis=1)
        o_ref[0, :, cols] = r.astype(BF16)


def _inproj(x, g, shift, scale, w, tm, rope=None):
    B, L, D = x.shape
    N = w.shape[1]
    in_specs = [pl.BlockSpec((1, tm, D), lambda b, i: (b, i, 0)),
                _const_spec((1, D)),
                pl.BlockSpec((1, 1, D), lambda b, i: (b, 0, 0)),
                pl.BlockSpec((1, 1, D), lambda b, i: (b, 0, 0)),
                pl.BlockSpec((D, N), lambda b, i: (0, 0), pipeline_mode=pl.Buffered(1))]
    args = [x, g, shift, scale, w]
    if rope is not None:
        in_specs += [pl.BlockSpec((tm, LANES), lambda b, i: (i, 0))] * 2
        args += list(rope)
    return pl.pallas_call(
        functools.partial(_inproj_kernel, rope=rope is not None),
        grid=(B, L // tm),
        in_specs=in_specs,
        out_specs=pl.BlockSpec((1, tm, N), lambda b, i: (b, i, 0)),
        out_shape=jax.ShapeDtypeStruct((B, L, N), BF16),
        compiler_params=_cparams(("parallel", "parallel")),
        name="inproj",
    )(*args)


def _block_diag_mask():
    r = lax.broadcasted_iota(jnp.int32, (LANES, LANES), 0) // RET_HEAD_DIM
    c = lax.broadcasted_iota(jnp.int32, (LANES, LANES), 1) // RET_HEAD_DIM
    return r == c


def _kt_v(k, v):
    return lax.dot_general(k, v, (((0,), (0,)), ((), ())), preferred_element_type=F32)


def _ctx_state_kernel(k_ref, v_ref, lgf_ref, lgb_ref, sf_ref, sb_ref):
    k = k_ref[0].astype(F32) * (RET_HEAD_DIM ** -0.5)
    v = v_ref[0]
    lc = k.shape[0]
    m = lax.broadcasted_iota(jnp.int32, (lc, LANES), 0).astype(F32)
    wf = jnp.exp(lgf_ref[0] * (lc - 1.0 - m))
    wb = jnp.exp(lgb_ref[0] * m)
    bd = _block_diag_mask()
    sf_ref[0, 0] = jnp.where(bd, _kt_v((k * wf).astype(BF16), v), 0.0)
    sb_ref[0, 0] = jnp.where(bd, _kt_v((k * wb).astype(BF16), v), 0.0)


def _ctx_states(uc, lgf, lgb):
    B, lc, _ = uc.shape
    kb = (HY_PROJ + RET_WIDTH) // LANES
    vb = (HY_PROJ + 2 * RET_WIDTH) // LANES
    st = jax.ShapeDtypeStruct((B, RET_BLOCKS, LANES, LANES), F32)
    sspec = pl.BlockSpec((1, 1, LANES, LANES), lambda b, p: (b, p, 0, 0))
    return pl.pallas_call(
        _ctx_state_kernel,
        grid=(B, RET_BLOCKS),
        in_specs=[pl.BlockSpec((1, lc, LANES), lambda b, p: (b, 0, kb + p)),
                  pl.BlockSpec((1, lc, LANES), lambda b, p: (b, 0, vb + p)),
                  pl.BlockSpec((1, 1, LANES), lambda b, p: (p, 0, 0)),
                  pl.BlockSpec((1, 1, LANES), lambda b, p: (p, 0, 0))],
        out_specs=[sspec, sspec],
        out_shape=[st, st],
        compiler_params=_cparams(("parallel", "parallel")),
        name="ctx_state",
    )(uc, uc, lgf, lgb)


def _filt_kernel(bands_ref, w1t_ref, w1c_ref, w1s_ref, b1_ref, fr1_ref, w2_ref, b2_ref, fr2_ref,
                 w3_ref, dl_ref, h_ref, asum_ref, *, seq_len, tl):
    i = pl.program_id(0)
    n = (lax.broadcasted_iota(jnp.int32, (tl, 1), 0) + i * tl).astype(F32)
    t = n / (seq_len - 1.0)
    w = (2.0 * math.pi / seq_len) * n
    arg = bands_ref[...] * w
    dot = functools.partial(jnp.dot, preferred_element_type=F32, precision=HIGHEST)
    pre = t * w1t_ref[...] + dot(jnp.cos(arg), w1c_ref[...]) - dot(jnp.sin(arg), w1s_ref[...]) + b1_ref[...]
    h = jnp.sin(fr1_ref[...] * pre)
    h = jnp.sin(fr2_ref[...] * (dot(h, w2_ref[...]) + b2_ref[...]))
    h = dot(h, w3_ref[...]) * jnp.exp(-t * dl_ref[...])

    @pl.when(i == 0)
    def _():
        asum_ref[...] = jnp.zeros_like(asum_ref)

    asum_ref[...] += jnp.sum(jnp.abs(h), axis=0, keepdims=True)
    h_ref[0] = h[:, :HY_WIDTH].astype(BF16)
    h_ref[1] = jnp.where(n == 0.0, 0.0, h[:, HY_WIDTH:]).astype(BF16)


def _pad2(a, rows, cols):
    return jnp.pad(a.astype(F32), ((0, rows - a.shape[0]), (0, cols - a.shape[1])))


def _filters(seq_len, w1, b1, fr1, w2, b2, fr2, w3):
    tl = 1024
    P = LANES
    bands = np.zeros((1, P), np.float32)
    bands[0, :HY_BANDS] = np.linspace(1e-4, HY_BANDS - 1, HY_BANDS, dtype=np.float32)
    deltas = np.abs(np.linspace(math.log(HY_DECAY_TARGET) / HY_SLOW_PCT,
                                math.log(HY_DECAY_TARGET) / HY_FAST_PCT, HY_WIDTH, dtype=np.float32))
    dl = np.tile(deltas, 2)[None, :]
    args = (jnp.asarray(bands),
            _pad2(w1[0:1], 1, P), _pad2(w1[1:1 + HY_BANDS], P, P), _pad2(w1[1 + HY_BANDS:], P, P),
            _pad2(b1[None], 1, P), _pad2(fr1[None], 1, P),
            _pad2(w2, P, P), _pad2(b2[None], 1, P), _pad2(fr2[None], 1, P),
            _pad2(w3, P, 2 * HY_WIDTH), jnp.asarray(dl))
    return pl.pallas_call(
        functools.partial(_filt_kernel, seq_len=seq_len, tl=tl),
        grid=(seq_len // tl,),
        in_specs=[_const_spec(a.shape) for a in args],
        out_specs=[pl.BlockSpec((2, tl, HY_WIDTH), lambda i: (0, i, 0)),
                   _const_spec((1, 2 * HY_WIDTH))],
        out_shape=[jax.ShapeDtypeStruct((2, seq_len, HY_WIDTH), BF16),
                   jax.ShapeDtypeStruct((1, 2 * HY_WIDTH), F32)],
        compiler_params=_cparams(("arbitrary",)),
        name="filt",
    )(*args)


HALO = 16


def _hy_gate_kernel(x0_ref, x1_ref, v_ref, w0_ref, w1_ref, wv_ref, b0_ref, b1_ref, bv_ref,
                    vg_ref, x0c_ref, *, rows):
    seq_len = x0_ref.shape[1]
    ridx = lax.broadcasted_iota(jnp.int32, (rows, LANES), 0)

    def conv(ref, w_ref, b_ref, r0):
        cur = ref[0, pl.ds(r0, rows), :].astype(F32)
        lo = jnp.maximum(r0 - HALO, 0)
        hi = jnp.minimum(r0 + rows, seq_len - HALO)
        before = ref[0, pl.ds(pl.multiple_of(lo, HALO), HALO), :].astype(F32)[HALO - 1:HALO]
        after = ref[0, pl.ds(pl.multiple_of(hi, HALO), HALO), :].astype(F32)[0:1]
        before = jnp.where(r0 > 0, before, 0.0)
        after = jnp.where(r0 + rows < seq_len, after, 0.0)
        prev = jnp.where(ridx == 0, before, pltpu.roll(cur, 1, 0))
        nxt = jnp.where(ridx == rows - 1, after, pltpu.roll(cur, rows - 1, 0))
        w = w_ref[...]
        return prev * w[0:1] + cur * w[1:2] + nxt * w[2:3] + b_ref[...]

    def body(c, carry):
        r0 = pl.multiple_of(c * rows, rows)
        x0 = conv(x0_ref, w0_ref, b0_ref, r0)
        x1 = conv(x1_ref, w1_ref, b1_ref, r0)
        v = conv(v_ref, wv_ref, bv_ref, r0)
        vg_ref[0, pl.ds(r0, rows), :] = (v * x1).astype(BF16)
        x0c_ref[0, pl.ds(r0, rows), :] = x0.astype(BF16)
        return carry

    lax.fori_loop(0, seq_len // rows, body, 0)


def _hy_gate(u, conv_w, conv_b):
    B, L, _ = u.shape
    nb = HY_WIDTH // LANES
    col = lambda g: pl.BlockSpec((1, L, LANES), lambda b, j: (b, 0, g * nb + j))
    wsp = lambda g: pl.BlockSpec((3, LANES), lambda b, j: (0, g * nb + j))
    bsp = lambda g: pl.BlockSpec((1, LANES), lambda b, j: (0, g * nb + j))
    osp = pl.BlockSpec((1, L, LANES), lambda b, j: (b, 0, j))
    st = jax.ShapeDtypeStruct((B, L, HY_WIDTH), BF16)
    return pl.pallas_call(
        functools.partial(_hy_gate_kernel, rows=256),
        grid=(B, nb),
        in_specs=[col(0), col(1), col(2), wsp(0), wsp(1), wsp(2), bsp(0), bsp(1), bsp(2)],
        out_specs=[osp, osp],
        out_shape=[st, st],
        compiler_params=_cparams(("parallel", "parallel")),
        name="hy_gate",
    )(u, u, u, conv_w, conv_w, conv_w, conv_b, conv_b, conv_b)


def _dft_tables(seq_len):
    n = 2 * seq_len
    n1_full = n // DFT_N2
    n1_used = n1_full // 2
    nk1 = n1_full // 2 + 1
    k1 = np.arange(nk1)[:, None]
    n1 = np.arange(n1_used)[None, :]
    th = 2.0 * np.pi * ((k1 * n1) % n1_full) / n1_full
    ck = np.full((nk1, 1), 2.0)
    ck[0, 0] = 1.0
    ck[-1, 0] = 1.0
    eye = np.eye(DFT_ROWS)
    far = np.kron(np.cos(th), eye)
    fai = np.kron(-np.sin(th), eye)
    finv = np.concatenate([np.kron((ck * np.cos(th) / n).T, eye),
                           np.kron((-ck * np.sin(th) / n).T, eye)], axis=1)
    k1 = np.arange(nk1)[:, None, None]
    k2 = np.arange(DFT_N2)[None, :, None]
    n2 = np.arange(DFT_N2)[None, None, :]
    ph = 2.0 * np.pi * ((n1_full * n2 * k2 + n2 * k1) % n) / n
    gr, gi = np.cos(ph), -np.sin(ph)
    g = np.concatenate([np.concatenate([gr, -gi], 2), np.concatenate([gi, gr], 2)], 1)
    grt, git = np.swapaxes(gr, 1, 2), -np.swapaxes(gi, 1, 2)
    ginv = np.concatenate([np.concatenate([grt, -git], 2), np.concatenate([git, grt], 2)], 1)
    f = lambda a: jnp.asarray(a.astype(np.float32))
    return f(far), f(fai), f(finv), f(g), f(ginv), nk1


def _rows2d(a):
    return a.reshape(a.shape[0] * a.shape[1], a.shape[2])


def _dft_a_kernel(x_ref, fr_ref, fi_ref, re_ref, im_ref):
    x = _rows2d(x_ref[0])
    for f_ref, o_ref in ((fr_ref, re_ref), (fi_ref, im_ref)):
        a = jnp.dot(f_ref[...], x, preferred_element_type=F32).astype(BF16)
        o_ref[0] = a.reshape(o_ref.shape[1:])


def _dft_a(xv, far, fai, nk1):
    B, n1u, n2, C = xv.shape
    st = jax.ShapeDtypeStruct((B, nk1, n2, C), BF16)
    osp = pl.BlockSpec((1, nk1, DFT_ROWS, C), lambda b, j: (b, 0, j, 0))
    return pl.pallas_call(
        _dft_a_kernel,
        grid=(B, n2 // DFT_ROWS),
        in_specs=[pl.BlockSpec((1, n1u, DFT_ROWS, C), lambda b, j: (b, 0, j, 0)),
                  _const_spec(far.shape), _const_spec(fai.shape)],
        out_specs=[osp, osp],
        out_shape=[st, st],
        compiler_params=_cparams(("parallel", "parallel")),
        name="dft_a",
    )(xv, far, fai)


def _dft_c_kernel(are_ref, aim_ref, hre_ref, him_ref, g_ref, gi_ref, asum_ref, bre_ref, bim_ref):
    half = DFT_N2
    g = g_ref[0].astype(BF16)
    gi = gi_ref[0].astype(BF16)

    def fwd(ar, ai):
        return jnp.dot(g, jnp.concatenate([ar, ai], axis=0), preferred_element_type=F32)

    asum = asum_ref[...]
    inv = 1.0 / (asum[:, :HY_WIDTH] + asum[:, HY_WIDTH:] + 1e-6)
    xf = fwd(hre_ref[0, 0], him_ref[0, 0])
    xb = fwd(hre_ref[1, 0], him_ref[1, 0])
    kr = (xf[:half] + xb[:half]) * inv
    ki = (xf[half:] - xb[half:]) * inv
    for b in range(are_ref.shape[0]):
        x = fwd(are_ref[b, 0], aim_ref[b, 0])
        xr, xi = x[:half], x[half:]
        y = jnp.concatenate([xr * kr - xi * ki, xr * ki + xi * kr], axis=0).astype(BF16)
        bk = jnp.dot(gi, y, preferred_element_type=F32)
        bre_ref[b, 0] = bk[:half].astype(BF16)
        bim_ref[b, 0] = bk[half:].astype(BF16)


def _dft_c(are, aim, hre, him, g, ginv, asum):
    B, nk1, n2, C = are.shape
    dspec = lambda nb: pl.BlockSpec((nb, 1, n2, C), lambda k: (0, k, 0, 0))
    gspec = pl.BlockSpec((1, 2 * n2, 2 * n2), lambda k: (k, 0, 0))
    st = jax.ShapeDtypeStruct(are.shape, BF16)
    return pl.pallas_call(
        _dft_c_kernel,
        grid=(nk1,),
        in_specs=[dspec(B), dspec(B), dspec(2), dspec(2), gspec, gspec, _const_spec(asum.shape)],
        out_specs=[dspec(B), dspec(B)],
        out_shape=[st, st],
        compiler_params=_cparams(("parallel",)),
        name="dft_c",
    )(are, aim, hre, him, g, ginv, asum)


def _idft_a_kernel(bre_ref, bim_ref, f_ref, vg_ref, x0_ref, bias_ref, o_ref):
    b = jnp.concatenate([_rows2d(bre_ref[0]), _rows2d(bim_ref[0])], axis=0)
    y = jnp.dot(f_ref[...], b, preferred_element_type=F32)
    y = y + _rows2d(vg_ref[0]).astype(F32) * bias_ref[...]
    o_ref[0] = (y * _rows2d(x0_ref[0]).astype(F32)).astype(BF16).reshape(o_ref.shape[1:])


def _idft_a(bre, bim, finv, vgv, x0v, bias):
    B, n1u, n2, C = vgv.shape
    nk1 = bre.shape[1]
    bsp = pl.BlockSpec((1, nk1, DFT_ROWS, C), lambda b, j: (b, 0, j, 0))
    dsp = pl.BlockSpec((1, n1u, DFT_ROWS, C), lambda b, j: (b, 0, j, 0))
    return pl.pallas_call(
        _idft_a_kernel,
        grid=(B, n2 // DFT_ROWS),
        in_specs=[bsp, bsp, _const_spec(finv.shape), dsp, dsp, _const_spec(bias.shape)],
        out_specs=dsp,
        out_shape=jax.ShapeDtypeStruct((B, n1u, n2, C), BF16),
        compiler_params=_cparams(("parallel", "parallel")),
        name="idft_a",
    )(bre, bim, finv, vgv, x0v, bias)


def _retention_kernel(q_ref, k_ref, v_ref, g_ref, lgf_ref, lgb_ref, s0f_ref, s0b_ref, gn_ref, o_ref,
                      inc_ref, st_ref, dm_ref, wt_ref):
    C = RET_CHUNK
    seq_len = q_ref.shape[1]
    nc = seq_len // C
    lgf = lgf_ref[0]
    lgb = lgb_ref[0]
    lane = lax.broadcasted_iota(jnp.int32, (1, LANES), 1)
    head_of_lane = lane // RET_HEAD_DIM
    bd = _block_diag_mask()

    ri = lax.broadcasted_iota(jnp.int32, (C, LANES), 0).astype(F32)
    wt_ref[0] = jnp.exp(lgf * (ri + 1.0)).astype(BF16)
    wt_ref[1] = jnp.exp(lgb * (C - ri)).astype(BF16)
    wt_ref[2] = jnp.exp(lgf * (C - 1.0 - ri)).astype(BF16)
    wt_ref[3] = jnp.exp(lgb * ri).astype(BF16)
    di = lax.broadcasted_iota(jnp.int32, (C, C), 0)
    dj = lax.broadcasted_iota(jnp.int32, (C, C), 1)
    dd = (di - dj).astype(F32)
    for h in range(HEADS_PER_BLOCK):
        lf = jnp.sum(jnp.where(lane == h * RET_HEAD_DIM, lgf, 0.0), axis=1, keepdims=True)
        lb = jnp.sum(jnp.where(lane == h * RET_HEAD_DIM, lgb, 0.0), axis=1, keepdims=True)
        dm_ref[h] = jnp.where(dd > 0, jnp.exp(lf * dd), jnp.where(dd < 0, jnp.exp(-lb * dd), 2.0)).astype(BF16)

    def chunk(c):
        return pl.ds(pl.multiple_of(c * C, C), C)

    def inc_body(c, carry):
        rows = chunk(c)
        k = k_ref[0, rows, :]
        kw = jnp.concatenate([k * wt_ref[2], k * wt_ref[3]], axis=1)
        t = _kt_v(kw, v_ref[0, rows, :])
        inc_ref[c, :LANES] = jnp.where(bd, t[:LANES], 0.0)
        inc_ref[c, LANES:] = jnp.where(bd, t[LANES:], 0.0)
        return carry

    lax.fori_loop(0, nc, inc_body, 0, unroll=2)

    gcf = jnp.exp(lgf * float(C))
    gcb = jnp.exp(lgb * float(C))

    def fscan(c, s):
        st_ref[c, :LANES] = s.astype(BF16)
        return s * gcf + inc_ref[c, :LANES]

    lax.fori_loop(0, nc, fscan, s0f_ref[0, 0])

    def bscan(i, s):
        c = nc - 1 - i
        st_ref[c, LANES:] = s.astype(BF16)
        return s * gcb + inc_ref[c, LANES:]

    lax.fori_loop(0, nc, bscan, s0b_ref[0, 0])

    gn = gn_ref[...]
    first = head_of_lane == 0
    inv_n = 1.0 / RET_HEAD_DIM

    def head_mean(a):
        tot = jnp.sum(a, axis=1, keepdims=True)
        s0 = jnp.sum(jnp.where(first, a, 0.0), axis=1, keepdims=True)
        return jnp.where(first, s0, tot - s0) * inv_n

    def out_body(c, carry):
        rows = chunk(c)
        q = q_ref[0, rows, :]
        k = k_ref[0, rows, :]
        v = v_ref[0, rows, :]
        qw = jnp.concatenate([q * wt_ref[0], q * wt_ref[1]], axis=1)
        o = jnp.dot(qw, st_ref[c], preferred_element_type=F32)
        oh = []
        for h in range(HEADS_PER_BLOCK):
            qm = jnp.where(head_of_lane == h, q, jnp.zeros_like(q))
            sc = lax.dot_general(qm, k, (((1,), (1,)), ((), ())), preferred_element_type=F32)
            oh.append(jnp.dot(sc.astype(BF16) * dm_ref[h], v, preferred_element_type=F32))
        o = o + jnp.where(first, oh[0], oh[1])
        d = o - head_mean(o)
        y = d * lax.rsqrt(head_mean(d * d) + NORM_EPS) * gn
        o_ref[0, rows, :] = (y * _silu(g_ref[0, rows, :].astype(F32))).astype(BF16)
        return carry

    lax.fori_loop(0, nc, out_body, 0, unroll=2)


def _retention(u, lgf, lgb, s0f, s0b, gn):
    B, L, _ = u.shape
    assert HEADS_PER_BLOCK == 2
    base = HY_PROJ // LANES
    nc = L // RET_CHUNK
    col = lambda g: pl.BlockSpec((1, L, LANES), lambda b, p: (b, 0, base + g * RET_BLOCKS + p))
    lsp = pl.BlockSpec((1, 1, LANES), lambda b, p: (p, 0, 0))
    ssp = pl.BlockSpec((1, 1, LANES, LANES), lambda b, p: (b, p, 0, 0))
    return pl.pallas_call(
        _retention_kernel,
        grid=(B, RET_BLOCKS),
        in_specs=[col(0), col(1), col(2), col(3), lsp, lsp, ssp, ssp,
                  pl.BlockSpec((1, LANES), lambda b, p: (0, p))],
        out_specs=pl.BlockSpec((1, L, LANES), lambda b, p: (b, 0, p)),
        out_shape=jax.ShapeDtypeStruct((B, L, RET_WIDTH), BF16),
        scratch_shapes=[pltpu.VMEM((nc, 2 * LANES, LANES), F32),
                        pltpu.VMEM((nc, 2 * LANES, LANES), BF16),
                        pltpu.VMEM((HEADS_PER_BLOCK, RET_CHUNK, RET_CHUNK), BF16),
                        pltpu.VMEM((4, RET_CHUNK, LANES), BF16)],
        compiler_params=_cparams(("parallel", "parallel")),
        name="retention",
    )(u, u, u, u, lgf, lgb, s0f, s0b, gn)


def _rope_tables(seq_len):
    rows = seq_len // GRID_W
    r, col = jnp.meshgrid(jnp.arange(rows, dtype=F32), jnp.arange(GRID_W, dtype=F32), indexing="ij")
    inv = ROPE_BASE ** (-jnp.arange(ROPE_PAIRS_AXIS, dtype=F32) / ROPE_PAIRS_AXIS)
    ang = jnp.concatenate([r.reshape(-1, 1) * inv, col.reshape(-1, 1) * inv], axis=-1)
    ang = jnp.tile(jnp.repeat(ang, 2, axis=1), (1, HEADS_PER_BLOCK))
    sign = jnp.where(jnp.arange(LANES) % 2 == 0, -1.0, 1.0).astype(F32)
    return jnp.cos(ang), jnp.sin(ang) * sign


def _out_mlp_kernel(x_ref, yh_ref, yr_ref, wo1_ref, wo2_ref, w1_ref, w2_ref, g2_ref, gf_ref, mod_ref, o_ref,
                    *, ff_chunk):
    mod = mod_ref[0]
    mix = (jnp.dot(yh_ref[0], wo1_ref[...], preferred_element_type=F32)
           + jnp.dot(yr_ref[0], wo2_ref[...], preferred_element_type=F32))
    x1 = x_ref[0] + mod[0:1] * mix
    h = (_rms(x1) * g2_ref[...] * (1.0 + mod[2:3]) + mod[1:2]).astype(BF16)
    acc = jnp.zeros_like(x1)
    for j in range(w1_ref.shape[1] // ff_chunk):
        cols = slice(j * ff_chunk, (j + 1) * ff_chunk)
        a = jnp.maximum(jnp.dot(h, w1_ref[:, cols], preferred_element_type=F32), 0.0)
        acc = acc + jnp.dot((a * a).astype(BF16), w2_ref[cols, :], preferred_element_type=F32)
    x2 = x1 + mod[3:4] * acc
    o_ref[0] = _rms(x2) * gf_ref[...]


def _out_mlp(x, yh, yr, wo1, wo2, w1, w2, g2, gf, mod, tm):
    B, L, D = x.shape
    dff = w1.shape[1]
    row = lambda n: pl.BlockSpec((1, tm, n), lambda b, i: (b, i, 0))
    wsp = lambda a: pl.BlockSpec(a.shape, lambda b, i: (0, 0), pipeline_mode=pl.Buffered(1))
    return pl.pallas_call(
        functools.partial(_out_mlp_kernel, ff_chunk=1024),
        grid=(B, L // tm),
        in_specs=[row(D), row(yh.shape[2]), row(yr.shape[2]), wsp(wo1), wsp(wo2), wsp(w1), wsp(w2),
                  _const_spec((1, D)), _const_spec((1, D)),
                  pl.BlockSpec((1, 4, D), lambda b, i: (b, 0, 0))],
        out_specs=row(D),
        out_shape=jax.ShapeDtypeStruct((B, L, D), F32),
        compiler_params=_cparams(("parallel", "parallel")),
        name="out_mlp",
    )(x, yh, yr, wo1, wo2, w1, w2, g2, gf, mod)


def kernel(x, c, ctx, c_ctx, w_ada, b_ada, norm1_g, w_in, hy_conv_w, hy_conv_b, hy_f_w1, hy_f_b1, hy_f_freq1,
           hy_f_w2, hy_f_b2, hy_f_freq2, hy_f_w3, hy_bias, ret_decay_logit, ret_gn_g, w_out, norm2_g, w_mlp1,
           w_mlp2, norm_f_g):
    B, L, D = x.shape
    depth = w_ada.shape[0]
    assert depth == 1, "single-layer block"
    l = 0
    rows = 16
    cvec = jnp.zeros((rows, D), F32).at[:B].set(c).at[B].set(c_ctx)
    mod = _ada(cvec, w_ada[l], b_ada[l][None])
    mx = mod[:B].reshape(B, 6, D)
    mc = mod[B].reshape(6, D)

    lg = jax.nn.log_sigmoid(ret_decay_logit[l].astype(F32))
    lg_lanes = jnp.repeat(lg, RET_HEAD_DIM, axis=1).reshape(2, RET_BLOCKS, 1, LANES)
    lgf, lgb = lg_lanes[0], lg_lanes[1]

    w_in_b = w_in[l].astype(BF16)
    g1 = norm1_g[l][None]
    u = _inproj(x, g1, mx[:, 0:1], mx[:, 1:2], w_in_b, tm=512, rope=_rope_tables(L))
    ones = jnp.ones((B, 1, 1), F32)
    uc = _inproj(ctx, g1, ones * mc[0], ones * mc[1], w_in_b, tm=ctx.shape[1])
    s0f, s0b = _ctx_states(uc, lgf, lgb)

    far, fai, finv, gmat, ginv, nk1 = _dft_tables(L)
    far, fai, finv = far.astype(BF16), fai.astype(BF16), finv.astype(BF16)
    taps, asum = _filters(L, hy_f_w1[l], hy_f_b1[l], hy_f_freq1[l], hy_f_w2[l], hy_f_b2[l], hy_f_freq2[l],
                          hy_f_w3[l])
    vg, x0c = _hy_gate(u, hy_conv_w[l], hy_conv_b[l][None])
    tview = lambda a: a.reshape(a.shape[0], L // DFT_N2, DFT_N2, HY_WIDTH)
    are, aim = _dft_a(tview(vg), far, fai, nk1)
    hre, him = _dft_a(tview(taps), far, fai, nk1)
    bre, bim = _dft_c(are, aim, hre, him, gmat, ginv, asum)
    y_hy = _idft_a(bre, bim, finv, tview(vg), tview(x0c), hy_bias[l][None].astype(F32)).reshape(B, L, HY_WIDTH)

    y_ret = _retention(u, lgf, lgb, s0f, s0b, ret_gn_g[l][None].astype(F32))

    w_out_b = w_out[l].astype(BF16)
    mod2 = mx[:, 2:6]
    return _out_mlp(x, y_hy, y_ret, w_out_b[:HY_WIDTH], w_out_b[HY_WIDTH:], w_mlp1[l].astype(BF16),
                    w_mlp2[l].astype(BF16), norm2_g[l][None], norm_f_g[None], mod2, tm=512)
```

```python
import functools
import math

import numpy as np
import jax
import jax.numpy as jnp
from jax import lax
from jax.experimental import pallas as pl
from jax.experimental.pallas import tpu as pltpu

F32 = jnp.float32
BF16 = jnp.bfloat16
HIGHEST = lax.Precision.HIGHEST

GRID_W = 64
HY_WIDTH = 512
RET_WIDTH = 512
RET_HEADS = 8
RET_HEAD_DIM = RET_WIDTH // RET_HEADS
HY_PROJ = 3 * HY_WIDTH
HY_BANDS = 16
HY_FILT_HID = 64
HY_DECAY_TARGET = 1e-2
HY_FAST_PCT = 0.3
HY_SLOW_PCT = 1.5
ROPE_BASE = 10000.0
ROPE_PAIRS_AXIS = RET_HEAD_DIM // 4
NORM_EPS = 1e-6

LANES = 128
HEADS_PER_BLOCK = LANES // RET_HEAD_DIM
RET_BLOCKS = RET_WIDTH // LANES
RET_CHUNK = 256
DFT_N2 = 128
DFT_ROWS = 16
DFT_KHALF = 72
DFTC_LANES = 1024
VMEM_LIMIT = 56 * 1024 * 1024


def _cparams(sem):
    return pltpu.CompilerParams(dimension_semantics=sem, vmem_limit_bytes=VMEM_LIMIT)


def _const_spec(shape):
    nd = len(shape)
    return pl.BlockSpec(shape, lambda *_: (0,) * nd)


def _silu(x):
    return x * jax.nn.sigmoid(x)


def _rms(x):
    return x * lax.rsqrt(jnp.mean(x * x, axis=-1, keepdims=True) + NORM_EPS)


def _ada_kernel(c_ref, w_ref, b_ref, o_ref):
    s = _silu(c_ref[...])
    o_ref[...] = jnp.dot(s, w_ref[...], preferred_element_type=F32, precision=HIGHEST) + b_ref[...]


def _ada(cvec, w, b):
    R, D = cvec.shape
    N = w.shape[1]
    return pl.pallas_call(
        _ada_kernel,
        grid=(N // D,),
        in_specs=[_const_spec((R, D)), pl.BlockSpec((D, D), lambda j: (0, j)),
                  pl.BlockSpec((1, D), lambda j: (0, j))],
        out_specs=pl.BlockSpec((R, D), lambda j: (0, j)),
        out_shape=jax.ShapeDtypeStruct((R, N), F32),
        compiler_params=_cparams(("arbitrary",)),
        name="ada",
    )(cvec, w, b)


INPROJ_NCHUNK = 512


def _inproj_kernel(x_ref, g_ref, sh_ref, sc_ref, w_ref, *rest, rope):
    if rope:
        cos_ref, sin_ref, o_ref = rest
        even = (lax.broadcasted_iota(jnp.int32, (1, LANES), 1) % 2) == 0
    else:
        (o_ref,) = rest
    h = _rms(x_ref[0]) * g_ref[...]
    hb = (h * (1.0 + sc_ref[0]) + sh_ref[0]).astype(BF16)
    for c0 in range(0, w_ref.shape[1], INPROJ_NCHUNK):
        cols = slice(c0, c0 + INPROJ_NCHUNK)
        r = jnp.dot(hb, w_ref[:, cols], preferred_element_type=F32)
        if rope and HY_PROJ <= c0 < HY_PROJ + 2 * RET_WIDTH:
            is_key = c0 >= HY_PROJ + RET_WIDTH
            cs, sn = cos_ref[...], sin_ref[...]
            parts = []
            for b0 in range(0, INPROJ_NCHUNK, LANES):
                t = r[:, b0:b0 + LANES]
                sw = jnp.where(even, pltpu.roll(t, LANES - 1, 1), pltpu.roll(t, 1, 1))
                t = t * cs + sw * sn
                parts.append(t * (RET_HEAD_DIM ** -0.5) if is_key else t)
            r = jnp.concatenate(parts, axis=1)
        o_ref[0, :, cols] = r.astype(BF16)


def _inproj(x, g, shift, scale, w, tm, rope=None):
    B, L, D = x.shape
    N = w.shape[1]
    in_specs = [pl.BlockSpec((1, tm, D), lambda b, i: (b, i, 0)),
                _const_spec((1, D)),
                pl.BlockSpec((1, 1, D), lambda b, i: (b, 0, 0)),
                pl.BlockSpec((1, 1, D), lambda b, i: (b, 0, 0)),
                pl.BlockSpec((D, N), lambda b, i: (0, 0), pipeline_mode=pl.Buffered(1))]
    args = [x, g, shift, scale, w]
    if rope is not None:
        in_specs += [pl.BlockSpec((tm, LANES), lambda b, i: (i, 0))] * 2
        args += list(rope)
    return pl.pallas_call(
        functools.partial(_inproj_kernel, rope=rope is not None),
        grid=(B, L // tm),
        in_specs=in_specs,
        out_specs=pl.BlockSpec((1, tm, N), lambda b, i: (b, i, 0)),
        out_shape=jax.ShapeDtypeStruct((B, L, N), BF16),
        compiler_params=_cparams(("parallel", "parallel")),
        name="inproj",
    )(*args)


def _block_diag_mask():
    r = lax.broadcasted_iota(jnp.int32, (LANES, LANES), 0) // RET_HEAD_DIM
    c = lax.broadcasted_iota(jnp.int32, (LANES, LANES), 1) // RET_HEAD_DIM
    return r == c


def _kt_v(k, v):
    return lax.dot_general(k, v, (((0,), (0,)), ((), ())), preferred_element_type=F32)


def _ctx_state_kernel(k_ref, v_ref, lgf_ref, lgb_ref, sf_ref, sb_ref):
    k = k_ref[0].astype(F32) * (RET_HEAD_DIM ** -0.5)
    v = v_ref[0]
    lc = k.shape[0]
    m = lax.broadcasted_iota(jnp.int32, (lc, LANES), 0).astype(F32)
    wf = jnp.exp(lgf_ref[0] * (lc - 1.0 - m))
    wb = jnp.exp(lgb_ref[0] * m)
    bd = _block_diag_mask()
    sf_ref[0, 0] = jnp.where(bd, _kt_v((k * wf).astype(BF16), v), 0.0)
    sb_ref[0, 0] = jnp.where(bd, _kt_v((k * wb).astype(BF16), v), 0.0)


def _ctx_states(uc, lgf, lgb):
    B, lc, _ = uc.shape
    kb = (HY_PROJ + RET_WIDTH) // LANES
    vb = (HY_PROJ + 2 * RET_WIDTH) // LANES
    st = jax.ShapeDtypeStruct((B, RET_BLOCKS, LANES, LANES), F32)
    sspec = pl.BlockSpec((1, 1, LANES, LANES), lambda b, p: (b, p, 0, 0))
    return pl.pallas_call(
        _ctx_state_kernel,
        grid=(B, RET_BLOCKS),
        in_specs=[pl.BlockSpec((1, lc, LANES), lambda b, p: (b, 0, kb + p)),
                  pl.BlockSpec((1, lc, LANES), lambda b, p: (b, 0, vb + p)),
                  pl.BlockSpec((1, 1, LANES), lambda b, p: (p, 0, 0)),
                  pl.BlockSpec((1, 1, LANES), lambda b, p: (p, 0, 0))],
        out_specs=[sspec, sspec],
        out_shape=[st, st],
        compiler_params=_cparams(("parallel", "parallel")),
        name="ctx_state",
    )(uc, uc, lgf, lgb)


def _filt_kernel(bands_ref, w1t_ref, w1c_ref, w1s_ref, b1_ref, fr1_ref, w2_ref, b2_ref, fr2_ref,
                 w3_ref, dl_ref, h_ref, asum_ref, *, seq_len, tl):
    i = pl.program_id(0)
    n = (lax.broadcasted_iota(jnp.int32, (tl, 1), 0) + i * tl).astype(F32)
    t = n / (seq_len - 1.0)
    w = (2.0 * math.pi / seq_len) * n
    arg = bands_ref[...] * w
    dot = functools.partial(jnp.dot, preferred_element_type=F32, precision=HIGHEST)
    pre = t * w1t_ref[...] + dot(jnp.cos(arg), w1c_ref[...]) - dot(jnp.sin(arg), w1s_ref[...]) + b1_ref[...]
    h = jnp.sin(fr1_ref[...] * pre)
    h = jnp.sin(fr2_ref[...] * (dot(h, w2_ref[...]) + b2_ref[...]))
    h = dot(h, w3_ref[...]) * jnp.exp(-t * dl_ref[...])

    @pl.when(i == 0)
    def _():
        asum_ref[...] = jnp.zeros_like(asum_ref)

    asum_ref[...] += jnp.sum(jnp.abs(h), axis=0, keepdims=True)
    h_ref[0] = h[:, :HY_WIDTH].astype(BF16)
    h_ref[1] = jnp.where(n == 0.0, 0.0, h[:, HY_WIDTH:]).astype(BF16)


def _pad2(a, rows, cols):
    return jnp.pad(a.astype(F32), ((0, rows - a.shape[0]), (0, cols - a.shape[1])))


def _filters(seq_len, w1, b1, fr1, w2, b2, fr2, w3):
    tl = 1024
    P = LANES
    bands = np.zeros((1, P), np.float32)
    bands[0, :HY_BANDS] = np.linspace(1e-4, HY_BANDS - 1, HY_BANDS, dtype=np.float32)
    deltas = np.abs(np.linspace(math.log(HY_DECAY_TARGET) / HY_SLOW_PCT,
                                math.log(HY_DECAY_TARGET) / HY_FAST_PCT, HY_WIDTH, dtype=np.float32))
    dl = np.tile(deltas, 2)[None, :]
    args = (jnp.asarray(bands),
            _pad2(w1[0:1], 1, P), _pad2(w1[1:1 + HY_BANDS], P, P), _pad2(w1[1 + HY_BANDS:], P, P),
            _pad2(b1[None], 1, P), _pad2(fr1[None], 1, P),
            _pad2(w2, P, P), _pad2(b2[None], 1, P), _pad2(fr2[None], 1, P),
            _pad2(w3, P, 2 * HY_WIDTH), jnp.asarray(dl))
    return pl.pallas_call(
        functools.partial(_filt_kernel, seq_len=seq_len, tl=tl),
        grid=(seq_len // tl,),
        in_specs=[_const_spec(a.shape) for a in args],
        out_specs=[pl.BlockSpec((2, tl, HY_WIDTH), lambda i: (0, i, 0)),
                   _const_spec((1, 2 * HY_WIDTH))],
        out_shape=[jax.ShapeDtypeStruct((2, seq_len, HY_WIDTH), BF16),
                   jax.ShapeDtypeStruct((1, 2 * HY_WIDTH), F32)],
        compiler_params=_cparams(("arbitrary",)),
        name="filt",
    )(*args)


HALO = 16


def _hy_gate_kernel(x0_ref, x1_ref, v_ref, w0_ref, w1_ref, wv_ref, b0_ref, b1_ref, bv_ref,
                    vg_ref, x0c_ref, *, rows):
    seq_len = x0_ref.shape[1]
    ridx = lax.broadcasted_iota(jnp.int32, (rows, LANES), 0)

    def conv(ref, w_ref, b_ref, r0):
        cur = ref[0, pl.ds(r0, rows), :].astype(F32)
        lo = jnp.maximum(r0 - HALO, 0)
        hi = jnp.minimum(r0 + rows, seq_len - HALO)
        before = ref[0, pl.ds(pl.multiple_of(lo, HALO), HALO), :].astype(F32)[HALO - 1:HALO]
        after = ref[0, pl.ds(pl.multiple_of(hi, HALO), HALO), :].astype(F32)[0:1]
        before = jnp.where(r0 > 0, before, 0.0)
        after = jnp.where(r0 + rows < seq_len, after, 0.0)
        prev = jnp.where(ridx == 0, before, pltpu.roll(cur, 1, 0))
        nxt = jnp.where(ridx == rows - 1, after, pltpu.roll(cur, rows - 1, 0))
        w = w_ref[...]
        return prev * w[0:1] + cur * w[1:2] + nxt * w[2:3] + b_ref[...]

    def body(c, carry):
        r0 = pl.multiple_of(c * rows, rows)
        x0 = conv(x0_ref, w0_ref, b0_ref, r0)
        x1 = conv(x1_ref, w1_ref, b1_ref, r0)
        v = conv(v_ref, wv_ref, bv_ref, r0)
        vg_ref[0, pl.ds(r0, rows), :] = (v * x1).astype(BF16)
        x0c_ref[0, pl.ds(r0, rows), :] = x0.astype(BF16)
        return carry

    lax.fori_loop(0, seq_len // rows, body, 0)


def _hy_gate(u, conv_w, conv_b):
    B, L, _ = u.shape
    nb = HY_WIDTH // LANES
    col = lambda g: pl.BlockSpec((1, L, LANES), lambda b, j: (b, 0, g * nb + j))
    wsp = lambda g: pl.BlockSpec((3, LANES), lambda b, j: (0, g * nb + j))
    bsp = lambda g: pl.BlockSpec((1, LANES), lambda b, j: (0, g * nb + j))
    osp = pl.BlockSpec((1, L, LANES), lambda b, j: (b, 0, j))
    st = jax.ShapeDtypeStruct((B, L, HY_WIDTH), BF16)
    return pl.pallas_call(
        functools.partial(_hy_gate_kernel, rows=256),
        grid=(B, nb),
        in_specs=[col(0), col(1), col(2), wsp(0), wsp(1), wsp(2), bsp(0), bsp(1), bsp(2)],
        out_specs=[osp, osp],
        out_shape=[st, st],
        compiler_params=_cparams(("parallel", "parallel")),
        name="hy_gate",
    )(u, u, u, conv_w, conv_w, conv_w, conv_b, conv_b, conv_b)


def _dft_tables(seq_len):
    n = 2 * seq_len
    n1_full = n // DFT_N2
    n1_used = n1_full // 2
    nk1 = n1_full // 2 + 1
    k1 = np.arange(nk1)[:, None]
    n1 = np.arange(n1_used)[None, :]
    th = 2.0 * np.pi * ((k1 * n1) % n1_full) / n1_full
    ck = np.full((nk1, 1), 2.0)
    ck[0, 0] = 1.0
    ck[-1, 0] = 1.0
    assert nk1 <= DFT_KHALF
    f1 = np.zeros((2 * DFT_KHALF, n1_used))
    f1[:nk1] = np.cos(th)
    f1[DFT_KHALF:DFT_KHALF + nk1] = -np.sin(th)
    finv = np.zeros((n1_used, 2 * DFT_KHALF))
    finv[:, 0:2 * nk1:2] = (ck * np.cos(th) / n).T
    finv[:, 1:2 * nk1:2] = (-ck * np.sin(th) / n).T
    k1 = np.arange(nk1)[:, None, None]
    k2 = np.arange(DFT_N2)[None, :, None]
    n2 = np.arange(DFT_N2)[None, None, :]
    ph = 2.0 * np.pi * ((n1_full * n2 * k2 + n2 * k1) % n) / n
    gr, gi = np.cos(ph), -np.sin(ph)
    g = np.concatenate([np.concatenate([gr, -gi], 2), np.concatenate([gi, gr], 2)], 1)
    grt, git = np.swapaxes(gr, 1, 2), -np.swapaxes(gi, 1, 2)
    ginv = np.concatenate([np.concatenate([grt, -git], 2), np.concatenate([git, grt], 2)], 1)
    f = lambda a: jnp.asarray(a.astype(np.float32))
    return f(f1), f(finv), f(g), f(ginv), nk1


def _rows2d(a):
    return a.reshape(a.shape[0] * a.shape[1], a.shape[2])


def _dft_a_kernel(x_ref, f_ref, o_ref):
    n1u = x_ref.shape[1]
    xs = pltpu.einshape("nrc->(rn)c", x_ref[0])
    f = f_ref[...]
    res = [jnp.dot(f, xs[r * n1u:(r + 1) * n1u], preferred_element_type=F32).astype(BF16)
           for r in range(DFT_ROWS)]
    a = pltpu.einshape("rmc->mrc", jnp.stack(res, axis=0))
    o_ref[:, 0] = a[:DFT_KHALF]
    o_ref[:, 1] = a[DFT_KHALF:]


def _dft_a(xv, f1):
    Bd, n1u, n2, C = xv.shape
    return pl.pallas_call(
        _dft_a_kernel,
        grid=(Bd, n2 // DFT_ROWS),
        in_specs=[pl.BlockSpec((1, n1u, DFT_ROWS, C), lambda b, j: (b, 0, j, 0)), _const_spec(f1.shape)],
        out_specs=pl.BlockSpec((DFT_KHALF, 2, DFT_ROWS, C), lambda b, j: (0, 0, j, b)),
        out_shape=jax.ShapeDtypeStruct((DFT_KHALF, 2, n2, Bd * C), BF16),
        compiler_params=_cparams(("parallel", "parallel")),
        name="dft_a",
    )(xv, f1)


def _dft_c_kernel(a_ref, h_ref, g_ref, gi_ref, asum_ref, o_ref):
    g = g_ref[0].astype(BF16)
    gi = gi_ref[0].astype(BF16)
    rows = g.shape[0]
    half = rows // 2
    C = HY_WIDTH
    xh = jnp.dot(g, h_ref[0].reshape(rows, h_ref.shape[3]), preferred_element_type=F32)
    asum = asum_ref[...]
    inv = 1.0 / (asum[:, :C] + asum[:, C:] + 1e-6)
    kr = (xh[:half, :C] + xh[:half, C:]) * inv
    ki = (xh[half:, :C] - xh[half:, C:]) * inv
    for l0 in range(0, a_ref.shape[3], DFTC_LANES):
        x = jnp.dot(g, a_ref[0, :, :, l0:l0 + DFTC_LANES].reshape(rows, DFTC_LANES), preferred_element_type=F32)
        ys = []
        for c0 in range(0, DFTC_LANES, C):
            xr, xi = x[:half, c0:c0 + C], x[half:, c0:c0 + C]
            ys.append(jnp.concatenate([xr * kr - xi * ki, xr * ki + xi * kr], axis=0).astype(BF16))
        bk = jnp.dot(gi, jnp.concatenate(ys, axis=1), preferred_element_type=F32).astype(BF16)
        o_ref[0, :, :, l0:l0 + DFTC_LANES] = bk.reshape(2, half, DFTC_LANES)


def _dft_c(a, h, g, ginv, asum, nk1):
    _, _, n2, bc = a.shape
    dspec = lambda w: pl.BlockSpec((1, 2, n2, w), lambda k: (k, 0, 0, 0))
    gspec = pl.BlockSpec((1, 2 * n2, 2 * n2), lambda k: (k, 0, 0))
    return pl.pallas_call(
        _dft_c_kernel,
        grid=(nk1,),
        in_specs=[dspec(bc), dspec(h.shape[3]), gspec, gspec, _const_spec(asum.shape)],
        out_specs=dspec(bc),
        out_shape=jax.ShapeDtypeStruct((nk1, 2, n2, bc), BF16),
        compiler_params=_cparams(("parallel",)),
        name="dft_c",
    )(a, h, g, ginv, asum)


def _idft_a_kernel(b_ref, f_ref, vg_ref, x0_ref, bias_ref, o_ref):
    nk1, _, R, C = b_ref.shape
    mp = f_ref.shape[1]
    bv = b_ref[...].reshape(2 * nk1, R, C)
    bv = jnp.concatenate([bv, jnp.zeros((mp - 2 * nk1, R, C), BF16)], axis=0)
    bs = pltpu.einshape("mrc->(rm)c", bv)
    f = f_ref[...]
    ys = [jnp.dot(f, bs[r * mp:(r + 1) * mp], preferred_element_type=F32).astype(BF16) for r in range(R)]
    y = pltpu.einshape("rnc->(nr)c", jnp.stack(ys, axis=0)).astype(F32)
    y = y + _rows2d(vg_ref[0]).astype(F32) * bias_ref[...]
    o_ref[0] = (y * _rows2d(x0_ref[0]).astype(F32)).astype(BF16).reshape(o_ref.shape[1:])


def _idft_a(bk, finv, vgv, x0v, bias):
    B, n1u, n2, C = vgv.shape
    nk1 = bk.shape[0]
    dsp = pl.BlockSpec((1, n1u, DFT_ROWS, C), lambda b, j: (b, 0, j, 0))
    return pl.pallas_call(
        _idft_a_kernel,
        grid=(B, n2 // DFT_ROWS),
        in_specs=[pl.BlockSpec((nk1, 2, DFT_ROWS, C), lambda b, j: (0, 0, j, b)),
                  _const_spec(finv.shape), dsp, dsp, _const_spec(bias.shape)],
        out_specs=dsp,
        out_shape=jax.ShapeDtypeStruct((B, n1u, n2, C), BF16),
        compiler_params=_cparams(("parallel", "parallel")),
        name="idft_a",
    )(bk, finv, vgv, x0v, bias)


def _retention_kernel(q_ref, k_ref, v_ref, g_ref, lgf_ref, lgb_ref, s0f_ref, s0b_ref, gn_ref, o_ref,
                      inc_ref, st_ref, dm_ref, wt_ref):
    C = RET_CHUNK
    seq_len = q_ref.shape[1]
    nc = seq_len // C
    lgf = lgf_ref[0]
    lgb = lgb_ref[0]
    lane = lax.broadcasted_iota(jnp.int32, (1, LANES), 1)
    head_of_lane = lane // RET_HEAD_DIM
    bd = _block_diag_mask()

    ri = lax.broadcasted_iota(jnp.int32, (C, LANES), 0).astype(F32)
    wt_ref[0] = jnp.exp(lgf * (ri + 1.0)).astype(BF16)
    wt_ref[1] = jnp.exp(lgb * (C - ri)).astype(BF16)
    wt_ref[2] = jnp.exp(lgf * (C - 1.0 - ri)).astype(BF16)
    wt_ref[3] = jnp.exp(lgb * ri).astype(BF16)
    di = lax.broadcasted_iota(jnp.int32, (C, C), 0)
    dj = lax.broadcasted_iota(jnp.int32, (C, C), 1)
    dd = (di - dj).astype(F32)
    for h in range(HEADS_PER_BLOCK):
        lf = jnp.sum(jnp.where(lane == h * RET_HEAD_DIM, lgf, 0.0), axis=1, keepdims=True)
        lb = jnp.sum(jnp.where(lane == h * RET_HEAD_DIM, lgb, 0.0), axis=1, keepdims=True)
        dm_ref[h] = jnp.where(dd > 0, jnp.exp(lf * dd), jnp.where(dd < 0, jnp.exp(-lb * dd), 2.0)).astype(BF16)

    def chunk(c):
        return pl.ds(pl.multiple_of(c * C, C), C)

    def inc_body(c, carry):
        rows = chunk(c)
        k = k_ref[0, rows, :]
        kw = jnp.concatenate([k * wt_ref[2], k * wt_ref[3]], axis=1)
        t = _kt_v(kw, v_ref[0, rows, :])
        inc_ref[c, :LANES] = jnp.where(bd, t[:LANES], 0.0)
        inc_ref[c, LANES:] = jnp.where(bd, t[LANES:], 0.0)
        return carry

    lax.fori_loop(0, nc, inc_body, 0, unroll=4)

    gcf = jnp.exp(lgf * float(C))
    gcb = jnp.exp(lgb * float(C))

    def fscan(c, s):
        st_ref[c, :LANES] = s.astype(BF16)
        return s * gcf + inc_ref[c, :LANES]

    lax.fori_loop(0, nc, fscan, s0f_ref[0, 0])

    def bscan(i, s):
        c = nc - 1 - i
        st_ref[c, LANES:] = s.astype(BF16)
        return s * gcb + inc_ref[c, LANES:]

    lax.fori_loop(0, nc, bscan, s0b_ref[0, 0])

    gn = gn_ref[...]
    first = head_of_lane == 0
    inv_n = 1.0 / RET_HEAD_DIM

    def head_mean(a):
        tot = jnp.sum(a, axis=1, keepdims=True)
        s0 = jnp.sum(jnp.where(first, a, 0.0), axis=1, keepdims=True)
        return jnp.where(first, s0, tot - s0) * inv_n

    def out_body(c, carry):
        rows = chunk(c)
        q = q_ref[0, rows, :]
        k = k_ref[0, rows, :]
        v = v_ref[0, rows, :]
        qw = jnp.concatenate([q * wt_ref[0], q * wt_ref[1]], axis=1)
        o = jnp.dot(qw, st_ref[c], preferred_element_type=F32)
        oh = []
        for h in range(HEADS_PER_BLOCK):
            qm = jnp.where(head_of_lane == h, q, jnp.zeros_like(q))
            sc = lax.dot_general(qm, k, (((1,), (1,)), ((), ())), preferred_element_type=F32)
            oh.append(jnp.dot(sc.astype(BF16) * dm_ref[h], v, preferred_element_type=F32))
        o = o + jnp.where(first, oh[0], oh[1])
        d = o - head_mean(o)
        y = d * lax.rsqrt(head_mean(d * d) + NORM_EPS) * gn
        o_ref[0, rows, :] = (y * _silu(g_ref[0, rows, :].astype(F32))).astype(BF16)
        return carry

    lax.fori_loop(0, nc, out_body, 0, unroll=2)


def _retention(u, lgf, lgb, s0f, s0b, gn):
    B, L, _ = u.shape
    assert HEADS_PER_BLOCK == 2
    base = HY_PROJ // LANES
    nc = L // RET_CHUNK
    col = lambda g: pl.BlockSpec((1, L, LANES), lambda b, p: (b, 0, base + g * RET_BLOCKS + p))
    lsp = pl.BlockSpec((1, 1, LANES), lambda b, p: (p, 0, 0))
    ssp = pl.BlockSpec((1, 1, LANES, LANES), lambda b, p: (b, p, 0, 0))
    return pl.pallas_call(
        _retention_kernel,
        grid=(B, RET_BLOCKS),
        in_specs=[col(0), col(1), col(2), col(3), lsp, lsp, ssp, ssp,
                  pl.BlockSpec((1, LANES), lambda b, p: (0, p))],
        out_specs=pl.BlockSpec((1, L, LANES), lambda b, p: (b, 0, p)),
        out_shape=jax.ShapeDtypeStruct((B, L, RET_WIDTH), BF16),
        scratch_shapes=[pltpu.VMEM((nc, 2 * LANES, LANES), F32),
                        pltpu.VMEM((nc, 2 * LANES, LANES), BF16),
                        pltpu.VMEM((HEADS_PER_BLOCK, RET_CHUNK, RET_CHUNK), BF16),
                        pltpu.VMEM((4, RET_CHUNK, LANES), BF16)],
        compiler_params=_cparams(("parallel", "parallel")),
        name="retention",
    )(u, u, u, u, lgf, lgb, s0f, s0b, gn)


def _rope_tables(seq_len):
    rows = seq_len // GRID_W
    r, col = jnp.meshgrid(jnp.arange(rows, dtype=F32), jnp.arange(GRID_W, dtype=F32), indexing="ij")
    inv = ROPE_BASE ** (-jnp.arange(ROPE_PAIRS_AXIS, dtype=F32) / ROPE_PAIRS_AXIS)
    ang = jnp.concatenate([r.reshape(-1, 1) * inv, col.reshape(-1, 1) * inv], axis=-1)
    ang = jnp.tile(jnp.repeat(ang, 2, axis=1), (1, HEADS_PER_BLOCK))
    sign = jnp.where(jnp.arange(LANES) % 2 == 0, -1.0, 1.0).astype(F32)
    return jnp.cos(ang), jnp.sin(ang) * sign


def _out_mlp_kernel(x_ref, yh_ref, yr_ref, wo1_ref, wo2_ref, w1_ref, w2_ref, g2_ref, gf_ref, mod_ref, o_ref,
                    *, ff_chunk):
    mod = mod_ref[0]
    mix = (jnp.dot(yh_ref[0], wo1_ref[...], preferred_element_type=F32)
           + jnp.dot(yr_ref[0], wo2_ref[...], preferred_element_type=F32))
    x1 = x_ref[0] + mod[0:1] * mix
    h = (_rms(x1) * g2_ref[...] * (1.0 + mod[2:3]) + mod[1:2]).astype(BF16)
    acc = jnp.zeros_like(x1)
    for j in range(w1_ref.shape[1] // ff_chunk):
        cols = slice(j * ff_chunk, (j + 1) * ff_chunk)
        a = jnp.maximum(jnp.dot(h, w1_ref[:, cols], preferred_element_type=F32), 0.0)
        acc = acc + jnp.dot((a * a).astype(BF16), w2_ref[cols, :], preferred_element_type=F32)
    x2 = x1 + mod[3:4] * acc
    o_ref[0] = _rms(x2) * gf_ref[...]


def _out_mlp(x, yh, yr, wo1, wo2, w1, w2, g2, gf, mod, tm):
    B, L, D = x.shape
    dff = w1.shape[1]
    row = lambda n: pl.BlockSpec((1, tm, n), lambda b, i: (b, i, 0))
    wsp = lambda a: pl.BlockSpec(a.shape, lambda b, i: (0, 0), pipeline_mode=pl.Buffered(1))
    return pl.pallas_call(
        functools.partial(_out_mlp_kernel, ff_chunk=1024),
        grid=(B, L // tm),
        in_specs=[row(D), row(yh.shape[2]), row(yr.shape[2]), wsp(wo1), wsp(wo2), wsp(w1), wsp(w2),
                  _const_spec((1, D)), _const_spec((1, D)),
                  pl.BlockSpec((1, 4, D), lambda b, i: (b, 0, 0))],
        out_specs=row(D),
        out_shape=jax.ShapeDtypeStruct((B, L, D), F32),
        compiler_params=_cparams(("parallel", "parallel")),
        name="out_mlp",
    )(x, yh, yr, wo1, wo2, w1, w2, g2, gf, mod)


def kernel(x, c, ctx, c_ctx, w_ada, b_ada, norm1_g, w_in, hy_conv_w, hy_conv_b, hy_f_w1, hy_f_b1, hy_f_freq1,
           hy_f_w2, hy_f_b2, hy_f_freq2, hy_f_w3, hy_bias, ret_decay_logit, ret_gn_g, w_out, norm2_g, w_mlp1,
           w_mlp2, norm_f_g):
    B, L, D = x.shape
    depth = w_ada.shape[0]
    assert depth == 1, "single-layer block"
    l = 0
    rows = 16
    cvec = jnp.zeros((rows, D), F32).at[:B].set(c).at[B].set(c_ctx)
    mod = _ada(cvec, w_ada[l], b_ada[l][None])
    mx = mod[:B].reshape(B, 6, D)
    mc = mod[B].reshape(6, D)

    lg = jax.nn.log_sigmoid(ret_decay_logit[l].astype(F32))
    lg_lanes = jnp.repeat(lg, RET_HEAD_DIM, axis=1).reshape(2, RET_BLOCKS, 1, LANES)
    lgf, lgb = lg_lanes[0], lg_lanes[1]

    w_in_b = w_in[l].astype(BF16)
    g1 = norm1_g[l][None]
    u = _inproj(x, g1, mx[:, 0:1], mx[:, 1:2], w_in_b, tm=512, rope=_rope_tables(L))
    ones = jnp.ones((B, 1, 1), F32)
    uc = _inproj(ctx, g1, ones * mc[0], ones * mc[1], w_in_b, tm=ctx.shape[1])
    s0f, s0b = _ctx_states(uc, lgf, lgb)

    f1, finv, gmat, ginv, nk1 = _dft_tables(L)
    f1, finv = f1.astype(BF16), finv.astype(BF16)
    taps, asum = _filters(L, hy_f_w1[l], hy_f_b1[l], hy_f_freq1[l], hy_f_w2[l], hy_f_b2[l], hy_f_freq2[l],
                          hy_f_w3[l])
    vg, x0c = _hy_gate(u, hy_conv_w[l], hy_conv_b[l][None])
    tview = lambda a: a.reshape(a.shape[0], L // DFT_N2, DFT_N2, HY_WIDTH)
    bk = _dft_c(_dft_a(tview(vg), f1), _dft_a(tview(taps), f1), gmat, ginv, asum, nk1)
    y_hy = _idft_a(bk, finv, tview(vg), tview(x0c), hy_bias[l][None].astype(F32)).reshape(B, L, HY_WIDTH)

    y_ret = _retention(u, lgf, lgb, s0f, s0b, ret_gn_g[l][None].astype(F32))

    w_out_b = w_out[l].astype(BF16)
    mod2 = mx[:, 2:6]
    return _out_mlp(x, y_hy, y_ret, w_out_b[:HY_WIDTH], w_out_b[HY_WIDTH:], w_mlp1[l].astype(BF16),
                    w_mlp2[l].astype(BF16), norm2_g[l][None], norm_f_g[None], mod2, tm=512)
```

```python
import functools
import math

import numpy as np
import jax
import jax.numpy as jnp
from jax import lax
from jax.experimental import pallas as pl
from jax.experimental.pallas import tpu as pltpu

F32 = jnp.float32
BF16 = jnp.bfloat16
HIGHEST = lax.Precision.HIGHEST

GRID_W = 64
HY_WIDTH = 512
RET_WIDTH = 512
RET_HEADS = 8
RET_HEAD_DIM = RET_WIDTH // RET_HEADS
HY_PROJ = 3 * HY_WIDTH
HY_BANDS = 16
HY_FILT_HID = 64
HY_DECAY_TARGET = 1e-2
HY_FAST_PCT = 0.3
HY_SLOW_PCT = 1.5
ROPE_BASE = 10000.0
ROPE_PAIRS_AXIS = RET_HEAD_DIM // 4
NORM_EPS = 1e-6

LANES = 128
HEADS_PER_BLOCK = LANES // RET_HEAD_DIM
RET_BLOCKS = RET_WIDTH // LANES
RET_CHUNK = 256
DFT_N2 = 128
DFT_ROWS = 16
DFT_KHALF = 72
DFTC_LANES = 1024
VMEM_LIMIT = 56 * 1024 * 1024


def _cparams(sem):
    return pltpu.CompilerParams(dimension_semantics=sem, vmem_limit_bytes=VMEM_LIMIT)


def _const_spec(shape):
    nd = len(shape)
    return pl.BlockSpec(shape, lambda *_: (0,) * nd)


def _silu(x):
    return x * jax.nn.sigmoid(x)


def _rms(x):
    return x * lax.rsqrt(jnp.mean(x * x, axis=-1, keepdims=True) + NORM_EPS)


def _ada_kernel(c_ref, w_ref, b_ref, o_ref):
    s = _silu(c_ref[...])
    o_ref[...] = jnp.dot(s, w_ref[...], preferred_element_type=F32, precision=HIGHEST) + b_ref[...]


def _ada(cvec, w, b):
    R, D = cvec.shape
    N = w.shape[1]
    return pl.pallas_call(
        _ada_kernel,
        grid=(N // D,),
        in_specs=[_const_spec((R, D)), pl.BlockSpec((D, D), lambda j: (0, j)),
                  pl.BlockSpec((1, D), lambda j: (0, j))],
        out_specs=pl.BlockSpec((R, D), lambda j: (0, j)),
        out_shape=jax.ShapeDtypeStruct((R, N), F32),
        compiler_params=_cparams(("arbitrary",)),
        name="ada",
    )(cvec, w, b)


INPROJ_NCHUNK = 512


HALO = 16


def _norm_mod(x, g_ref, sh_ref, sc_ref):
    h = _rms(x) * g_ref[...]
    return (h * (1.0 + sc_ref[0]) + sh_ref[0]).astype(BF16)


def _inproj_ctx_kernel(x_ref, g_ref, sh_ref, sc_ref, w_ref, o_ref):
    hb = _norm_mod(x_ref[0], g_ref, sh_ref, sc_ref)
    for c0 in range(0, w_ref.shape[1], INPROJ_NCHUNK):
        cols = slice(c0, c0 + INPROJ_NCHUNK)
        o_ref[0, :, cols] = jnp.dot(hb, w_ref[:, cols], preferred_element_type=F32).astype(BF16)


def _inproj_ctx(x, g, shift, scale, w):
    B, lc, D = x.shape
    N = w.shape[1]
    vec = pl.BlockSpec((1, 1, D), lambda b: (b, 0, 0))
    return pl.pallas_call(
        _inproj_ctx_kernel,
        grid=(B,),
        in_specs=[pl.BlockSpec((1, lc, D), lambda b: (b, 0, 0)), _const_spec((1, D)), vec, vec,
                  pl.BlockSpec((D, N), lambda b: (0, 0), pipeline_mode=pl.Buffered(1))],
        out_specs=pl.BlockSpec((1, lc, N), lambda b: (b, 0, 0)),
        out_shape=jax.ShapeDtypeStruct((B, lc, N), BF16),
        compiler_params=_cparams(("parallel",)),
        name="inproj_ctx",
    )(x, g, shift, scale, w)


def _inproj_kernel(x_ref, xp_ref, xn_ref, g_ref, sh_ref, sc_ref, w_ref, cw_ref, cb_ref, cos_ref, sin_ref,
                   vg_ref, x0_ref, ur_ref):
    i = pl.program_id(1)
    tm = x_ref.shape[1]
    ext = tm + 2 * HALO
    hb = _norm_mod(jnp.concatenate([xp_ref[0, 0], x_ref[0], xn_ref[0, 0]], axis=0), g_ref, sh_ref, sc_ref)
    row = lax.broadcasted_iota(jnp.int32, (ext, 1), 0)
    outside = ((row < HALO) & (i == 0)) | ((row >= tm + HALO) & (i == pl.num_programs(1) - 1))
    conv = []
    for c0 in range(0, HY_PROJ, HY_WIDTH):
        cols = slice(c0, c0 + HY_WIDTH)
        r = jnp.where(outside, 0.0, jnp.dot(hb, w_ref[:, cols], preferred_element_type=F32))
        cw = cw_ref[:, cols]
        y = pltpu.roll(r, 1, 0) * cw[0:1] + r * cw[1:2] + pltpu.roll(r, ext - 1, 0) * cw[2:3] + cb_ref[:, cols]
        conv.append(y[HALO:HALO + tm])
    x0_ref[0] = conv[0].astype(BF16)
    vg_ref[0] = (conv[2] * conv[1]).astype(BF16)

    hm = hb[HALO:HALO + tm]
    even = (lax.broadcasted_iota(jnp.int32, (1, LANES), 1) % 2) == 0
    for c0 in range(HY_PROJ, w_ref.shape[1], INPROJ_NCHUNK):
        r = jnp.dot(hm, w_ref[:, c0:c0 + INPROJ_NCHUNK], preferred_element_type=F32)
        if c0 < HY_PROJ + 2 * RET_WIDTH:
            is_key = c0 >= HY_PROJ + RET_WIDTH
            cs, sn = cos_ref[...], sin_ref[...]
            parts = []
            for b0 in range(0, INPROJ_NCHUNK, LANES):
                t = r[:, b0:b0 + LANES]
                sw = jnp.where(even, pltpu.roll(t, LANES - 1, 1), pltpu.roll(t, 1, 1))
                t = t * cs + sw * sn
                parts.append(t * (RET_HEAD_DIM ** -0.5) if is_key else t)
            r = jnp.concatenate(parts, axis=1)
        ur_ref[0, :, c0 - HY_PROJ:c0 - HY_PROJ + INPROJ_NCHUNK] = r.astype(BF16)


def _inproj(x, g, shift, scale, w, conv_w, conv_b, rope, tm):
    B, L, D = x.shape
    N = w.shape[1]
    nh = tm // HALO
    nrow = L // HALO
    xh = x.reshape(B, nrow, HALO, D)
    vec = pl.BlockSpec((1, 1, D), lambda b, i: (b, 0, 0))
    row = lambda n: pl.BlockSpec((1, tm, n), lambda b, i: (b, i, 0))
    tab = pl.BlockSpec((tm, LANES), lambda b, i: (i, 0))
    return pl.pallas_call(
        _inproj_kernel,
        grid=(B, L // tm),
        in_specs=[row(D),
                  pl.BlockSpec((1, 1, HALO, D), lambda b, i: (b, jnp.maximum(i * nh - 1, 0), 0, 0)),
                  pl.BlockSpec((1, 1, HALO, D), lambda b, i: (b, jnp.minimum((i + 1) * nh, nrow - 1), 0, 0)),
                  _const_spec((1, D)), vec, vec,
                  pl.BlockSpec((D, N), lambda b, i: (0, 0), pipeline_mode=pl.Buffered(1)),
                  _const_spec(conv_w.shape), _const_spec(conv_b.shape), tab, tab],
        out_specs=[row(HY_WIDTH), row(HY_WIDTH), row(N - HY_PROJ)],
        out_shape=[jax.ShapeDtypeStruct((B, L, HY_WIDTH), BF16), jax.ShapeDtypeStruct((B, L, HY_WIDTH), BF16),
                   jax.ShapeDtypeStruct((B, L, N - HY_PROJ), BF16)],
        compiler_params=_cparams(("parallel", "parallel")),
        name="inproj",
    )(x, xh, xh, g, shift, scale, w, conv_w, conv_b, *rope)


def _block_diag_mask():
    r = lax.broadcasted_iota(jnp.int32, (LANES, LANES), 0) // RET_HEAD_DIM
    c = lax.broadcasted_iota(jnp.int32, (LANES, LANES), 1) // RET_HEAD_DIM
    return r == c


def _kt_v(k, v):
    return lax.dot_general(k, v, (((0,), (0,)), ((), ())), preferred_element_type=F32)


def _ctx_state_kernel(k_ref, v_ref, lgf_ref, lgb_ref, sf_ref, sb_ref):
    k = k_ref[0].astype(F32) * (RET_HEAD_DIM ** -0.5)
    v = v_ref[0]
    lc = k.shape[0]
    m = lax.broadcasted_iota(jnp.int32, (lc, LANES), 0).astype(F32)
    wf = jnp.exp(lgf_ref[0] * (lc - 1.0 - m))
    wb = jnp.exp(lgb_ref[0] * m)
    bd = _block_diag_mask()
    sf_ref[0, 0] = jnp.where(bd, _kt_v((k * wf).astype(BF16), v), 0.0)
    sb_ref[0, 0] = jnp.where(bd, _kt_v((k * wb).astype(BF16), v), 0.0)


def _ctx_states(uc, lgf, lgb):
    B, lc, _ = uc.shape
    kb = (HY_PROJ + RET_WIDTH) // LANES
    vb = (HY_PROJ + 2 * RET_WIDTH) // LANES
    st = jax.ShapeDtypeStruct((B, RET_BLOCKS, LANES, LANES), F32)
    sspec = pl.BlockSpec((1, 1, LANES, LANES), lambda b, p: (b, p, 0, 0))
    return pl.pallas_call(
        _ctx_state_kernel,
        grid=(B, RET_BLOCKS),
        in_specs=[pl.BlockSpec((1, lc, LANES), lambda b, p: (b, 0, kb + p)),
                  pl.BlockSpec((1, lc, LANES), lambda b, p: (b, 0, vb + p)),
                  pl.BlockSpec((1, 1, LANES), lambda b, p: (p, 0, 0)),
                  pl.BlockSpec((1, 1, LANES), lambda b, p: (p, 0, 0))],
        out_specs=[sspec, sspec],
        out_shape=[st, st],
        compiler_params=_cparams(("parallel", "parallel")),
        name="ctx_state",
    )(uc, uc, lgf, lgb)


def _filt_kernel(bands_ref, w1t_ref, w1c_ref, w1s_ref, b1_ref, fr1_ref, w2_ref, b2_ref, fr2_ref,
                 w3_ref, dl_ref, h_ref, asum_ref, *, seq_len, tl):
    i = pl.program_id(0)
    n = (lax.broadcasted_iota(jnp.int32, (tl, 1), 0) + i * tl).astype(F32)
    t = n / (seq_len - 1.0)
    w = (2.0 * math.pi / seq_len) * n
    arg = bands_ref[...] * w
    dot = functools.partial(jnp.dot, preferred_element_type=F32, precision=HIGHEST)
    pre = t * w1t_ref[...] + dot(jnp.cos(arg), w1c_ref[...]) - dot(jnp.sin(arg), w1s_ref[...]) + b1_ref[...]
    h = jnp.sin(fr1_ref[...] * pre)
    h = jnp.sin(fr2_ref[...] * (dot(h, w2_ref[...]) + b2_ref[...]))
    h = dot(h, w3_ref[...]) * jnp.exp(-t * dl_ref[...])

    @pl.when(i == 0)
    def _():
        asum_ref[...] = jnp.zeros_like(asum_ref)

    asum_ref[...] += jnp.sum(jnp.abs(h), axis=0, keepdims=True)
    h_ref[0] = h[:, :HY_WIDTH].astype(BF16)
    h_ref[1] = jnp.where(n == 0.0, 0.0, h[:, HY_WIDTH:]).astype(BF16)


def _pad2(a, rows, cols):
    return jnp.pad(a.astype(F32), ((0, rows - a.shape[0]), (0, cols - a.shape[1])))


def _filters(seq_len, w1, b1, fr1, w2, b2, fr2, w3):
    tl = 1024
    P = LANES
    bands = np.zeros((1, P), np.float32)
    bands[0, :HY_BANDS] = np.linspace(1e-4, HY_BANDS - 1, HY_BANDS, dtype=np.float32)
    deltas = np.abs(np.linspace(math.log(HY_DECAY_TARGET) / HY_SLOW_PCT,
                                math.log(HY_DECAY_TARGET) / HY_FAST_PCT, HY_WIDTH, dtype=np.float32))
    dl = np.tile(deltas, 2)[None, :]
    args = (jnp.asarray(bands),
            _pad2(w1[0:1], 1, P), _pad2(w1[1:1 + HY_BANDS], P, P), _pad2(w1[1 + HY_BANDS:], P, P),
            _pad2(b1[None], 1, P), _pad2(fr1[None], 1, P),
            _pad2(w2, P, P), _pad2(b2[None], 1, P), _pad2(fr2[None], 1, P),
            _pad2(w3, P, 2 * HY_WIDTH), jnp.asarray(dl))
    return pl.pallas_call(
        functools.partial(_filt_kernel, seq_len=seq_len, tl=tl),
        grid=(seq_len // tl,),
        in_specs=[_const_spec(a.shape) for a in args],
        out_specs=[pl.BlockSpec((2, tl, HY_WIDTH), lambda i: (0, i, 0)),
                   _const_spec((1, 2 * HY_WIDTH))],
        out_shape=[jax.ShapeDtypeStruct((2, seq_len, HY_WIDTH), BF16),
                   jax.ShapeDtypeStruct((1, 2 * HY_WIDTH), F32)],
        compiler_params=_cparams(("arbitrary",)),
        name="filt",
    )(*args)


def _dft_tables(seq_len):
    n = 2 * seq_len
    n1_full = n // DFT_N2
    n1_used = n1_full // 2
    nk1 = n1_full // 2 + 1
    k1 = np.arange(nk1)[:, None]
    n1 = np.arange(n1_used)[None, :]
    th = 2.0 * np.pi * ((k1 * n1) % n1_full) / n1_full
    ck = np.full((nk1, 1), 2.0)
    ck[0, 0] = 1.0
    ck[-1, 0] = 1.0
    assert nk1 <= DFT_KHALF
    f1 = np.zeros((2 * DFT_KHALF, n1_used))
    f1[:nk1] = np.cos(th)
    f1[DFT_KHALF:DFT_KHALF + nk1] = -np.sin(th)
    finv = np.zeros((n1_used, 2 * DFT_KHALF))
    finv[:, 0:2 * nk1:2] = (ck * np.cos(th) / n).T
    finv[:, 1:2 * nk1:2] = (-ck * np.sin(th) / n).T
    k1 = np.arange(nk1)[:, None, None]
    k2 = np.arange(DFT_N2)[None, :, None]
    n2 = np.arange(DFT_N2)[None, None, :]
    ph = 2.0 * np.pi * ((n1_full * n2 * k2 + n2 * k1) % n) / n
    gr, gi = np.cos(ph), -np.sin(ph)
    g = np.concatenate([np.concatenate([gr, -gi], 2), np.concatenate([gi, gr], 2)], 1)
    grt, git = np.swapaxes(gr, 1, 2), -np.swapaxes(gi, 1, 2)
    ginv = np.concatenate([np.concatenate([grt, -git], 2), np.concatenate([git, grt], 2)], 1)
    f = lambda a: jnp.asarray(a.astype(np.float32))
    return f(f1), f(finv), f(g), f(ginv), nk1


def _rows2d(a):
    return a.reshape(a.shape[0] * a.shape[1], a.shape[2])


def _dft_a_kernel(x_ref, f_ref, o_ref):
    n1u = x_ref.shape[1]
    xs = _rows2d(jnp.swapaxes(x_ref[0], 0, 1))
    f = f_ref[...]
    res = [jnp.dot(f, xs[r * n1u:(r + 1) * n1u], preferred_element_type=F32).astype(BF16)
           for r in range(DFT_ROWS)]
    a = jnp.swapaxes(jnp.stack(res, axis=0), 0, 1)
    o_ref[:, 0] = a[:DFT_KHALF]
    o_ref[:, 1] = a[DFT_KHALF:]


def _dft_a(xv, f1):
    Bd, n1u, n2, C = xv.shape
    return pl.pallas_call(
        _dft_a_kernel,
        grid=(Bd, n2 // DFT_ROWS),
        in_specs=[pl.BlockSpec((1, n1u, DFT_ROWS, C), lambda b, j: (b, 0, j, 0)), _const_spec(f1.shape)],
        out_specs=pl.BlockSpec((DFT_KHALF, 2, DFT_ROWS, C), lambda b, j: (0, 0, j, b)),
        out_shape=jax.ShapeDtypeStruct((DFT_KHALF, 2, n2, Bd * C), BF16),
        compiler_params=_cparams(("parallel", "parallel")),
        name="dft_a",
    )(xv, f1)


def _dft_c_kernel(a_ref, h_ref, g_ref, gi_ref, asum_ref, o_ref):
    g = g_ref[0].astype(BF16)
    gi = gi_ref[0].astype(BF16)
    rows = g.shape[0]
    half = rows // 2
    C = HY_WIDTH
    xh = jnp.dot(g, h_ref[0].reshape(rows, h_ref.shape[3]), preferred_element_type=F32)
    asum = asum_ref[...]
    inv = 1.0 / (asum[:, :C] + asum[:, C:] + 1e-6)
    kr = (xh[:half, :C] + xh[:half, C:]) * inv
    ki = (xh[half:, :C] - xh[half:, C:]) * inv
    step = min(DFTC_LANES, a_ref.shape[3])
    for l0 in range(0, a_ref.shape[3], step):
        x = jnp.dot(g, a_ref[0, :, :, l0:l0 + step].reshape(rows, step), preferred_element_type=F32)
        ys = []
        for c0 in range(0, step, C):
            xr, xi = x[:half, c0:c0 + C], x[half:, c0:c0 + C]
            ys.append(jnp.concatenate([xr * kr - xi * ki, xr * ki + xi * kr], axis=0).astype(BF16))
        bk = jnp.dot(gi, jnp.concatenate(ys, axis=1), preferred_element_type=F32).astype(BF16)
        o_ref[0, :, :, l0:l0 + step] = bk.reshape(2, half, step)


def _dft_c(a, h, g, ginv, asum, nk1):
    _, _, n2, bc = a.shape
    dspec = lambda w: pl.BlockSpec((1, 2, n2, w), lambda k: (k, 0, 0, 0))
    gspec = pl.BlockSpec((1, 2 * n2, 2 * n2), lambda k: (k, 0, 0))
    return pl.pallas_call(
        _dft_c_kernel,
        grid=(nk1,),
        in_specs=[dspec(bc), dspec(h.shape[3]), gspec, gspec, _const_spec(asum.shape)],
        out_specs=dspec(bc),
        out_shape=jax.ShapeDtypeStruct((nk1, 2, n2, bc), BF16),
        compiler_params=_cparams(("parallel",)),
        name="dft_c",
    )(a, h, g, ginv, asum)


def _idft_a_kernel(b_ref, f_ref, vg_ref, x0_ref, bias_ref, o_ref):
    nk1, _, R, C = b_ref.shape
    mp = f_ref.shape[1]
    bv = b_ref[...].reshape(2 * nk1, R, C)
    bv = jnp.concatenate([bv, jnp.zeros((mp - 2 * nk1, R, C), BF16)], axis=0)
    bs = _rows2d(jnp.swapaxes(bv, 0, 1))
    f = f_ref[...]
    ys = [jnp.dot(f, bs[r * mp:(r + 1) * mp], preferred_element_type=F32).astype(BF16) for r in range(R)]
    y = _rows2d(jnp.swapaxes(jnp.stack(ys, axis=0), 0, 1)).astype(F32)
    y = y + _rows2d(vg_ref[0]).astype(F32) * bias_ref[...]
    o_ref[0] = (y * _rows2d(x0_ref[0]).astype(F32)).astype(BF16).reshape(o_ref.shape[1:])


def _idft_a(bk, finv, vgv, x0v, bias):
    B, n1u, n2, C = vgv.shape
    nk1 = bk.shape[0]
    dsp = pl.BlockSpec((1, n1u, DFT_ROWS, C), lambda b, j: (b, 0, j, 0))
    return pl.pallas_call(
        _idft_a_kernel,
        grid=(B, n2 // DFT_ROWS),
        in_specs=[pl.BlockSpec((nk1, 2, DFT_ROWS, C), lambda b, j: (0, 0, j, b)),
                  _const_spec(finv.shape), dsp, dsp, _const_spec(bias.shape)],
        out_specs=dsp,
        out_shape=jax.ShapeDtypeStruct((B, n1u, n2, C), BF16),
        compiler_params=_cparams(("parallel", "parallel")),
        name="idft_a",
    )(bk, finv, vgv, x0v, bias)


def _retention_kernel(q_ref, k_ref, v_ref, g_ref, lgf_ref, lgb_ref, s0f_ref, s0b_ref, gn_ref, o_ref,
                      inc_ref, st_ref, dm_ref, wt_ref):
    C = RET_CHUNK
    seq_len = q_ref.shape[1]
    nc = seq_len // C
    lgf = lgf_ref[0]
    lgb = lgb_ref[0]
    lane = lax.broadcasted_iota(jnp.int32, (1, LANES), 1)
    head_of_lane = lane // RET_HEAD_DIM
    bd = _block_diag_mask()

    ri = lax.broadcasted_iota(jnp.int32, (C, LANES), 0).astype(F32)
    wt_ref[0] = jnp.exp(lgf * (ri + 1.0)).astype(BF16)
    wt_ref[1] = jnp.exp(lgb * (C - ri)).astype(BF16)
    wt_ref[2] = jnp.exp(lgf * (C - 1.0 - ri)).astype(BF16)
    wt_ref[3] = jnp.exp(lgb * ri).astype(BF16)
    di = lax.broadcasted_iota(jnp.int32, (C, C), 0)
    dj = lax.broadcasted_iota(jnp.int32, (C, C), 1)
    dd = (di - dj).astype(F32)
    for h in range(HEADS_PER_BLOCK):
        lf = jnp.sum(jnp.where(lane == h * RET_HEAD_DIM, lgf, 0.0), axis=1, keepdims=True)
        lb = jnp.sum(jnp.where(lane == h * RET_HEAD_DIM, lgb, 0.0), axis=1, keepdims=True)
        dm_ref[h] = jnp.where(dd > 0, jnp.exp(lf * dd), jnp.where(dd < 0, jnp.exp(-lb * dd), 2.0)).astype(BF16)

    def chunk(c):
        return pl.ds(pl.multiple_of(c * C, C), C)

    def inc_body(c, carry):
        rows = chunk(c)
        k = k_ref[0, rows, :]
        kw = jnp.concatenate([k * wt_ref[2], k * wt_ref[3]], axis=1)
        t = _kt_v(kw, v_ref[0, rows, :])
        inc_ref[c, :LANES] = jnp.where(bd, t[:LANES], 0.0)
        inc_ref[c, LANES:] = jnp.where(bd, t[LANES:], 0.0)
        return carry

    lax.fori_loop(0, nc, inc_body, 0, unroll=4)

    gcf = jnp.exp(lgf * float(C))
    gcb = jnp.exp(lgb * float(C))

    def fscan(c, s):
        st_ref[c, :LANES] = s.astype(BF16)
        return s * gcf + inc_ref[c, :LANES]

    lax.fori_loop(0, nc, fscan, s0f_ref[0, 0])

    def bscan(i, s):
        c = nc - 1 - i
        st_ref[c, LANES:] = s.astype(BF16)
        return s * gcb + inc_ref[c, LANES:]

    lax.fori_loop(0, nc, bscan, s0b_ref[0, 0])

    gn = gn_ref[...]
    first = head_of_lane == 0
    inv_n = 1.0 / RET_HEAD_DIM

    def head_mean(a):
        tot = jnp.sum(a, axis=1, keepdims=True)
        s0 = jnp.sum(jnp.where(first, a, 0.0), axis=1, keepdims=True)
        return jnp.where(first, s0, tot - s0) * inv_n

    def out_body(c, carry):
        rows = chunk(c)
        q = q_ref[0, rows, :]
        k = k_ref[0, rows, :]
        v = v_ref[0, rows, :]
        qw = jnp.concatenate([q * wt_ref[0], q * wt_ref[1]], axis=1)
        o = jnp.dot(qw, st_ref[c], preferred_element_type=F32)
        oh = []
        for h in range(HEADS_PER_BLOCK):
            qm = jnp.where(head_of_lane == h, q, jnp.zeros_like(q))
            sc = lax.dot_general(qm, k, (((1,), (1,)), ((), ())), preferred_element_type=F32)
            oh.append(jnp.dot(sc.astype(BF16) * dm_ref[h], v, preferred_element_type=F32))
        o = o + jnp.where(first, oh[0], oh[1])
        d = o - head_mean(o)
        y = d * lax.rsqrt(head_mean(d * d) + NORM_EPS) * gn
        o_ref[0, rows, :] = (y * _silu(g_ref[0, rows, :].astype(F32))).astype(BF16)
        return carry

    lax.fori_loop(0, nc, out_body, 0, unroll=2)


def _retention(u, lgf, lgb, s0f, s0b, gn):
    B, L, _ = u.shape
    assert HEADS_PER_BLOCK == 2
    nc = L // RET_CHUNK
    col = lambda g: pl.BlockSpec((1, L, LANES), lambda b, p: (b, 0, g * RET_BLOCKS + p))
    lsp = pl.BlockSpec((1, 1, LANES), lambda b, p: (p, 0, 0))
    ssp = pl.BlockSpec((1, 1, LANES, LANES), lambda b, p: (b, p, 0, 0))
    return pl.pallas_call(
        _retention_kernel,
        grid=(B, RET_BLOCKS),
        in_specs=[col(0), col(1), col(2), col(3), lsp, lsp, ssp, ssp,
                  pl.BlockSpec((1, LANES), lambda b, p: (0, p))],
        out_specs=pl.BlockSpec((1, L, LANES), lambda b, p: (b, 0, p)),
        out_shape=jax.ShapeDtypeStruct((B, L, RET_WIDTH), BF16),
        scratch_shapes=[pltpu.VMEM((nc, 2 * LANES, LANES), F32),
                        pltpu.VMEM((nc, 2 * LANES, LANES), BF16),
                        pltpu.VMEM((HEADS_PER_BLOCK, RET_CHUNK, RET_CHUNK), BF16),
                        pltpu.VMEM((4, RET_CHUNK, LANES), BF16)],
        compiler_params=_cparams(("parallel", "parallel")),
        name="retention",
    )(u, u, u, u, lgf, lgb, s0f, s0b, gn)


def _rope_tables(seq_len):
    rows = seq_len // GRID_W
    r, col = jnp.meshgrid(jnp.arange(rows, dtype=F32), jnp.arange(GRID_W, dtype=F32), indexing="ij")
    inv = ROPE_BASE ** (-jnp.arange(ROPE_PAIRS_AXIS, dtype=F32) / ROPE_PAIRS_AXIS)
    ang = jnp.concatenate([r.reshape(-1, 1) * inv, col.reshape(-1, 1) * inv], axis=-1)
    ang = jnp.tile(jnp.repeat(ang, 2, axis=1), (1, HEADS_PER_BLOCK))
    sign = jnp.where(jnp.arange(LANES) % 2 == 0, -1.0, 1.0).astype(F32)
    return jnp.cos(ang), jnp.sin(ang) * sign


def _out_mlp_kernel(x_ref, yh_ref, yr_ref, wo1_ref, wo2_ref, w1_ref, w2_ref, g2_ref, gf_ref, mod_ref, o_ref,
                    *, ff_chunk):
    mod = mod_ref[0]
    mix = (jnp.dot(yh_ref[0], wo1_ref[...], preferred_element_type=F32)
           + jnp.dot(yr_ref[0], wo2_ref[...], preferred_element_type=F32))
    x1 = x_ref[0] + mod[0:1] * mix
    h = (_rms(x1) * g2_ref[...] * (1.0 + mod[2:3]) + mod[1:2]).astype(BF16)
    acc = jnp.zeros_like(x1)
    for j in range(w1_ref.shape[1] // ff_chunk):
        cols = slice(j * ff_chunk, (j + 1) * ff_chunk)
        a = jnp.maximum(jnp.dot(h, w1_ref[:, cols], preferred_element_type=F32), 0.0)
        acc = acc + jnp.dot((a * a).astype(BF16), w2_ref[cols, :], preferred_element_type=F32)
    x2 = x1 + mod[3:4] * acc
    o_ref[0] = _rms(x2) * gf_ref[...]


def _out_mlp(x, yh, yr, wo1, wo2, w1, w2, g2, gf, mod, tm):
    B, L, D = x.shape
    dff = w1.shape[1]
    row = lambda n: pl.BlockSpec((1, tm, n), lambda b, i: (b, i, 0))
    wsp = lambda a: pl.BlockSpec(a.shape, lambda b, i: (0, 0), pipeline_mode=pl.Buffered(1))
    return pl.pallas_call(
        functools.partial(_out_mlp_kernel, ff_chunk=1024),
        grid=(B, L // tm),
        in_specs=[row(D), row(yh.shape[2]), row(yr.shape[2]), wsp(wo1), wsp(wo2), wsp(w1), wsp(w2),
                  _const_spec((1, D)), _const_spec((1, D)),
                  pl.BlockSpec((1, 4, D), lambda b, i: (b, 0, 0))],
        out_specs=row(D),
        out_shape=jax.ShapeDtypeStruct((B, L, D), F32),
        compiler_params=_cparams(("parallel", "parallel")),
        name="out_mlp",
    )(x, yh, yr, wo1, wo2, w1, w2, g2, gf, mod)


def kernel(x, c, ctx, c_ctx, w_ada, b_ada, norm1_g, w_in, hy_conv_w, hy_conv_b, hy_f_w1, hy_f_b1, hy_f_freq1,
           hy_f_w2, hy_f_b2, hy_f_freq2, hy_f_w3, hy_bias, ret_decay_logit, ret_gn_g, w_out, norm2_g, w_mlp1,
           w_mlp2, norm_f_g):
    B, L, D = x.shape
    depth = w_ada.shape[0]
    assert depth == 1, "single-layer block"
    l = 0
    rows = 16
    cvec = jnp.zeros((rows, D), F32).at[:B].set(c).at[B].set(c_ctx)
    mod = _ada(cvec, w_ada[l], b_ada[l][None])
    mx = mod[:B].reshape(B, 6, D)
    mc = mod[B].reshape(6, D)

    lg = jax.nn.log_sigmoid(ret_decay_logit[l].astype(F32))
    lg_lanes = jnp.repeat(lg, RET_HEAD_DIM, axis=1).reshape(2, RET_BLOCKS, 1, LANES)
    lgf, lgb = lg_lanes[0], lg_lanes[1]

    w_in_b = w_in[l].astype(BF16)
    g1 = norm1_g[l][None]
    vg, x0c, u_ret = _inproj(x, g1, mx[:, 0:1], mx[:, 1:2], w_in_b, hy_conv_w[l].astype(F32),
                             hy_conv_b[l][None].astype(F32), _rope_tables(L), tm=512)
    ones = jnp.ones((B, 1, 1), F32)
    uc = _inproj_ctx(ctx, g1, ones * mc[0], ones * mc[1], w_in_b)
    s0f, s0b = _ctx_states(uc, lgf, lgb)

    f1, finv, gmat, ginv, nk1 = _dft_tables(L)
    f1, finv = f1.astype(BF16), finv.astype(BF16)
    taps, asum = _filters(L, hy_f_w1[l], hy_f_b1[l], hy_f_freq1[l], hy_f_w2[l], hy_f_b2[l], hy_f_freq2[l],
                          hy_f_w3[l])
    tview = lambda a: a.reshape(a.shape[0], L // DFT_N2, DFT_N2, HY_WIDTH)
    bk = _dft_c(_dft_a(tview(vg), f1), _dft_a(tview(taps), f1), gmat, ginv, asum, nk1)
    y_hy = _idft_a(bk, finv, tview(vg), tview(x0c), hy_bias[l][None].astype(F32)).reshape(B, L, HY_WIDTH)

    y_ret = _retention(u_ret, lgf, lgb, s0f, s0b, ret_gn_g[l][None].astype(F32))

    w_out_b = w_out[l].astype(BF16)
    mod2 = mx[:, 2:6]
    return _out_mlp(x, y_hy, y_ret, w_out_b[:HY_WIDTH], w_out_b[HY_WIDTH:], w_mlp1[l].astype(BF16),
                    w_mlp2[l].astype(BF16), norm2_g[l][None], norm_f_g[None], mod2, tm=512)
```

```python
import functools
import math

import numpy as np
import jax
import jax.numpy as jnp
from jax import lax
from jax.experimental import pallas as pl
from jax.experimental.pallas import tpu as pltpu

F32 = jnp.float32
BF16 = jnp.bfloat16
HIGHEST = lax.Precision.HIGHEST

GRID_W = 64
HY_WIDTH = 512
RET_WIDTH = 512
RET_HEADS = 8
RET_HEAD_DIM = RET_WIDTH // RET_HEADS
HY_PROJ = 3 * HY_WIDTH
HY_BANDS = 16
HY_FILT_HID = 64
HY_DECAY_TARGET = 1e-2
HY_FAST_PCT = 0.3
HY_SLOW_PCT = 1.5
ROPE_BASE = 10000.0
ROPE_PAIRS_AXIS = RET_HEAD_DIM // 4
NORM_EPS = 1e-6

LANES = 128
HEADS_PER_BLOCK = LANES // RET_HEAD_DIM
RET_BLOCKS = RET_WIDTH // LANES
RET_CHUNK = 256
DFT_N2 = 128
DFT_ROWS = 16
DFT_KHALF = 72
DFTC_LANES = 1024
VMEM_LIMIT = 56 * 1024 * 1024


def _cparams(sem):
    return pltpu.CompilerParams(dimension_semantics=sem, vmem_limit_bytes=VMEM_LIMIT)


def _const_spec(shape):
    nd = len(shape)
    return pl.BlockSpec(shape, lambda *_: (0,) * nd)


def _silu(x):
    return x * jax.nn.sigmoid(x)


def _rms(x):
    return x * lax.rsqrt(jnp.mean(x * x, axis=-1, keepdims=True) + NORM_EPS)


def _ada_kernel(c_ref, w_ref, b_ref, o_ref):
    s = _silu(c_ref[...])
    o_ref[...] = jnp.dot(s, w_ref[...], preferred_element_type=F32, precision=HIGHEST) + b_ref[...]


def _ada(cvec, w, b):
    R, D = cvec.shape
    N = w.shape[1]
    return pl.pallas_call(
        _ada_kernel,
        grid=(N // D,),
        in_specs=[_const_spec((R, D)), pl.BlockSpec((D, D), lambda j: (0, j)),
                  pl.BlockSpec((1, D), lambda j: (0, j))],
        out_specs=pl.BlockSpec((R, D), lambda j: (0, j)),
        out_shape=jax.ShapeDtypeStruct((R, N), F32),
        compiler_params=_cparams(("arbitrary",)),
        name="ada",
    )(cvec, w, b)


INPROJ_NCHUNK = 512


HALO = 16


def _norm_mod(x, g_ref, sh_ref, sc_ref):
    h = _rms(x) * g_ref[...]
    return (h * (1.0 + sc_ref[0]) + sh_ref[0]).astype(BF16)


def _inproj_ctx_kernel(x_ref, g_ref, sh_ref, sc_ref, w_ref, o_ref):
    hb = _norm_mod(x_ref[0], g_ref, sh_ref, sc_ref)
    for c0 in range(0, w_ref.shape[1], INPROJ_NCHUNK):
        cols = slice(c0, c0 + INPROJ_NCHUNK)
        o_ref[0, :, cols] = jnp.dot(hb, w_ref[:, cols], preferred_element_type=F32).astype(BF16)


def _inproj_ctx(x, g, shift, scale, w):
    B, lc, D = x.shape
    N = w.shape[1]
    vec = pl.BlockSpec((1, 1, D), lambda b: (b, 0, 0))
    return pl.pallas_call(
        _inproj_ctx_kernel,
        grid=(B,),
        in_specs=[pl.BlockSpec((1, lc, D), lambda b: (b, 0, 0)), _const_spec((1, D)), vec, vec,
                  pl.BlockSpec((D, N), lambda b: (0, 0), pipeline_mode=pl.Buffered(1))],
        out_specs=pl.BlockSpec((1, lc, N), lambda b: (b, 0, 0)),
        out_shape=jax.ShapeDtypeStruct((B, lc, N), BF16),
        compiler_params=_cparams(("parallel",)),
        name="inproj_ctx",
    )(x, g, shift, scale, w)


def _inproj_kernel(x_ref, xp_ref, xn_ref, g_ref, sh_ref, sc_ref, w_ref, cw_ref, cb_ref, cos_ref, sin_ref,
                   vg_ref, x0_ref, ur_ref):
    i = pl.program_id(1)
    tm = x_ref.shape[1]
    ext = tm + 2 * HALO
    hb = _norm_mod(jnp.concatenate([xp_ref[0, 0], x_ref[0], xn_ref[0, 0]], axis=0), g_ref, sh_ref, sc_ref)
    row = lax.broadcasted_iota(jnp.int32, (ext, 1), 0)
    outside = ((row < HALO) & (i == 0)) | ((row >= tm + HALO) & (i == pl.num_programs(1) - 1))
    hb = jnp.where(outside, jnp.zeros_like(hb), hb)
    conv = []
    for c0 in range(0, HY_PROJ, HY_WIDTH):
        cols = slice(c0, c0 + HY_WIDTH)
        r = jnp.dot(hb, w_ref[:, cols], preferred_element_type=F32)
        cw = cw_ref[:, cols]
        y = pltpu.roll(r, 1, 0) * cw[0:1] + r * cw[1:2] + pltpu.roll(r, ext - 1, 0) * cw[2:3] + cb_ref[:, cols]
        conv.append(y[HALO:HALO + tm])
    x0_ref[0] = conv[0].astype(BF16)
    vg_ref[0] = (conv[2] * conv[1]).astype(BF16)

    hm = hb[HALO:HALO + tm]
    even = (lax.broadcasted_iota(jnp.int32, (1, LANES), 1) % 2) == 0
    for c0 in range(HY_PROJ, w_ref.shape[1], INPROJ_NCHUNK):
        r = jnp.dot(hm, w_ref[:, c0:c0 + INPROJ_NCHUNK], preferred_element_type=F32)
        if c0 < HY_PROJ + 2 * RET_WIDTH:
            is_key = c0 >= HY_PROJ + RET_WIDTH
            cs, sn = cos_ref[...], sin_ref[...]
            parts = []
            for b0 in range(0, INPROJ_NCHUNK, LANES):
                t = r[:, b0:b0 + LANES]
                sw = jnp.where(even, pltpu.roll(t, LANES - 1, 1), pltpu.roll(t, 1, 1))
                t = t * cs + sw * sn
                parts.append(t * (RET_HEAD_DIM ** -0.5) if is_key else t)
            r = jnp.concatenate(parts, axis=1)
        ur_ref[0, :, c0 - HY_PROJ:c0 - HY_PROJ + INPROJ_NCHUNK] = r.astype(BF16)


def _inproj(x, g, shift, scale, w, conv_w, conv_b, rope, tm):
    B, L, D = x.shape
    N = w.shape[1]
    nh = tm // HALO
    nrow = L // HALO
    xh = x.reshape(B, nrow, HALO, D)
    vec = pl.BlockSpec((1, 1, D), lambda b, i: (b, 0, 0))
    row = lambda n: pl.BlockSpec((1, tm, n), lambda b, i: (b, i, 0))
    tab = pl.BlockSpec((tm, LANES), lambda b, i: (i, 0))
    return pl.pallas_call(
        _inproj_kernel,
        grid=(B, L // tm),
        in_specs=[row(D),
                  pl.BlockSpec((1, 1, HALO, D), lambda b, i: (b, jnp.maximum(i * nh - 1, 0), 0, 0)),
                  pl.BlockSpec((1, 1, HALO, D), lambda b, i: (b, jnp.minimum((i + 1) * nh, nrow - 1), 0, 0)),
                  _const_spec((1, D)), vec, vec,
                  pl.BlockSpec((D, N), lambda b, i: (0, 0), pipeline_mode=pl.Buffered(1)),
                  _const_spec(conv_w.shape), _const_spec(conv_b.shape), tab, tab],
        out_specs=[row(HY_WIDTH), row(HY_WIDTH), row(N - HY_PROJ)],
        out_shape=[jax.ShapeDtypeStruct((B, L, HY_WIDTH), BF16), jax.ShapeDtypeStruct((B, L, HY_WIDTH), BF16),
                   jax.ShapeDtypeStruct((B, L, N - HY_PROJ), BF16)],
        compiler_params=_cparams(("parallel", "parallel")),
        name="inproj",
    )(x, xh, xh, g, shift, scale, w, conv_w, conv_b, *rope)


def _block_diag_mask():
    r = lax.broadcasted_iota(jnp.int32, (LANES, LANES), 0) // RET_HEAD_DIM
    c = lax.broadcasted_iota(jnp.int32, (LANES, LANES), 1) // RET_HEAD_DIM
    return r == c


def _kt_v(k, v):
    return lax.dot_general(k, v, (((0,), (0,)), ((), ())), preferred_element_type=F32)


def _ctx_state_kernel(k_ref, v_ref, lgf_ref, lgb_ref, sf_ref, sb_ref):
    k = k_ref[0].astype(F32) * (RET_HEAD_DIM ** -0.5)
    v = v_ref[0]
    lc = k.shape[0]
    m = lax.broadcasted_iota(jnp.int32, (lc, LANES), 0).astype(F32)
    wf = jnp.exp(lgf_ref[0] * (lc - 1.0 - m))
    wb = jnp.exp(lgb_ref[0] * m)
    bd = _block_diag_mask()
    sf_ref[0, 0] = jnp.where(bd, _kt_v((k * wf).astype(BF16), v), 0.0)
    sb_ref[0, 0] = jnp.where(bd, _kt_v((k * wb).astype(BF16), v), 0.0)


def _ctx_states(uc, lgf, lgb):
    B, lc, _ = uc.shape
    kb = (HY_PROJ + RET_WIDTH) // LANES
    vb = (HY_PROJ + 2 * RET_WIDTH) // LANES
    st = jax.ShapeDtypeStruct((B, RET_BLOCKS, LANES, LANES), F32)
    sspec = pl.BlockSpec((1, 1, LANES, LANES), lambda b, p: (b, p, 0, 0))
    return pl.pallas_call(
        _ctx_state_kernel,
        grid=(B, RET_BLOCKS),
        in_specs=[pl.BlockSpec((1, lc, LANES), lambda b, p: (b, 0, kb + p)),
                  pl.BlockSpec((1, lc, LANES), lambda b, p: (b, 0, vb + p)),
                  pl.BlockSpec((1, 1, LANES), lambda b, p: (p, 0, 0)),
                  pl.BlockSpec((1, 1, LANES), lambda b, p: (p, 0, 0))],
        out_specs=[sspec, sspec],
        out_shape=[st, st],
        compiler_params=_cparams(("parallel", "parallel")),
        name="ctx_state",
    )(uc, uc, lgf, lgb)


FILT_ROWS = 1024
FILT_GROUPS = LANES // HY_BANDS
FILT_HALVES = LANES // HY_FILT_HID


def _filt_kernel(bands_ref, w1t_ref, w1c_ref, w1s_ref, b1_ref, fr1_ref, w2_ref, b2_ref, fr2_ref,
                 w3_ref, dl_ref, h_ref, asum_ref, *, seq_len):
    i = pl.program_id(0)
    tl = FILT_ROWS
    gr = tl // FILT_GROUPS
    hr = tl // FILT_HALVES
    lane = lax.broadcasted_iota(jnp.int32, (1, LANES), 1)
    base = i * tl
    dot = functools.partial(jnp.dot, preferred_element_type=F32, precision=HIGHEST)

    n8 = (lax.broadcasted_iota(jnp.int32, (gr, 1), 0) + (lane // HY_BANDS) * gr + base).astype(F32)
    arg = bands_ref[...] * ((2.0 * math.pi / seq_len) * n8)
    cz, sz = jnp.cos(arg), jnp.sin(arg)
    nh = (lax.broadcasted_iota(jnp.int32, (hr, 1), 0) + (lane // HY_FILT_HID) * hr + base).astype(F32)
    blocks = [dot(cz, w1c_ref[r]) - dot(sz, w1s_ref[r]) for r in range(hr // gr)]
    pre = jnp.concatenate(blocks, axis=0) + (nh / (seq_len - 1.0)) * w1t_ref[...] + b1_ref[...]
    h = jnp.sin(fr1_ref[...] * pre)
    h = jnp.sin(fr2_ref[...] * (dot(h, w2_ref[...]) + b2_ref[...])).astype(BF16)

    @pl.when(i == 0)
    def _():
        asum_ref[...] = jnp.zeros_like(asum_ref)

    for p in range(FILT_HALVES):
        n = (lax.broadcasted_iota(jnp.int32, (hr, 1), 0) + p * hr + base).astype(F32)
        t = n / (seq_len - 1.0)
        taps = jnp.dot(h, w3_ref[p], preferred_element_type=F32) * jnp.exp(-t * dl_ref[...])
        asum_ref[...] += jnp.sum(jnp.abs(taps), axis=0, keepdims=True)
        rows = slice(p * hr, (p + 1) * hr)
        h_ref[0, rows] = taps[:, :HY_WIDTH].astype(BF16)
        h_ref[1, rows] = jnp.where(n == 0.0, 0.0, taps[:, HY_WIDTH:]).astype(BF16)


def _filters(seq_len, w1, b1, fr1, w2, b2, fr2, w3):
    hid = HY_FILT_HID
    f32 = lambda a: a.astype(F32)
    bands = np.tile(np.linspace(1e-4, HY_BANDS - 1, HY_BANDS, dtype=np.float32), FILT_GROUPS)[None]
    deltas = np.abs(np.linspace(math.log(HY_DECAY_TARGET) / HY_SLOW_PCT,
                                math.log(HY_DECAY_TARGET) / HY_FAST_PCT, HY_WIDTH, dtype=np.float32))
    dl = np.tile(deltas, 2)[None, :]
    halves = lambda v: jnp.tile(f32(v)[None], (1, FILT_HALVES))

    def place_l1(wb):
        nb = FILT_GROUPS // FILT_HALVES
        out = jnp.zeros((nb, LANES, LANES), F32)
        for r in range(nb):
            for p in range(FILT_HALVES):
                s = p * nb + r
                out = out.at[r, s * HY_BANDS:(s + 1) * HY_BANDS, p * hid:(p + 1) * hid].set(f32(wb))
        return out

    w2d = jnp.zeros((LANES, LANES), F32)
    w3p = jnp.zeros((FILT_HALVES, LANES, 2 * HY_WIDTH), F32)
    for p in range(FILT_HALVES):
        w2d = w2d.at[p * hid:(p + 1) * hid, p * hid:(p + 1) * hid].set(f32(w2))
        w3p = w3p.at[p, p * hid:(p + 1) * hid].set(f32(w3))
    args = (jnp.asarray(bands), halves(w1[0]), place_l1(w1[1:1 + HY_BANDS]), place_l1(w1[1 + HY_BANDS:]),
            halves(b1), halves(fr1), w2d, halves(b2), halves(fr2), w3p.astype(BF16), jnp.asarray(dl))
    return pl.pallas_call(
        functools.partial(_filt_kernel, seq_len=seq_len),
        grid=(seq_len // FILT_ROWS,),
        in_specs=[_const_spec(a.shape) for a in args],
        out_specs=[pl.BlockSpec((2, FILT_ROWS, HY_WIDTH), lambda i: (0, i, 0)),
                   _const_spec((1, 2 * HY_WIDTH))],
        out_shape=[jax.ShapeDtypeStruct((2, seq_len, HY_WIDTH), BF16),
                   jax.ShapeDtypeStruct((1, 2 * HY_WIDTH), F32)],
        compiler_params=_cparams(("arbitrary",)),
        name="filt",
    )(*args)


def _dft_tables(seq_len):
    n = 2 * seq_len
    n1_full = n // DFT_N2
    n1_used = n1_full // 2
    nk1 = n1_full // 2 + 1
    k1 = np.arange(nk1)[:, None]
    n1 = np.arange(n1_used)[None, :]
    th = 2.0 * np.pi * ((k1 * n1) % n1_full) / n1_full
    ck = np.full((nk1, 1), 2.0)
    ck[0, 0] = 1.0
    ck[-1, 0] = 1.0
    assert nk1 <= DFT_KHALF
    f1 = np.zeros((2 * DFT_KHALF, n1_used))
    f1[:nk1] = np.cos(th)
    f1[DFT_KHALF:DFT_KHALF + nk1] = -np.sin(th)
    finv = np.zeros((n1_used, 2 * DFT_KHALF))
    finv[:, 0:2 * nk1:2] = (ck * np.cos(th) / n).T
    finv[:, 1:2 * nk1:2] = (-ck * np.sin(th) / n).T
    k1 = np.arange(nk1)[:, None, None]
    k2 = np.arange(DFT_N2)[None, :, None]
    n2 = np.arange(DFT_N2)[None, None, :]
    ph = 2.0 * np.pi * ((n1_full * n2 * k2 + n2 * k1) % n) / n
    gr, gi = np.cos(ph), -np.sin(ph)
    g = np.concatenate([np.concatenate([gr, -gi], 2), np.concatenate([gi, gr], 2)], 1)
    grt, git = np.swapaxes(gr, 1, 2), -np.swapaxes(gi, 1, 2)
    ginv = np.concatenate([np.concatenate([grt, -git], 2), np.concatenate([git, grt], 2)], 1)
    f = lambda a: jnp.asarray(a.astype(np.float32))
    return f(f1), f(finv), f(g), f(ginv), nk1


def _rows2d(a):
    return a.reshape(a.shape[0] * a.shape[1], a.shape[2])


def _dft_a_kernel(x_ref, f_ref, o_ref):
    n1u = x_ref.shape[1]
    xs = _rows2d(jnp.swapaxes(x_ref[0], 0, 1))
    f = f_ref[...]
    res = [jnp.dot(f, xs[r * n1u:(r + 1) * n1u], preferred_element_type=F32).astype(BF16)
           for r in range(DFT_ROWS)]
    a = jnp.swapaxes(jnp.stack(res, axis=0), 0, 1)
    o_ref[:, 0] = a[:DFT_KHALF]
    o_ref[:, 1] = a[DFT_KHALF:]


def _dft_a(xv, f1):
    Bd, n1u, n2, C = xv.shape
    return pl.pallas_call(
        _dft_a_kernel,
        grid=(Bd, n2 // DFT_ROWS),
        in_specs=[pl.BlockSpec((1, n1u, DFT_ROWS, C), lambda b, j: (b, 0, j, 0)), _const_spec(f1.shape)],
        out_specs=pl.BlockSpec((DFT_KHALF, 2, DFT_ROWS, C), lambda b, j: (0, 0, j, b)),
        out_shape=jax.ShapeDtypeStruct((DFT_KHALF, 2, n2, Bd * C), BF16),
        compiler_params=_cparams(("parallel", "parallel")),
        name="dft_a",
    )(xv, f1)


def _dft_c_kernel(a_ref, h_ref, g_ref, gi_ref, asum_ref, o_ref):
    g = g_ref[0].astype(BF16)
    gi = gi_ref[0].astype(BF16)
    rows = g.shape[0]
    half = rows // 2
    C = HY_WIDTH
    xh = jnp.dot(g, h_ref[0].reshape(rows, h_ref.shape[3]), preferred_element_type=F32)
    asum = asum_ref[...]
    inv = 1.0 / (asum[:, :C] + asum[:, C:] + 1e-6)
    kr = (xh[:half, :C] + xh[:half, C:]) * inv
    ki = (xh[half:, :C] - xh[half:, C:]) * inv
    step = min(DFTC_LANES, a_ref.shape[3])
    for l0 in range(0, a_ref.shape[3], step):
        x = jnp.dot(g, a_ref[0, :, :, l0:l0 + step].reshape(rows, step), preferred_element_type=F32)
        ys = []
        for c0 in range(0, step, C):
            xr, xi = x[:half, c0:c0 + C], x[half:, c0:c0 + C]
            ys.append(jnp.concatenate([xr * kr - xi * ki, xr * ki + xi * kr], axis=0).astype(BF16))
        bk = jnp.dot(gi, jnp.concatenate(ys, axis=1), preferred_element_type=F32).astype(BF16)
        o_ref[0, :, :, l0:l0 + step] = bk.reshape(2, half, step)


def _dft_c(a, h, g, ginv, asum, nk1):
    _, _, n2, bc = a.shape
    dspec = lambda w: pl.BlockSpec((1, 2, n2, w), lambda k: (k, 0, 0, 0))
    gspec = pl.BlockSpec((1, 2 * n2, 2 * n2), lambda k: (k, 0, 0))
    return pl.pallas_call(
        _dft_c_kernel,
        grid=(nk1,),
        in_specs=[dspec(bc), dspec(h.shape[3]), gspec, gspec, _const_spec(asum.shape)],
        out_specs=dspec(bc),
        out_shape=jax.ShapeDtypeStruct((nk1, 2, n2, bc), BF16),
        compiler_params=_cparams(("parallel",)),
        name="dft_c",
    )(a, h, g, ginv, asum)


def _idft_a_kernel(b_ref, f_ref, vg_ref, x0_ref, bias_ref, o_ref):
    nk1, _, R, C = b_ref.shape
    mp = f_ref.shape[1]
    bv = b_ref[...].reshape(2 * nk1, R, C)
    bv = jnp.concatenate([bv, jnp.zeros((mp - 2 * nk1, R, C), BF16)], axis=0)
    bs = _rows2d(jnp.swapaxes(bv, 0, 1))
    f = f_ref[...]
    ys = [jnp.dot(f, bs[r * mp:(r + 1) * mp], preferred_element_type=F32).astype(BF16) for r in range(R)]
    y = _rows2d(jnp.swapaxes(jnp.stack(ys, axis=0), 0, 1)).astype(F32)
    y = y + _rows2d(vg_ref[0]).astype(F32) * bias_ref[...]
    o_ref[0] = (y * _rows2d(x0_ref[0]).astype(F32)).astype(BF16).reshape(o_ref.shape[1:])


def _idft_a(bk, finv, vgv, x0v, bias):
    B, n1u, n2, C = vgv.shape
    nk1 = bk.shape[0]
    dsp = pl.BlockSpec((1, n1u, DFT_ROWS, C), lambda b, j: (b, 0, j, 0))
    return pl.pallas_call(
        _idft_a_kernel,
        grid=(B, n2 // DFT_ROWS),
        in_specs=[pl.BlockSpec((nk1, 2, DFT_ROWS, C), lambda b, j: (0, 0, j, b)),
                  _const_spec(finv.shape), dsp, dsp, _const_spec(bias.shape)],
        out_specs=dsp,
        out_shape=jax.ShapeDtypeStruct((B, n1u, n2, C), BF16),
        compiler_params=_cparams(("parallel", "parallel")),
        name="idft_a",
    )(bk, finv, vgv, x0v, bias)


def _retention_kernel(q_ref, k_ref, v_ref, g_ref, lgf_ref, lgb_ref, s0f_ref, s0b_ref, gn_ref, o_ref,
                      inc_ref, st_ref, dm_ref, wt_ref):
    C = RET_CHUNK
    seq_len = q_ref.shape[1]
    nc = seq_len // C
    lgf = lgf_ref[0]
    lgb = lgb_ref[0]
    lane = lax.broadcasted_iota(jnp.int32, (1, LANES), 1)
    head_of_lane = lane // RET_HEAD_DIM
    bd = _block_diag_mask()

    ri = lax.broadcasted_iota(jnp.int32, (C, LANES), 0).astype(F32)
    wt_ref[0] = jnp.exp(lgf * (ri + 1.0)).astype(BF16)
    wt_ref[1] = jnp.exp(lgb * (C - ri)).astype(BF16)
    wt_ref[2] = jnp.exp(lgf * (C - 1.0 - ri)).astype(BF16)
    wt_ref[3] = jnp.exp(lgb * ri).astype(BF16)
    di = lax.broadcasted_iota(jnp.int32, (C, C), 0)
    dj = lax.broadcasted_iota(jnp.int32, (C, C), 1)
    dd = (di - dj).astype(F32)
    for h in range(HEADS_PER_BLOCK):
        lf = jnp.sum(jnp.where(lane == h * RET_HEAD_DIM, lgf, 0.0), axis=1, keepdims=True)
        lb = jnp.sum(jnp.where(lane == h * RET_HEAD_DIM, lgb, 0.0), axis=1, keepdims=True)
        dm_ref[h] = jnp.where(dd > 0, jnp.exp(lf * dd), jnp.where(dd < 0, jnp.exp(-lb * dd), 2.0)).astype(BF16)

    def chunk(c):
        return pl.ds(pl.multiple_of(c * C, C), C)

    def inc_body(c, carry):
        rows = chunk(c)
        k = k_ref[0, rows, :]
        kw = jnp.concatenate([k * wt_ref[2], k * wt_ref[3]], axis=1)
        t = _kt_v(kw, v_ref[0, rows, :])
        inc_ref[c, :LANES] = jnp.where(bd, t[:LANES], 0.0)
        inc_ref[c, LANES:] = jnp.where(bd, t[LANES:], 0.0)
        return carry

    lax.fori_loop(0, nc, inc_body, 0, unroll=True)

    gcf = jnp.exp(lgf * float(C))
    gcb = jnp.exp(lgb * float(C))

    def fscan(c, s):
        st_ref[c, :LANES] = s.astype(BF16)
        return s * gcf + inc_ref[c, :LANES]

    lax.fori_loop(0, nc, fscan, s0f_ref[0, 0])

    def bscan(i, s):
        c = nc - 1 - i
        st_ref[c, LANES:] = s.astype(BF16)
        return s * gcb + inc_ref[c, LANES:]

    lax.fori_loop(0, nc, bscan, s0b_ref[0, 0])

    gn = gn_ref[...]
    first = head_of_lane == 0
    inv_n = 1.0 / RET_HEAD_DIM

    def head_mean(a):
        tot = jnp.sum(a, axis=1, keepdims=True)
        s0 = jnp.sum(jnp.where(first, a, 0.0), axis=1, keepdims=True)
        return jnp.where(first, s0, tot - s0) * inv_n

    def out_body(c, carry):
        rows = chunk(c)
        q = q_ref[0, rows, :]
        k = k_ref[0, rows, :]
        v = v_ref[0, rows, :]
        qw = jnp.concatenate([q * wt_ref[0], q * wt_ref[1]], axis=1)
        o = jnp.dot(qw, st_ref[c], preferred_element_type=F32)
        oh = []
        for h in range(HEADS_PER_BLOCK):
            qm = jnp.where(head_of_lane == h, q, jnp.zeros_like(q))
            sc = lax.dot_general(qm, k, (((1,), (1,)), ((), ())), preferred_element_type=F32)
            oh.append(jnp.dot(sc.astype(BF16) * dm_ref[h], v, preferred_element_type=F32))
        o = o + jnp.where(first, oh[0], oh[1])
        d = o - head_mean(o)
        y = d * lax.rsqrt(head_mean(d * d) + NORM_EPS) * gn
        o_ref[0, rows, :] = (y * _silu(g_ref[0, rows, :].astype(F32))).astype(BF16)
        return carry

    lax.fori_loop(0, nc, out_body, 0, unroll=True)


def _retention(u, lgf, lgb, s0f, s0b, gn):
    B, L, _ = u.shape
    assert HEADS_PER_BLOCK == 2
    nc = L // RET_CHUNK
    col = lambda g: pl.BlockSpec((1, L, LANES), lambda b, p: (b, 0, g * RET_BLOCKS + p))
    lsp = pl.BlockSpec((1, 1, LANES), lambda b, p: (p, 0, 0))
    ssp = pl.BlockSpec((1, 1, LANES, LANES), lambda b, p: (b, p, 0, 0))
    return pl.pallas_call(
        _retention_kernel,
        grid=(B, RET_BLOCKS),
        in_specs=[col(0), col(1), col(2), col(3), lsp, lsp, ssp, ssp,
                  pl.BlockSpec((1, LANES), lambda b, p: (0, p))],
        out_specs=pl.BlockSpec((1, L, LANES), lambda b, p: (b, 0, p)),
        out_shape=jax.ShapeDtypeStruct((B, L, RET_WIDTH), BF16),
        scratch_shapes=[pltpu.VMEM((nc, 2 * LANES, LANES), F32),
                        pltpu.VMEM((nc, 2 * LANES, LANES), BF16),
                        pltpu.VMEM((HEADS_PER_BLOCK, RET_CHUNK, RET_CHUNK), BF16),
                        pltpu.VMEM((4, RET_CHUNK, LANES), BF16)],
        compiler_params=_cparams(("parallel", "parallel")),
        name="retention",
    )(u, u, u, u, lgf, lgb, s0f, s0b, gn)


def _rope_tables(seq_len):
    rows = seq_len // GRID_W
    r, col = jnp.meshgrid(jnp.arange(rows, dtype=F32), jnp.arange(GRID_W, dtype=F32), indexing="ij")
    inv = ROPE_BASE ** (-jnp.arange(ROPE_PAIRS_AXIS, dtype=F32) / ROPE_PAIRS_AXIS)
    ang = jnp.concatenate([r.reshape(-1, 1) * inv, col.reshape(-1, 1) * inv], axis=-1)
    ang = jnp.tile(jnp.repeat(ang, 2, axis=1), (1, HEADS_PER_BLOCK))
    sign = jnp.where(jnp.arange(LANES) % 2 == 0, -1.0, 1.0).astype(F32)
    return jnp.cos(ang), jnp.sin(ang) * sign


def _out_mlp_kernel(x_ref, yh_ref, yr_ref, wo1_ref, wo2_ref, w1_ref, w2_ref, g2_ref, gf_ref, mod_ref, o_ref,
                    *, ff_chunk):
    mod = mod_ref[0]
    mix = (jnp.dot(yh_ref[0], wo1_ref[...], preferred_element_type=F32)
           + jnp.dot(yr_ref[0], wo2_ref[...], preferred_element_type=F32))
    x1 = x_ref[0] + mod[0:1] * mix
    h = (_rms(x1) * g2_ref[...] * (1.0 + mod[2:3]) + mod[1:2]).astype(BF16)
    acc = jnp.zeros_like(x1)
    for j in range(w1_ref.shape[1] // ff_chunk):
        cols = slice(j * ff_chunk, (j + 1) * ff_chunk)
        a = jnp.maximum(jnp.dot(h, w1_ref[:, cols], preferred_element_type=F32), 0.0)
        acc = acc + jnp.dot((a * a).astype(BF16), w2_ref[cols, :], preferred_element_type=F32)
    x2 = x1 + mod[3:4] * acc
    o_ref[0] = _rms(x2) * gf_ref[...]


def _out_mlp(x, yh, yr, wo1, wo2, w1, w2, g2, gf, mod, tm):
    B, L, D = x.shape
    dff = w1.shape[1]
    row = lambda n: pl.BlockSpec((1, tm, n), lambda b, i: (b, i, 0))
    wsp = lambda a: pl.BlockSpec(a.shape, lambda b, i: (0, 0), pipeline_mode=pl.Buffered(1))
    return pl.pallas_call(
        functools.partial(_out_mlp_kernel, ff_chunk=1024),
        grid=(B, L // tm),
        in_specs=[row(D), row(yh.shape[2]), row(yr.shape[2]), wsp(wo1), wsp(wo2), wsp(w1), wsp(w2),
                  _const_spec((1, D)), _const_spec((1, D)),
                  pl.BlockSpec((1, 4, D), lambda b, i: (b, 0, 0))],
        out_specs=row(D),
        out_shape=jax.ShapeDtypeStruct((B, L, D), F32),
        compiler_params=_cparams(("parallel", "parallel")),
        name="out_mlp",
    )(x, yh, yr, wo1, wo2, w1, w2, g2, gf, mod)


def kernel(x, c, ctx, c_ctx, w_ada, b_ada, norm1_g, w_in, hy_conv_w, hy_conv_b, hy_f_w1, hy_f_b1, hy_f_freq1,
           hy_f_w2, hy_f_b2, hy_f_freq2, hy_f_w3, hy_bias, ret_decay_logit, ret_gn_g, w_out, norm2_g, w_mlp1,
           w_mlp2, norm_f_g):
    B, L, D = x.shape
    depth = w_ada.shape[0]
    assert depth == 1, "single-layer block"
    l = 0
    rows = 16
    cvec = jnp.zeros((rows, D), F32).at[:B].set(c).at[B].set(c_ctx)
    mod = _ada(cvec, w_ada[l], b_ada[l][None])
    mx = mod[:B].reshape(B, 6, D)
    mc = mod[B].reshape(6, D)

    lg = jax.nn.log_sigmoid(ret_decay_logit[l].astype(F32))
    lg_lanes = jnp.repeat(lg, RET_HEAD_DIM, axis=1).reshape(2, RET_BLOCKS, 1, LANES)
    lgf, lgb = lg_lanes[0], lg_lanes[1]

    w_in_b = w_in[l].astype(BF16)
    g1 = norm1_g[l][None]
    vg, x0c, u_ret = _inproj(x, g1, mx[:, 0:1], mx[:, 1:2], w_in_b, hy_conv_w[l].astype(F32),
                             hy_conv_b[l][None].astype(F32), _rope_tables(L), tm=1024)
    ones = jnp.ones((B, 1, 1), F32)
    uc = _inproj_ctx(ctx, g1, ones * mc[0], ones * mc[1], w_in_b)
    s0f, s0b = _ctx_states(uc, lgf, lgb)

    f1, finv, gmat, ginv, nk1 = _dft_tables(L)
    f1, finv = f1.astype(BF16), finv.astype(BF16)
    taps, asum = _filters(L, hy_f_w1[l], hy_f_b1[l], hy_f_freq1[l], hy_f_w2[l], hy_f_b2[l], hy_f_freq2[l],
                          hy_f_w3[l])
    tview = lambda a: a.reshape(a.shape[0], L // DFT_N2, DFT_N2, HY_WIDTH)
    bk = _dft_c(_dft_a(tview(vg), f1), _dft_a(tview(taps), f1), gmat, ginv, asum, nk1)
    y_hy = _idft_a(bk, finv, tview(vg), tview(x0c), hy_bias[l][None].astype(F32)).reshape(B, L, HY_WIDTH)

    y_ret = _retention(u_ret, lgf, lgb, s0f, s0b, ret_gn_g[l][None].astype(F32))

    w_out_b = w_out[l].astype(BF16)
    mod2 = mx[:, 2:6]
    return _out_mlp(x, y_hy, y_ret, w_out_b[:HY_WIDTH], w_out_b[HY_WIDTH:], w_mlp1[l].astype(BF16),
                    w_mlp2[l].astype(BF16), norm2_g[l][None], norm_f_g[None], mod2, tm=512)
```

```python
import functools
import math

import numpy as np
import jax
import jax.numpy as jnp
from jax import lax
from jax.experimental import pallas as pl
from jax.experimental.pallas import tpu as pltpu

F32 = jnp.float32
BF16 = jnp.bfloat16
HIGHEST = lax.Precision.HIGHEST

GRID_W = 64
HY_WIDTH = 512
RET_WIDTH = 512
RET_HEADS = 8
RET_HEAD_DIM = RET_WIDTH // RET_HEADS
HY_PROJ = 3 * HY_WIDTH
HY_BANDS = 16
HY_FILT_HID = 64
HY_DECAY_TARGET = 1e-2
HY_FAST_PCT = 0.3
HY_SLOW_PCT = 1.5
ROPE_BASE = 10000.0
ROPE_PAIRS_AXIS = RET_HEAD_DIM // 4
NORM_EPS = 1e-6

LANES = 128
HEADS_PER_BLOCK = LANES // RET_HEAD_DIM
RET_BLOCKS = RET_WIDTH // LANES
RET_CHUNK = 256
DFT_N2 = 128
DFT_ROWS = 16
DFT_KHALF = 72
DFTC_LANES = 2048
VMEM_LIMIT = 56 * 1024 * 1024


def _cparams(sem):
    return pltpu.CompilerParams(dimension_semantics=sem, vmem_limit_bytes=VMEM_LIMIT)


def _const_spec(shape):
    nd = len(shape)
    return pl.BlockSpec(shape, lambda *_: (0,) * nd)


def _silu(x):
    return x * jax.nn.sigmoid(x)


def _rms(x):
    return x * lax.rsqrt(jnp.mean(x * x, axis=-1, keepdims=True) + NORM_EPS)


def _ada_kernel(c_ref, w_ref, b_ref, o_ref):
    s = _silu(c_ref[...])
    o_ref[...] = jnp.dot(s, w_ref[...], preferred_element_type=F32, precision=HIGHEST) + b_ref[...]


def _ada(cvec, w, b):
    R, D = cvec.shape
    N = w.shape[1]
    return pl.pallas_call(
        _ada_kernel,
        grid=(N // D,),
        in_specs=[_const_spec((R, D)), pl.BlockSpec((D, D), lambda j: (0, j)),
                  pl.BlockSpec((1, D), lambda j: (0, j))],
        out_specs=pl.BlockSpec((R, D), lambda j: (0, j)),
        out_shape=jax.ShapeDtypeStruct((R, N), F32),
        compiler_params=_cparams(("arbitrary",)),
        name="ada",
    )(cvec, w, b)


INPROJ_NCHUNK = 1024
INPROJ_SUBTILES = 1


HALO = 16


def _norm_mod(x, g_ref, sh_ref, sc_ref):
    h = _rms(x) * g_ref[...]
    return (h * (1.0 + sc_ref[0]) + sh_ref[0]).astype(BF16)


def _inproj_kernel(x_ref, xp_ref, xn_ref, g_ref, sh_ref, sc_ref, w_ref, cw_ref, cb_ref, cos_ref, sin_ref,
                   vg_ref, x0_ref, ur_ref):
    i = pl.program_id(1)
    tm = x_ref.shape[1]
    ts = tm // INPROJ_SUBTILES
    ext = ts + 2 * HALO
    xa = jnp.concatenate([xp_ref[0, 0], x_ref[0], xn_ref[0, 0]], axis=0)
    row = lax.broadcasted_iota(jnp.int32, (ext, 1), 0)
    even = (lax.broadcasted_iota(jnp.int32, (1, LANES), 1) % 2) == 0
    for s in range(INPROJ_SUBTILES):
        rows = slice(s * ts, (s + 1) * ts)
        hb = _norm_mod(xa[s * ts:s * ts + ext], g_ref, sh_ref, sc_ref)
        if s == 0:
            hb = jnp.where((row < HALO) & (i == 0), jnp.zeros_like(hb), hb)
        if s == INPROJ_SUBTILES - 1:
            hb = jnp.where((row >= ts + HALO) & (i == pl.num_programs(1) - 1), jnp.zeros_like(hb), hb)
        conv = []
        for c0 in range(0, HY_PROJ, HY_WIDTH):
            cols = slice(c0, c0 + HY_WIDTH)
            r = jnp.dot(hb, w_ref[:, cols], preferred_element_type=F32)
            cw = cw_ref[:, cols]
            y = pltpu.roll(r, 1, 0) * cw[0:1] + r * cw[1:2] + pltpu.roll(r, ext - 1, 0) * cw[2:3] + cb_ref[:, cols]
            conv.append(y[HALO:HALO + ts])
        x0_ref[0, rows] = conv[0].astype(BF16)
        vg_ref[0, rows] = (conv[2] * conv[1]).astype(BF16)

        hm = hb[HALO:HALO + ts]
        for c0 in range(HY_PROJ, w_ref.shape[1], INPROJ_NCHUNK):
            r = jnp.dot(hm, w_ref[:, c0:c0 + INPROJ_NCHUNK], preferred_element_type=F32)
            parts = []
            for b0 in range(0, INPROJ_NCHUNK, LANES):
                t = r[:, b0:b0 + LANES]
                if c0 + b0 < HY_PROJ + 2 * RET_WIDTH:
                    sw = jnp.where(even, pltpu.roll(t, LANES - 1, 1), pltpu.roll(t, 1, 1))
                    t = t * cos_ref[rows] + sw * sin_ref[rows]
                    if c0 + b0 >= HY_PROJ + RET_WIDTH:
                        t = t * (RET_HEAD_DIM ** -0.5)
                parts.append(t)
            ur_ref[0, rows, c0 - HY_PROJ:c0 - HY_PROJ + INPROJ_NCHUNK] = jnp.concatenate(parts, axis=1).astype(BF16)


def _inproj(x, g, shift, scale, w, conv_w, conv_b, rope, tm):
    B, L, D = x.shape
    N = w.shape[1]
    nh = tm // HALO
    nrow = L // HALO
    xh = x.reshape(B, nrow, HALO, D)
    vec = pl.BlockSpec((1, 1, D), lambda b, i: (b, 0, 0))
    row = lambda n: pl.BlockSpec((1, tm, n), lambda b, i: (b, i, 0))
    tab = pl.BlockSpec((tm, LANES), lambda b, i: (i, 0))
    return pl.pallas_call(
        _inproj_kernel,
        grid=(B, L // tm),
        in_specs=[row(D),
                  pl.BlockSpec((1, 1, HALO, D), lambda b, i: (b, jnp.maximum(i * nh - 1, 0), 0, 0)),
                  pl.BlockSpec((1, 1, HALO, D), lambda b, i: (b, jnp.minimum((i + 1) * nh, nrow - 1), 0, 0)),
                  _const_spec((1, D)), vec, vec,
                  pl.BlockSpec((D, N), lambda b, i: (0, 0), pipeline_mode=pl.Buffered(1)),
                  _const_spec(conv_w.shape), _const_spec(conv_b.shape), tab, tab],
        out_specs=[row(HY_WIDTH), row(HY_WIDTH), row(N - HY_PROJ)],
        out_shape=[jax.ShapeDtypeStruct((B, L, HY_WIDTH), BF16), jax.ShapeDtypeStruct((B, L, HY_WIDTH), BF16),
                   jax.ShapeDtypeStruct((B, L, N - HY_PROJ), BF16)],
        compiler_params=_cparams(("parallel", "parallel")),
        name="inproj",
    )(x, xh, xh, g, shift, scale, w, conv_w, conv_b, *rope)


def _block_diag_mask():
    r = lax.broadcasted_iota(jnp.int32, (LANES, LANES), 0) // RET_HEAD_DIM
    c = lax.broadcasted_iota(jnp.int32, (LANES, LANES), 1) // RET_HEAD_DIM
    return r == c


def _kt_v(k, v):
    return lax.dot_general(k, v, (((0,), (0,)), ((), ())), preferred_element_type=F32)


def _ctx_state_kernel(x_ref, g_ref, sh_ref, sc_ref, wk_ref, wv_ref, lgf_ref, lgb_ref, sf_ref, sb_ref):
    hb = _norm_mod(x_ref[0], g_ref, sh_ref, sc_ref)
    k = jnp.dot(hb, wk_ref[...], preferred_element_type=F32) * (RET_HEAD_DIM ** -0.5)
    v = jnp.dot(hb, wv_ref[...], preferred_element_type=F32).astype(BF16)
    lc = k.shape[0]
    m = lax.broadcasted_iota(jnp.int32, (lc, LANES), 0).astype(F32)
    bd = _block_diag_mask()
    for p in range(RET_BLOCKS):
        lanes = slice(p * LANES, (p + 1) * LANES)
        wf = jnp.exp(lgf_ref[p] * (lc - 1.0 - m))
        wb = jnp.exp(lgb_ref[p] * m)
        sf_ref[0, p] = jnp.where(bd, _kt_v((k[:, lanes] * wf).astype(BF16), v[:, lanes]), 0.0)
        sb_ref[0, p] = jnp.where(bd, _kt_v((k[:, lanes] * wb).astype(BF16), v[:, lanes]), 0.0)


def _ctx_states(ctx, g, shift, scale, wk, wv, lgf, lgb):
    B, lc, D = ctx.shape
    st = jax.ShapeDtypeStruct((B, RET_BLOCKS, LANES, LANES), F32)
    sspec = pl.BlockSpec((1, RET_BLOCKS, LANES, LANES), lambda b: (b, 0, 0, 0))
    return pl.pallas_call(
        _ctx_state_kernel,
        grid=(B,),
        in_specs=[pl.BlockSpec((1, lc, D), lambda b: (b, 0, 0)), _const_spec((1, D)),
                  _const_spec(shift.shape), _const_spec(scale.shape), _const_spec(wk.shape), _const_spec(wv.shape),
                  _const_spec(lgf.shape), _const_spec(lgb.shape)],
        out_specs=[sspec, sspec],
        out_shape=[st, st],
        compiler_params=_cparams(("parallel",)),
        name="ctx_state",
    )(ctx, g, shift, scale, wk, wv, lgf, lgb)


FILT_ROWS = 1024
FILT_GROUPS = LANES // HY_BANDS
FILT_HALVES = LANES // HY_FILT_HID


def _filt_kernel(bands_ref, w1t_ref, w1c_ref, w1s_ref, b1_ref, fr1_ref, w2_ref, b2_ref, fr2_ref,
                 w3_ref, dl_ref, h_ref, asum_ref, *, seq_len):
    i = pl.program_id(0)
    tl = FILT_ROWS
    gr = tl // FILT_GROUPS
    hr = tl // FILT_HALVES
    lane = lax.broadcasted_iota(jnp.int32, (1, LANES), 1)
    base = i * tl
    dot = functools.partial(jnp.dot, preferred_element_type=F32, precision=HIGHEST)

    n8 = (lax.broadcasted_iota(jnp.int32, (gr, 1), 0) + (lane // HY_BANDS) * gr + base).astype(F32)
    arg = bands_ref[...] * ((2.0 * math.pi / seq_len) * n8)
    cz, sz = jnp.cos(arg), jnp.sin(arg)
    nh = (lax.broadcasted_iota(jnp.int32, (hr, 1), 0) + (lane // HY_FILT_HID) * hr + base).astype(F32)
    blocks = [dot(cz, w1c_ref[r]) - dot(sz, w1s_ref[r]) for r in range(hr // gr)]
    pre = jnp.concatenate(blocks, axis=0) + (nh / (seq_len - 1.0)) * w1t_ref[...] + b1_ref[...]
    h = jnp.sin(fr1_ref[...] * pre)
    h = jnp.sin(fr2_ref[...] * (dot(h, w2_ref[...]) + b2_ref[...])).astype(BF16)

    @pl.when(i == 0)
    def _():
        asum_ref[...] = jnp.zeros_like(asum_ref)

    for p in range(FILT_HALVES):
        n = (lax.broadcasted_iota(jnp.int32, (hr, 1), 0) + p * hr + base).astype(F32)
        t = n / (seq_len - 1.0)
        taps = jnp.dot(h, w3_ref[p], preferred_element_type=F32) * jnp.exp(-t * dl_ref[...])
        asum_ref[...] += jnp.sum(jnp.abs(taps), axis=0, keepdims=True)
        rows = slice(p * hr, (p + 1) * hr)
        h_ref[0, rows] = taps[:, :HY_WIDTH].astype(BF16)
        h_ref[1, rows] = jnp.where(n == 0.0, 0.0, taps[:, HY_WIDTH:]).astype(BF16)


def _filters(seq_len, w1, b1, fr1, w2, b2, fr2, w3):
    hid = HY_FILT_HID
    f32 = lambda a: a.astype(F32)
    bands = np.tile(np.linspace(1e-4, HY_BANDS - 1, HY_BANDS, dtype=np.float32), FILT_GROUPS)[None]
    deltas = np.abs(np.linspace(math.log(HY_DECAY_TARGET) / HY_SLOW_PCT,
                                math.log(HY_DECAY_TARGET) / HY_FAST_PCT, HY_WIDTH, dtype=np.float32))
    dl = np.tile(deltas, 2)[None, :]
    halves = lambda v: jnp.tile(f32(v)[None], (1, FILT_HALVES))

    def place_l1(wb):
        nb = FILT_GROUPS // FILT_HALVES
        out = jnp.zeros((nb, LANES, LANES), F32)
        for r in range(nb):
            for p in range(FILT_HALVES):
                s = p * nb + r
                out = out.at[r, s * HY_BANDS:(s + 1) * HY_BANDS, p * hid:(p + 1) * hid].set(f32(wb))
        return out

    w2d = jnp.zeros((LANES, LANES), F32)
    w3p = jnp.zeros((FILT_HALVES, LANES, 2 * HY_WIDTH), F32)
    for p in range(FILT_HALVES):
        w2d = w2d.at[p * hid:(p + 1) * hid, p * hid:(p + 1) * hid].set(f32(w2))
        w3p = w3p.at[p, p * hid:(p + 1) * hid].set(f32(w3))
    args = (jnp.asarray(bands), halves(w1[0]), place_l1(w1[1:1 + HY_BANDS]), place_l1(w1[1 + HY_BANDS:]),
            halves(b1), halves(fr1), w2d, halves(b2), halves(fr2), w3p.astype(BF16), jnp.asarray(dl))
    return pl.pallas_call(
        functools.partial(_filt_kernel, seq_len=seq_len),
        grid=(seq_len // FILT_ROWS,),
        in_specs=[_const_spec(a.shape) for a in args],
        out_specs=[pl.BlockSpec((2, FILT_ROWS, HY_WIDTH), lambda i: (0, i, 0)),
                   _const_spec((1, 2 * HY_WIDTH))],
        out_shape=[jax.ShapeDtypeStruct((2, seq_len, HY_WIDTH), BF16),
                   jax.ShapeDtypeStruct((1, 2 * HY_WIDTH), F32)],
        compiler_params=_cparams(("arbitrary",)),
        name="filt",
    )(*args)


def _dft_tables(seq_len):
    n = 2 * seq_len
    n1_full = n // DFT_N2
    n1_used = n1_full // 2
    nk1 = n1_full // 2 + 1
    k1 = np.arange(nk1)[:, None]
    n1 = np.arange(n1_used)[None, :]
    th = 2.0 * np.pi * ((k1 * n1) % n1_full) / n1_full
    ck = np.full((nk1, 1), 2.0)
    ck[0, 0] = 1.0
    ck[-1, 0] = 1.0
    assert nk1 <= DFT_KHALF
    f1 = np.zeros((2 * DFT_KHALF, n1_used))
    f1[:nk1] = np.cos(th)
    f1[DFT_KHALF:DFT_KHALF + nk1] = -np.sin(th)
    finv = np.zeros((n1_used, 2 * DFT_KHALF))
    finv[:, 0:2 * nk1:2] = (ck * np.cos(th) / n).T
    finv[:, 1:2 * nk1:2] = (-ck * np.sin(th) / n).T
    k1 = np.arange(nk1)[:, None, None]
    k2 = np.arange(DFT_N2)[None, :, None]
    n2 = np.arange(DFT_N2)[None, None, :]
    ph = 2.0 * np.pi * ((n1_full * n2 * k2 + n2 * k1) % n) / n
    gr, gi = np.cos(ph), -np.sin(ph)
    g = np.concatenate([np.concatenate([gr, -gi], 2), np.concatenate([gi, gr], 2)], 1)
    grt, git = np.swapaxes(gr, 1, 2), -np.swapaxes(gi, 1, 2)
    ginv = np.concatenate([np.concatenate([grt, -git], 2), np.concatenate([git, grt], 2)], 1)
    f = lambda a: jnp.asarray(a.astype(np.float32))
    return f(f1), f(finv), f(g), f(ginv), nk1


def _rows2d(a):
    return a.reshape(a.shape[0] * a.shape[1], a.shape[2])


def _dft_a_kernel(x_ref, f_ref, o_ref):
    n1u = x_ref.shape[1]
    xs = _rows2d(jnp.swapaxes(x_ref[0], 0, 1))
    f = f_ref[...]
    res = [jnp.dot(f, xs[r * n1u:(r + 1) * n1u], preferred_element_type=F32).astype(BF16)
           for r in range(DFT_ROWS)]
    a = jnp.swapaxes(jnp.stack(res, axis=0), 0, 1)
    o_ref[:, 0] = a[:DFT_KHALF]
    o_ref[:, 1] = a[DFT_KHALF:]


def _dft_a(xv, f1):
    Bd, n1u, n2, C = xv.shape
    return pl.pallas_call(
        _dft_a_kernel,
        grid=(Bd, n2 // DFT_ROWS),
        in_specs=[pl.BlockSpec((1, n1u, DFT_ROWS, C), lambda b, j: (b, 0, j, 0)), _const_spec(f1.shape)],
        out_specs=pl.BlockSpec((DFT_KHALF, 2, DFT_ROWS, C), lambda b, j: (0, 0, j, b)),
        out_shape=jax.ShapeDtypeStruct((DFT_KHALF, 2, n2, Bd * C), BF16),
        compiler_params=_cparams(("parallel", "parallel")),
        name="dft_a",
    )(xv, f1)


def _dft_c_kernel(a_ref, h_ref, g_ref, gi_ref, asum_ref, o_ref):
    g = g_ref[0].astype(BF16)
    gi = gi_ref[0].astype(BF16)
    rows = g.shape[0]
    half = rows // 2
    C = HY_WIDTH
    xh = jnp.dot(g, h_ref[0].reshape(rows, h_ref.shape[3]), preferred_element_type=F32)
    asum = asum_ref[...]
    inv = 1.0 / (asum[:, :C] + asum[:, C:] + 1e-6)
    kr = (xh[:half, :C] + xh[:half, C:]) * inv
    ki = (xh[half:, :C] - xh[half:, C:]) * inv
    step = min(DFTC_LANES, a_ref.shape[3])
    for l0 in range(0, a_ref.shape[3], step):
        x = jnp.dot(g, a_ref[0, :, :, l0:l0 + step].reshape(rows, step), preferred_element_type=F32)
        ys = []
        for c0 in range(0, step, C):
            xr, xi = x[:half, c0:c0 + C], x[half:, c0:c0 + C]
            ys.append(jnp.concatenate([xr * kr - xi * ki, xr * ki + xi * kr], axis=0).astype(BF16))
        bk = jnp.dot(gi, jnp.concatenate(ys, axis=1), preferred_element_type=F32).astype(BF16)
        o_ref[0, :, :, l0:l0 + step] = bk.reshape(2, half, step)


def _dft_c(a, h, g, ginv, asum, nk1):
    _, _, n2, bc = a.shape
    dspec = lambda w: pl.BlockSpec((1, 2, n2, w), lambda k: (k, 0, 0, 0))
    gspec = pl.BlockSpec((1, 2 * n2, 2 * n2), lambda k: (k, 0, 0))
    return pl.pallas_call(
        _dft_c_kernel,
        grid=(nk1,),
        in_specs=[dspec(bc), dspec(h.shape[3]), gspec, gspec, _const_spec(asum.shape)],
        out_specs=dspec(bc),
        out_shape=jax.ShapeDtypeStruct((nk1, 2, n2, bc), BF16),
        compiler_params=_cparams(("parallel",)),
        name="dft_c",
    )(a, h, g, ginv, asum)


def _idft_a_kernel(b_ref, f_ref, vg_ref, x0_ref, bias_ref, o_ref):
    nk1, _, R, C = b_ref.shape
    mp = f_ref.shape[1]
    bv = b_ref[...].reshape(2 * nk1, R, C)
    bv = jnp.concatenate([bv, jnp.zeros((mp - 2 * nk1, R, C), BF16)], axis=0)
    bs = _rows2d(jnp.swapaxes(bv, 0, 1))
    f = f_ref[...]
    ys = [jnp.dot(f, bs[r * mp:(r + 1) * mp], preferred_element_type=F32).astype(BF16) for r in range(R)]
    y = _rows2d(jnp.swapaxes(jnp.stack(ys, axis=0), 0, 1)).astype(F32)
    y = y + _rows2d(vg_ref[0]).astype(F32) * bias_ref[...]
    o_ref[0] = (y * _rows2d(x0_ref[0]).astype(F32)).astype(BF16).reshape(o_ref.shape[1:])


def _idft_a(bk, finv, vgv, x0v, bias):
    B, n1u, n2, C = vgv.shape
    nk1 = bk.shape[0]
    dsp = pl.BlockSpec((1, n1u, DFT_ROWS, C), lambda b, j: (b, 0, j, 0))
    return pl.pallas_call(
        _idft_a_kernel,
        grid=(B, n2 // DFT_ROWS),
        in_specs=[pl.BlockSpec((nk1, 2, DFT_ROWS, C), lambda b, j: (0, 0, j, b)),
                  _const_spec(finv.shape), dsp, dsp, _const_spec(bias.shape)],
        out_specs=dsp,
        out_shape=jax.ShapeDtypeStruct((B, n1u, n2, C), BF16),
        compiler_params=_cparams(("parallel", "parallel")),
        name="idft_a",
    )(bk, finv, vgv, x0v, bias)


def _retention_kernel(q_ref, k_ref, v_ref, g_ref, lgf_ref, lgb_ref, s0f_ref, s0b_ref, gn_ref, o_ref,
                      inc_ref, st_ref, dm_ref, wt_ref):
    C = RET_CHUNK
    seq_len = q_ref.shape[1]
    nc = seq_len // C
    lgf = lgf_ref[0]
    lgb = lgb_ref[0]
    lane = lax.broadcasted_iota(jnp.int32, (1, LANES), 1)
    head_of_lane = lane // RET_HEAD_DIM
    bd = _block_diag_mask()

    ri = lax.broadcasted_iota(jnp.int32, (C, LANES), 0).astype(F32)
    wt_ref[0] = jnp.exp(lgf * (ri + 1.0)).astype(BF16)
    wt_ref[1] = jnp.exp(lgb * (C - ri)).astype(BF16)
    wt_ref[2] = jnp.exp(lgf * (C - 1.0 - ri)).astype(BF16)
    wt_ref[3] = jnp.exp(lgb * ri).astype(BF16)
    di = lax.broadcasted_iota(jnp.int32, (C, C), 0)
    dj = lax.broadcasted_iota(jnp.int32, (C, C), 1)
    dd = (di - dj).astype(F32)
    for h in range(HEADS_PER_BLOCK):
        lf = jnp.sum(jnp.where(lane == h * RET_HEAD_DIM, lgf, 0.0), axis=1, keepdims=True)
        lb = jnp.sum(jnp.where(lane == h * RET_HEAD_DIM, lgb, 0.0), axis=1, keepdims=True)
        dm_ref[h] = jnp.where(dd > 0, jnp.exp(lf * dd), jnp.where(dd < 0, jnp.exp(-lb * dd), 2.0)).astype(BF16)

    def chunk(c):
        return pl.ds(pl.multiple_of(c * C, C), C)

    def inc_body(c, carry):
        rows = chunk(c)
        k = k_ref[0, rows, :]
        kw = jnp.concatenate([k * wt_ref[2], k * wt_ref[3]], axis=1)
        t = _kt_v(kw, v_ref[0, rows, :])
        inc_ref[c, :LANES] = jnp.where(bd, t[:LANES], 0.0)
        inc_ref[c, LANES:] = jnp.where(bd, t[LANES:], 0.0)
        return carry

    lax.fori_loop(0, nc, inc_body, 0, unroll=True)

    gcf = jnp.exp(lgf * float(C))
    gcb = jnp.exp(lgb * float(C))

    def fscan(c, s):
        st_ref[c, :LANES] = s.astype(BF16)
        return s * gcf + inc_ref[c, :LANES]

    lax.fori_loop(0, nc, fscan, s0f_ref[0, 0])

    def bscan(i, s):
        c = nc - 1 - i
        st_ref[c, LANES:] = s.astype(BF16)
        return s * gcb + inc_ref[c, LANES:]

    lax.fori_loop(0, nc, bscan, s0b_ref[0, 0])

    gn = gn_ref[...]
    first = head_of_lane == 0
    inv_n = 1.0 / RET_HEAD_DIM

    def head_mean(a):
        tot = jnp.sum(a, axis=1, keepdims=True)
        s0 = jnp.sum(jnp.where(first, a, 0.0), axis=1, keepdims=True)
        return jnp.where(first, s0, tot - s0) * inv_n

    def out_body(c, carry):
        rows = chunk(c)
        q = q_ref[0, rows, :]
        k = k_ref[0, rows, :]
        v = v_ref[0, rows, :]
        qw = jnp.concatenate([q * wt_ref[0], q * wt_ref[1]], axis=1)
        o = jnp.dot(qw, st_ref[c], preferred_element_type=F32)
        oh = []
        for h in range(HEADS_PER_BLOCK):
            qm = jnp.where(head_of_lane == h, q, jnp.zeros_like(q))
            sc = lax.dot_general(qm, k, (((1,), (1,)), ((), ())), preferred_element_type=F32)
            oh.append(jnp.dot(sc.astype(BF16) * dm_ref[h], v, preferred_element_type=F32))
        o = o + jnp.where(first, oh[0], oh[1])
        d = o - head_mean(o)
        y = d * lax.rsqrt(head_mean(d * d) + NORM_EPS) * gn
        o_ref[0, rows, :] = (y * _silu(g_ref[0, rows, :].astype(F32))).astype(BF16)
        return carry

    lax.fori_loop(0, nc, out_body, 0, unroll=True)


def _retention(u, lgf, lgb, s0f, s0b, gn):
    B, L, _ = u.shape
    assert HEADS_PER_BLOCK == 2
    nc = L // RET_CHUNK
    col = lambda g: pl.BlockSpec((1, L, LANES), lambda b, p: (b, 0, g * RET_BLOCKS + p))
    lsp = pl.BlockSpec((1, 1, LANES), lambda b, p: (p, 0, 0))
    ssp = pl.BlockSpec((1, 1, LANES, LANES), lambda b, p: (b, p, 0, 0))
    return pl.pallas_call(
        _retention_kernel,
        grid=(B, RET_BLOCKS),
        in_specs=[col(0), col(1), col(2), col(3), lsp, lsp, ssp, ssp,
                  pl.BlockSpec((1, LANES), lambda b, p: (0, p))],
        out_specs=pl.BlockSpec((1, L, LANES), lambda b, p: (b, 0, p)),
        out_shape=jax.ShapeDtypeStruct((B, L, RET_WIDTH), BF16),
        scratch_shapes=[pltpu.VMEM((nc, 2 * LANES, LANES), F32),
                        pltpu.VMEM((nc, 2 * LANES, LANES), BF16),
                        pltpu.VMEM((HEADS_PER_BLOCK, RET_CHUNK, RET_CHUNK), BF16),
                        pltpu.VMEM((4, RET_CHUNK, LANES), BF16)],
        compiler_params=_cparams(("parallel", "parallel")),
        name="retention",
    )(u, u, u, u, lgf, lgb, s0f, s0b, gn)


def _rope_tables(seq_len):
    rows = seq_len // GRID_W
    r, col = jnp.meshgrid(jnp.arange(rows, dtype=F32), jnp.arange(GRID_W, dtype=F32), indexing="ij")
    inv = ROPE_BASE ** (-jnp.arange(ROPE_PAIRS_AXIS, dtype=F32) / ROPE_PAIRS_AXIS)
    ang = jnp.concatenate([r.reshape(-1, 1) * inv, col.reshape(-1, 1) * inv], axis=-1)
    ang = jnp.tile(jnp.repeat(ang, 2, axis=1), (1, HEADS_PER_BLOCK))
    sign = jnp.where(jnp.arange(LANES) % 2 == 0, -1.0, 1.0).astype(F32)
    return jnp.cos(ang), jnp.sin(ang) * sign


def _out_mlp_kernel(x_ref, yh_ref, yr_ref, wo1_ref, wo2_ref, w1_ref, w2_ref, g2_ref, gf_ref, mod_ref, o_ref,
                    *, ff_chunk):
    mod = mod_ref[0]
    mix = (jnp.dot(yh_ref[0], wo1_ref[...], preferred_element_type=F32)
           + jnp.dot(yr_ref[0], wo2_ref[...], preferred_element_type=F32))
    x1 = x_ref[0] + mod[0:1] * mix
    h = (_rms(x1) * g2_ref[...] * (1.0 + mod[2:3]) + mod[1:2]).astype(BF16)
    acc = jnp.zeros_like(x1)
    for j in range(w1_ref.shape[1] // ff_chunk):
        cols = slice(j * ff_chunk, (j + 1) * ff_chunk)
        a = jnp.maximum(jnp.dot(h, w1_ref[:, cols], preferred_element_type=F32), 0.0)
        acc = acc + jnp.dot((a * a).astype(BF16), w2_ref[cols, :], preferred_element_type=F32)
    x2 = x1 + mod[3:4] * acc
    o_ref[0] = _rms(x2) * gf_ref[...]


def _out_mlp(x, yh, yr, wo1, wo2, w1, w2, g2, gf, mod, tm):
    B, L, D = x.shape
    dff = w1.shape[1]
    row = lambda n: pl.BlockSpec((1, tm, n), lambda b, i: (b, i, 0))
    wsp = lambda a: pl.BlockSpec(a.shape, lambda b, i: (0, 0), pipeline_mode=pl.Buffered(1))
    return pl.pallas_call(
        functools.partial(_out_mlp_kernel, ff_chunk=1024),
        grid=(B, L // tm),
        in_specs=[row(D), row(yh.shape[2]), row(yr.shape[2]), wsp(wo1), wsp(wo2), wsp(w1), wsp(w2),
                  _const_spec((1, D)), _const_spec((1, D)),
                  pl.BlockSpec((1, 4, D), lambda b, i: (b, 0, 0))],
        out_specs=row(D),
        out_shape=jax.ShapeDtypeStruct((B, L, D), F32),
        compiler_params=_cparams(("parallel", "parallel")),
        name="out_mlp",
    )(x, yh, yr, wo1, wo2, w1, w2, g2, gf, mod)


def kernel(x, c, ctx, c_ctx, w_ada, b_ada, norm1_g, w_in, hy_conv_w, hy_conv_b, hy_f_w1, hy_f_b1, hy_f_freq1,
           hy_f_w2, hy_f_b2, hy_f_freq2, hy_f_w3, hy_bias, ret_decay_logit, ret_gn_g, w_out, norm2_g, w_mlp1,
           w_mlp2, norm_f_g):
    B, L, D = x.shape
    depth = w_ada.shape[0]
    assert depth == 1, "single-layer block"
    l = 0
    rows = 16
    cvec = jnp.zeros((rows, D), F32).at[:B].set(c).at[B].set(c_ctx)
    mod = _ada(cvec, w_ada[l], b_ada[l][None])
    mx = mod[:B].reshape(B, 6, D)
    mc = mod[B].reshape(6, D)

    lg = jax.nn.log_sigmoid(ret_decay_logit[l].astype(F32))
    lg_lanes = jnp.repeat(lg, RET_HEAD_DIM, axis=1).reshape(2, RET_BLOCKS, 1, LANES)
    lgf, lgb = lg_lanes[0], lg_lanes[1]

    w_in_b = w_in[l].astype(BF16)
    g1 = norm1_g[l][None]
    vg, x0c, u_ret = _inproj(x, g1, mx[:, 0:1], mx[:, 1:2], w_in_b, hy_conv_w[l].astype(F32),
                             hy_conv_b[l][None].astype(F32), _rope_tables(L), tm=1024)
    kc0 = HY_PROJ + RET_WIDTH
    s0f, s0b = _ctx_states(ctx, g1, mc[0][None, None], mc[1][None, None], w_in_b[:, kc0:kc0 + RET_WIDTH],
                           w_in_b[:, kc0 + RET_WIDTH:kc0 + 2 * RET_WIDTH], lgf, lgb)

    f1, finv, gmat, ginv, nk1 = _dft_tables(L)
    f1, finv = f1.astype(BF16), finv.astype(BF16)
    taps, asum = _filters(L, hy_f_w1[l], hy_f_b1[l], hy_f_freq1[l], hy_f_w2[l], hy_f_b2[l], hy_f_freq2[l],
                          hy_f_w3[l])
    tview = lambda a: a.reshape(a.shape[0], L // DFT_N2, DFT_N2, HY_WIDTH)
    bk = _dft_c(_dft_a(tview(vg), f1), _dft_a(tview(taps), f1), gmat, ginv, asum, nk1)
    y_hy = _idft_a(bk, finv, tview(vg), tview(x0c), hy_bias[l][None].astype(F32)).reshape(B, L, HY_WIDTH)

    y_ret = _retention(u_ret, lgf, lgb, s0f, s0b, ret_gn_g[l][None].astype(F32))

    w_out_b = w_out[l].astype(BF16)
    mod2 = mx[:, 2:6]
    return _out_mlp(x, y_hy, y_ret, w_out_b[:HY_WIDTH], w_out_b[HY_WIDTH:], w_mlp1[l].astype(BF16),
                    w_mlp2[l].astype(BF16), norm2_g[l][None], norm_f_g[None], mod2, tm=1024)
```

```python
import functools
import math

import numpy as np
import jax
import jax.numpy as jnp
from jax import lax
from jax.experimental import pallas as pl
from jax.experimental.pallas import tpu as pltpu

F32 = jnp.float32
BF16 = jnp.bfloat16
HIGHEST = lax.Precision.HIGHEST

GRID_W = 64
HY_WIDTH = 512
RET_WIDTH = 512
RET_HEADS = 8
RET_HEAD_DIM = RET_WIDTH // RET_HEADS
HY_PROJ = 3 * HY_WIDTH
HY_BANDS = 16
HY_FILT_HID = 64
HY_DECAY_TARGET = 1e-2
HY_FAST_PCT = 0.3
HY_SLOW_PCT = 1.5
ROPE_BASE = 10000.0
ROPE_PAIRS_AXIS = RET_HEAD_DIM // 4
NORM_EPS = 1e-6

LANES = 128
HEADS_PER_BLOCK = LANES // RET_HEAD_DIM
RET_BLOCKS = RET_WIDTH // LANES
RET_CHUNK = 256
DFT_N2 = 128
DFT_ROWS = 16
DFT_KHALF = 72
DFTC_LANES = 2048
VMEM_LIMIT = 56 * 1024 * 1024


def _cparams(sem):
    return pltpu.CompilerParams(dimension_semantics=sem, vmem_limit_bytes=VMEM_LIMIT)


def _const_spec(shape):
    nd = len(shape)
    return pl.BlockSpec(shape, lambda *_: (0,) * nd)


def _silu(x):
    return x * jax.nn.sigmoid(x)


def _rms(x):
    return x * lax.rsqrt(jnp.mean(x * x, axis=-1, keepdims=True) + NORM_EPS)


def _ada_kernel(c_ref, w_ref, b_ref, o_ref):
    s = _silu(c_ref[...])
    o_ref[...] = jnp.dot(s, w_ref[...], preferred_element_type=F32, precision=HIGHEST) + b_ref[...]


def _ada(cvec, w, b):
    R, D = cvec.shape
    N = w.shape[1]
    return pl.pallas_call(
        _ada_kernel,
        grid=(N // D,),
        in_specs=[_const_spec((R, D)), pl.BlockSpec((D, D), lambda j: (0, j)),
                  pl.BlockSpec((1, D), lambda j: (0, j))],
        out_specs=pl.BlockSpec((R, D), lambda j: (0, j)),
        out_shape=jax.ShapeDtypeStruct((R, N), F32),
        compiler_params=_cparams(("arbitrary",)),
        name="ada",
    )(cvec, w, b)


INPROJ_NCHUNK = 1024
INPROJ_SUBTILES = 1


HALO = 16


def _norm_mod(x, g_ref, sh_ref, sc_ref):
    h = _rms(x) * g_ref[...]
    return (h * (1.0 + sc_ref[0]) + sh_ref[0]).astype(BF16)


def _inproj_kernel(x_ref, xp_ref, xn_ref, g_ref, sh_ref, sc_ref, w_ref, cw_ref, cb_ref, cos_ref, sin_ref,
                   vg_ref, x0_ref, ur_ref):
    i = pl.program_id(1)
    tm = x_ref.shape[1]
    ts = tm // INPROJ_SUBTILES
    ext = ts + 2 * HALO
    xa = jnp.concatenate([xp_ref[0, 0], x_ref[0], xn_ref[0, 0]], axis=0)
    row = lax.broadcasted_iota(jnp.int32, (ext, 1), 0)
    even = (lax.broadcasted_iota(jnp.int32, (1, LANES), 1) % 2) == 0
    for s in range(INPROJ_SUBTILES):
        rows = slice(s * ts, (s + 1) * ts)
        hb = _norm_mod(xa[s * ts:s * ts + ext], g_ref, sh_ref, sc_ref)
        if s == 0:
            hb = jnp.where((row < HALO) & (i == 0), jnp.zeros_like(hb), hb)
        if s == INPROJ_SUBTILES - 1:
            hb = jnp.where((row >= ts + HALO) & (i == pl.num_programs(1) - 1), jnp.zeros_like(hb), hb)
        conv = []
        for c0 in range(0, HY_PROJ, HY_WIDTH):
            cols = slice(c0, c0 + HY_WIDTH)
            r = jnp.dot(hb, w_ref[:, cols], preferred_element_type=F32)
            cw = cw_ref[:, cols]
            y = pltpu.roll(r, 1, 0) * cw[0:1] + r * cw[1:2] + pltpu.roll(r, ext - 1, 0) * cw[2:3] + cb_ref[:, cols]
            conv.append(y[HALO:HALO + ts])
        x0_ref[0, rows] = conv[0].astype(BF16)
        vg_ref[0, rows] = (conv[2] * conv[1]).astype(BF16)

        hm = hb[HALO:HALO + ts]
        for c0 in range(HY_PROJ, w_ref.shape[1], INPROJ_NCHUNK):
            r = jnp.dot(hm, w_ref[:, c0:c0 + INPROJ_NCHUNK], preferred_element_type=F32)
            parts = []
            for b0 in range(0, INPROJ_NCHUNK, LANES):
                t = r[:, b0:b0 + LANES]
                if c0 + b0 < HY_PROJ + 2 * RET_WIDTH:
                    sw = jnp.where(even, pltpu.roll(t, LANES - 1, 1), pltpu.roll(t, 1, 1))
                    t = t * cos_ref[rows] + sw * sin_ref[rows]
                    if c0 + b0 >= HY_PROJ + RET_WIDTH:
                        t = t * (RET_HEAD_DIM ** -0.5)
                parts.append(t)
            ur_ref[0, rows, c0 - HY_PROJ:c0 - HY_PROJ + INPROJ_NCHUNK] = jnp.concatenate(parts, axis=1).astype(BF16)


def _inproj(x, g, shift, scale, w, conv_w, conv_b, rope, tm):
    B, L, D = x.shape
    N = w.shape[1]
    nh = tm // HALO
    nrow = L // HALO
    xh = x.reshape(B, nrow, HALO, D)
    vec = pl.BlockSpec((1, 1, D), lambda b, i: (b, 0, 0))
    row = lambda n: pl.BlockSpec((1, tm, n), lambda b, i: (b, i, 0))
    tab = pl.BlockSpec((tm, LANES), lambda b, i: (i, 0))
    return pl.pallas_call(
        _inproj_kernel,
        grid=(B, L // tm),
        in_specs=[row(D),
                  pl.BlockSpec((1, 1, HALO, D), lambda b, i: (b, jnp.maximum(i * nh - 1, 0), 0, 0)),
                  pl.BlockSpec((1, 1, HALO, D), lambda b, i: (b, jnp.minimum((i + 1) * nh, nrow - 1), 0, 0)),
                  _const_spec((1, D)), vec, vec,
                  pl.BlockSpec((D, N), lambda b, i: (0, 0), pipeline_mode=pl.Buffered(1)),
                  _const_spec(conv_w.shape), _const_spec(conv_b.shape), tab, tab],
        out_specs=[row(HY_WIDTH), row(HY_WIDTH), row(N - HY_PROJ)],
        out_shape=[jax.ShapeDtypeStruct((B, L, HY_WIDTH), BF16), jax.ShapeDtypeStruct((B, L, HY_WIDTH), BF16),
                   jax.ShapeDtypeStruct((B, L, N - HY_PROJ), BF16)],
        compiler_params=_cparams(("parallel", "parallel")),
        name="inproj",
    )(x, xh, xh, g, shift, scale, w, conv_w, conv_b, *rope)


def _block_diag_mask():
    r = lax.broadcasted_iota(jnp.int32, (LANES, LANES), 0) // RET_HEAD_DIM
    c = lax.broadcasted_iota(jnp.int32, (LANES, LANES), 1) // RET_HEAD_DIM
    return r == c


def _kt_v(k, v):
    return lax.dot_general(k, v, (((0,), (0,)), ((), ())), preferred_element_type=F32)


def _ctx_state_kernel(x_ref, g_ref, sh_ref, sc_ref, wk_ref, wv_ref, lgf_ref, lgb_ref, sf_ref, sb_ref):
    hb = _norm_mod(x_ref[0], g_ref, sh_ref, sc_ref)
    k = jnp.dot(hb, wk_ref[...], preferred_element_type=F32) * (RET_HEAD_DIM ** -0.5)
    v = jnp.dot(hb, wv_ref[...], preferred_element_type=F32).astype(BF16)
    lc = k.shape[0]
    m = lax.broadcasted_iota(jnp.int32, (lc, LANES), 0).astype(F32)
    bd = _block_diag_mask()
    for p in range(RET_BLOCKS):
        lanes = slice(p * LANES, (p + 1) * LANES)
        wf = jnp.exp(lgf_ref[p] * (lc - 1.0 - m))
        wb = jnp.exp(lgb_ref[p] * m)
        sf_ref[0, p] = jnp.where(bd, _kt_v((k[:, lanes] * wf).astype(BF16), v[:, lanes]), 0.0)
        sb_ref[0, p] = jnp.where(bd, _kt_v((k[:, lanes] * wb).astype(BF16), v[:, lanes]), 0.0)


def _ctx_states(ctx, g, shift, scale, wk, wv, lgf, lgb):
    B, lc, D = ctx.shape
    st = jax.ShapeDtypeStruct((B, RET_BLOCKS, LANES, LANES), F32)
    sspec = pl.BlockSpec((1, RET_BLOCKS, LANES, LANES), lambda b: (b, 0, 0, 0))
    return pl.pallas_call(
        _ctx_state_kernel,
        grid=(B,),
        in_specs=[pl.BlockSpec((1, lc, D), lambda b: (b, 0, 0)), _const_spec((1, D)),
                  _const_spec(shift.shape), _const_spec(scale.shape), _const_spec(wk.shape), _const_spec(wv.shape),
                  _const_spec(lgf.shape), _const_spec(lgb.shape)],
        out_specs=[sspec, sspec],
        out_shape=[st, st],
        compiler_params=_cparams(("parallel",)),
        name="ctx_state",
    )(ctx, g, shift, scale, wk, wv, lgf, lgb)


FILT_ROWS = 1024
FILT_GROUPS = LANES // HY_BANDS
FILT_HALVES = LANES // HY_FILT_HID


def _filt_kernel(bands_ref, w1t_ref, w1c_ref, w1s_ref, b1_ref, fr1_ref, w2_ref, b2_ref, fr2_ref,
                 w3_ref, dl_ref, h_ref, asum_ref, *, seq_len):
    i = pl.program_id(0)
    tl = FILT_ROWS
    gr = tl // FILT_GROUPS
    hr = tl // FILT_HALVES
    lane = lax.broadcasted_iota(jnp.int32, (1, LANES), 1)
    base = i * tl
    dot = functools.partial(jnp.dot, preferred_element_type=F32, precision=HIGHEST)

    n8 = (lax.broadcasted_iota(jnp.int32, (gr, 1), 0) + (lane // HY_BANDS) * gr + base).astype(F32)
    arg = bands_ref[...] * ((2.0 * math.pi / seq_len) * n8)
    cz, sz = jnp.cos(arg), jnp.sin(arg)
    nh = (lax.broadcasted_iota(jnp.int32, (hr, 1), 0) + (lane // HY_FILT_HID) * hr + base).astype(F32)
    blocks = [dot(cz, w1c_ref[r]) - dot(sz, w1s_ref[r]) for r in range(hr // gr)]
    pre = jnp.concatenate(blocks, axis=0) + (nh / (seq_len - 1.0)) * w1t_ref[...] + b1_ref[...]
    h = jnp.sin(fr1_ref[...] * pre)
    h = jnp.sin(fr2_ref[...] * (dot(h, w2_ref[...]) + b2_ref[...])).astype(BF16)

    @pl.when(i == 0)
    def _():
        asum_ref[...] = jnp.zeros_like(asum_ref)

    for p in range(FILT_HALVES):
        n = (lax.broadcasted_iota(jnp.int32, (hr, 1), 0) + p * hr + base).astype(F32)
        t = n / (seq_len - 1.0)
        taps = jnp.dot(h, w3_ref[p], preferred_element_type=F32) * jnp.exp(-t * dl_ref[...])
        asum_ref[...] += jnp.sum(jnp.abs(taps), axis=0, keepdims=True)
        rows = slice(p * hr, (p + 1) * hr)
        h_ref[0, rows] = taps[:, :HY_WIDTH].astype(BF16)
        h_ref[1, rows] = jnp.where(n == 0.0, 0.0, taps[:, HY_WIDTH:]).astype(BF16)


def _filters(seq_len, w1, b1, fr1, w2, b2, fr2, w3):
    hid = HY_FILT_HID
    f32 = lambda a: a.astype(F32)
    bands = np.tile(np.linspace(1e-4, HY_BANDS - 1, HY_BANDS, dtype=np.float32), FILT_GROUPS)[None]
    deltas = np.abs(np.linspace(math.log(HY_DECAY_TARGET) / HY_SLOW_PCT,
                                math.log(HY_DECAY_TARGET) / HY_FAST_PCT, HY_WIDTH, dtype=np.float32))
    dl = np.tile(deltas, 2)[None, :]
    halves = lambda v: jnp.tile(f32(v)[None], (1, FILT_HALVES))

    nb = FILT_GROUPS // FILT_HALVES
    sel = np.zeros((nb, FILT_GROUPS, FILT_HALVES), np.float32)
    for r in range(nb):
        for p in range(FILT_HALVES):
            sel[r, p * nb + r, p] = 1.0
    eye = np.eye(FILT_HALVES, dtype=np.float32)

    def place_l1(wb):
        return (sel[:, :, None, :, None] * f32(wb)[None, None, :, None, :]).reshape(nb, LANES, LANES)

    w2d = (eye[:, None, :, None] * f32(w2)[None, :, None, :]).reshape(LANES, LANES)
    w3p = (eye[:, :, None, None] * f32(w3)[None, None]).reshape(FILT_HALVES, LANES, 2 * HY_WIDTH)
    args = (jnp.asarray(bands), halves(w1[0]), place_l1(w1[1:1 + HY_BANDS]), place_l1(w1[1 + HY_BANDS:]),
            halves(b1), halves(fr1), w2d, halves(b2), halves(fr2), w3p.astype(BF16), jnp.asarray(dl))
    return pl.pallas_call(
        functools.partial(_filt_kernel, seq_len=seq_len),
        grid=(seq_len // FILT_ROWS,),
        in_specs=[_const_spec(a.shape) for a in args],
        out_specs=[pl.BlockSpec((2, FILT_ROWS, HY_WIDTH), lambda i: (0, i, 0)),
                   _const_spec((1, 2 * HY_WIDTH))],
        out_shape=[jax.ShapeDtypeStruct((2, seq_len, HY_WIDTH), BF16),
                   jax.ShapeDtypeStruct((1, 2 * HY_WIDTH), F32)],
        compiler_params=_cparams(("arbitrary",)),
        name="filt",
    )(*args)


def _dft_tables(seq_len):
    n = 2 * seq_len
    n1_full = n // DFT_N2
    n1_used = n1_full // 2
    nk1 = n1_full // 2 + 1
    k1 = np.arange(nk1)[:, None]
    n1 = np.arange(n1_used)[None, :]
    th = 2.0 * np.pi * ((k1 * n1) % n1_full) / n1_full
    ck = np.full((nk1, 1), 2.0)
    ck[0, 0] = 1.0
    ck[-1, 0] = 1.0
    assert nk1 <= DFT_KHALF
    f1 = np.zeros((2 * DFT_KHALF, n1_used))
    f1[:nk1] = np.cos(th)
    f1[DFT_KHALF:DFT_KHALF + nk1] = -np.sin(th)
    finv = np.zeros((n1_used, 2 * DFT_KHALF))
    finv[:, 0:2 * nk1:2] = (ck * np.cos(th) / n).T
    finv[:, 1:2 * nk1:2] = (-ck * np.sin(th) / n).T
    k1 = np.arange(nk1)[:, None, None]
    k2 = np.arange(DFT_N2)[None, :, None]
    n2 = np.arange(DFT_N2)[None, None, :]
    ph = 2.0 * np.pi * ((n1_full * n2 * k2 + n2 * k1) % n) / n
    gr, gi = np.cos(ph), -np.sin(ph)
    g = np.concatenate([np.concatenate([gr, -gi], 2), np.concatenate([gi, gr], 2)], 1)
    grt, git = np.swapaxes(gr, 1, 2), -np.swapaxes(gi, 1, 2)
    ginv = np.concatenate([np.concatenate([grt, -git], 2), np.concatenate([git, grt], 2)], 1)
    f = lambda a: jnp.asarray(a.astype(np.float32))
    return f(f1), f(finv), f(g), f(ginv), nk1


def _rows2d(a):
    return a.reshape(a.shape[0] * a.shape[1], a.shape[2])


def _dft_a_kernel(x_ref, f_ref, o_ref):
    n1u = x_ref.shape[1]
    xs = _rows2d(jnp.swapaxes(x_ref[0], 0, 1))
    f = f_ref[...]
    res = [jnp.dot(f, xs[r * n1u:(r + 1) * n1u], preferred_element_type=F32).astype(BF16)
           for r in range(DFT_ROWS)]
    a = jnp.swapaxes(jnp.stack(res, axis=0), 0, 1)
    o_ref[:, 0] = a[:DFT_KHALF]
    o_ref[:, 1] = a[DFT_KHALF:]


def _dft_a(xv, f1):
    Bd, n1u, n2, C = xv.shape
    return pl.pallas_call(
        _dft_a_kernel,
        grid=(Bd, n2 // DFT_ROWS),
        in_specs=[pl.BlockSpec((1, n1u, DFT_ROWS, C), lambda b, j: (b, 0, j, 0)), _const_spec(f1.shape)],
        out_specs=pl.BlockSpec((DFT_KHALF, 2, DFT_ROWS, C), lambda b, j: (0, 0, j, b)),
        out_shape=jax.ShapeDtypeStruct((DFT_KHALF, 2, n2, Bd * C), BF16),
        compiler_params=_cparams(("parallel", "parallel")),
        name="dft_a",
    )(xv, f1)


def _dft_c_kernel(a_ref, h_ref, g_ref, gi_ref, asum_ref, o_ref):
    g = g_ref[0].astype(BF16)
    gi = gi_ref[0].astype(BF16)
    rows = g.shape[0]
    half = rows // 2
    C = HY_WIDTH
    xh = jnp.dot(g, h_ref[0].reshape(rows, h_ref.shape[3]), preferred_element_type=F32)
    asum = asum_ref[...]
    inv = 1.0 / (asum[:, :C] + asum[:, C:] + 1e-6)
    kr = (xh[:half, :C] + xh[:half, C:]) * inv
    ki = (xh[half:, :C] - xh[half:, C:]) * inv
    step = min(DFTC_LANES, a_ref.shape[3])
    for l0 in range(0, a_ref.shape[3], step):
        x = jnp.dot(g, a_ref[0, :, :, l0:l0 + step].reshape(rows, step), preferred_element_type=F32)
        ys = []
        for c0 in range(0, step, C):
            xr, xi = x[:half, c0:c0 + C], x[half:, c0:c0 + C]
            ys.append(jnp.concatenate([xr * kr - xi * ki, xr * ki + xi * kr], axis=0).astype(BF16))
        bk = jnp.dot(gi, jnp.concatenate(ys, axis=1), preferred_element_type=F32).astype(BF16)
        o_ref[0, :, :, l0:l0 + step] = bk.reshape(2, half, step)


def _dft_c(a, h, g, ginv, asum, nk1):
    _, _, n2, bc = a.shape
    dspec = lambda w: pl.BlockSpec((1, 2, n2, w), lambda k: (k, 0, 0, 0))
    gspec = pl.BlockSpec((1, 2 * n2, 2 * n2), lambda k: (k, 0, 0))
    return pl.pallas_call(
        _dft_c_kernel,
        grid=(nk1,),
        in_specs=[dspec(bc), dspec(h.shape[3]), gspec, gspec, _const_spec(asum.shape)],
        out_specs=dspec(bc),
        out_shape=jax.ShapeDtypeStruct((nk1, 2, n2, bc), BF16),
        compiler_params=_cparams(("parallel",)),
        name="dft_c",
    )(a, h, g, ginv, asum)


def _idft_a_kernel(b_ref, f_ref, vg_ref, x0_ref, bias_ref, o_ref):
    nk1, _, R, C = b_ref.shape
    mp = f_ref.shape[1]
    bv = b_ref[...].reshape(2 * nk1, R, C)
    bv = jnp.concatenate([bv, jnp.zeros((mp - 2 * nk1, R, C), BF16)], axis=0)
    bs = _rows2d(jnp.swapaxes(bv, 0, 1))
    f = f_ref[...]
    ys = [jnp.dot(f, bs[r * mp:(r + 1) * mp], preferred_element_type=F32).astype(BF16) for r in range(R)]
    y = _rows2d(jnp.swapaxes(jnp.stack(ys, axis=0), 0, 1)).astype(F32)
    y = y + _rows2d(vg_ref[0]).astype(F32) * bias_ref[...]
    o_ref[0] = (y * _rows2d(x0_ref[0]).astype(F32)).astype(BF16).reshape(o_ref.shape[1:])


def _idft_a(bk, finv, vgv, x0v, bias):
    B, n1u, n2, C = vgv.shape
    nk1 = bk.shape[0]
    dsp = pl.BlockSpec((1, n1u, DFT_ROWS, C), lambda b, j: (b, 0, j, 0))
    return pl.pallas_call(
        _idft_a_kernel,
        grid=(B, n2 // DFT_ROWS),
        in_specs=[pl.BlockSpec((nk1, 2, DFT_ROWS, C), lambda b, j: (0, 0, j, b)),
                  _const_spec(finv.shape), dsp, dsp, _const_spec(bias.shape)],
        out_specs=dsp,
        out_shape=jax.ShapeDtypeStruct((B, n1u, n2, C), BF16),
        compiler_params=_cparams(("parallel", "parallel")),
        name="idft_a",
    )(bk, finv, vgv, x0v, bias)


def _retention_kernel(q_ref, k_ref, v_ref, g_ref, lgf_ref, lgb_ref, s0f_ref, s0b_ref, gn_ref, o_ref,
                      inc_ref, st_ref, dm_ref, wt_ref):
    C = RET_CHUNK
    seq_len = q_ref.shape[1]
    nc = seq_len // C
    lgf = lgf_ref[0]
    lgb = lgb_ref[0]
    lane = lax.broadcasted_iota(jnp.int32, (1, LANES), 1)
    head_of_lane = lane // RET_HEAD_DIM
    bd = _block_diag_mask()

    ri = lax.broadcasted_iota(jnp.int32, (C, LANES), 0).astype(F32)
    wt_ref[0] = jnp.exp(lgf * (ri + 1.0)).astype(BF16)
    wt_ref[1] = jnp.exp(lgb * (C - ri)).astype(BF16)
    wt_ref[2] = jnp.exp(lgf * (C - 1.0 - ri)).astype(BF16)
    wt_ref[3] = jnp.exp(lgb * ri).astype(BF16)
    di = lax.broadcasted_iota(jnp.int32, (C, C), 0)
    dj = lax.broadcasted_iota(jnp.int32, (C, C), 1)
    dd = (di - dj).astype(F32)
    for h in range(HEADS_PER_BLOCK):
        lf = jnp.sum(jnp.where(lane == h * RET_HEAD_DIM, lgf, 0.0), axis=1, keepdims=True)
        lb = jnp.sum(jnp.where(lane == h * RET_HEAD_DIM, lgb, 0.0), axis=1, keepdims=True)
        dm_ref[h] = jnp.where(dd > 0, jnp.exp(lf * dd), jnp.where(dd < 0, jnp.exp(-lb * dd), 2.0)).astype(BF16)

    def chunk(c):
        return pl.ds(pl.multiple_of(c * C, C), C)

    def inc_body(c, carry):
        rows = chunk(c)
        k = k_ref[0, rows, :]
        kw = jnp.concatenate([k * wt_ref[2], k * wt_ref[3]], axis=1)
        t = _kt_v(kw, v_ref[0, rows, :])
        inc_ref[c, :LANES] = jnp.where(bd, t[:LANES], 0.0)
        inc_ref[c, LANES:] = jnp.where(bd, t[LANES:], 0.0)
        return carry

    lax.fori_loop(0, nc, inc_body, 0, unroll=True)

    gcf = jnp.exp(lgf * float(C))
    gcb = jnp.exp(lgb * float(C))

    def fscan(c, s):
        st_ref[c, :LANES] = s.astype(BF16)
        return s * gcf + inc_ref[c, :LANES]

    lax.fori_loop(0, nc, fscan, s0f_ref[0, 0])

    def bscan(i, s):
        c = nc - 1 - i
        st_ref[c, LANES:] = s.astype(BF16)
        return s * gcb + inc_ref[c, LANES:]

    lax.fori_loop(0, nc, bscan, s0b_ref[0, 0])

    gn = gn_ref[...]
    first = head_of_lane == 0
    inv_n = 1.0 / RET_HEAD_DIM

    def head_mean(a):
        tot = jnp.sum(a, axis=1, keepdims=True)
        s0 = jnp.sum(jnp.where(first, a, 0.0), axis=1, keepdims=True)
        return jnp.where(first, s0, tot - s0) * inv_n

    def out_body(c, carry):
        rows = chunk(c)
        q = q_ref[0, rows, :]
        k = k_ref[0, rows, :]
        v = v_ref[0, rows, :]
        qw = jnp.concatenate([q * wt_ref[0], q * wt_ref[1]], axis=1)
        o = jnp.dot(qw, st_ref[c], preferred_element_type=F32)
        oh = []
        for h in range(HEADS_PER_BLOCK):
            qm = jnp.where(head_of_lane == h, q, jnp.zeros_like(q))
            sc = lax.dot_general(qm, k, (((1,), (1,)), ((), ())), preferred_element_type=F32)
            oh.append(jnp.dot(sc.astype(BF16) * dm_ref[h], v, preferred_element_type=F32))
        o = o + jnp.where(first, oh[0], oh[1])
        d = o - head_mean(o)
        y = d * lax.rsqrt(head_mean(d * d) + NORM_EPS) * gn
        o_ref[0, rows, :] = (y * _silu(g_ref[0, rows, :].astype(F32))).astype(BF16)
        return carry

    lax.fori_loop(0, nc, out_body, 0, unroll=True)


def _retention(u, lgf, lgb, s0f, s0b, gn):
    B, L, _ = u.shape
    assert HEADS_PER_BLOCK == 2
    nc = L // RET_CHUNK
    col = lambda g: pl.BlockSpec((1, L, LANES), lambda b, p: (b, 0, g * RET_BLOCKS + p))
    lsp = pl.BlockSpec((1, 1, LANES), lambda b, p: (p, 0, 0))
    ssp = pl.BlockSpec((1, 1, LANES, LANES), lambda b, p: (b, p, 0, 0))
    return pl.pallas_call(
        _retention_kernel,
        grid=(B, RET_BLOCKS),
        in_specs=[col(0), col(1), col(2), col(3), lsp, lsp, ssp, ssp,
                  pl.BlockSpec((1, LANES), lambda b, p: (0, p))],
        out_specs=pl.BlockSpec((1, L, LANES), lambda b, p: (b, 0, p)),
        out_shape=jax.ShapeDtypeStruct((B, L, RET_WIDTH), BF16),
        scratch_shapes=[pltpu.VMEM((nc, 2 * LANES, LANES), F32),
                        pltpu.VMEM((nc, 2 * LANES, LANES), BF16),
                        pltpu.VMEM((HEADS_PER_BLOCK, RET_CHUNK, RET_CHUNK), BF16),
                        pltpu.VMEM((4, RET_CHUNK, LANES), BF16)],
        compiler_params=_cparams(("parallel", "parallel")),
        name="retention",
    )(u, u, u, u, lgf, lgb, s0f, s0b, gn)


def _rope_tables(seq_len):
    rows = seq_len // GRID_W
    r, col = np.meshgrid(np.arange(rows, dtype=np.float64), np.arange(GRID_W, dtype=np.float64), indexing="ij")
    inv = ROPE_BASE ** (-np.arange(ROPE_PAIRS_AXIS, dtype=np.float64) / ROPE_PAIRS_AXIS)
    ang = np.concatenate([r.reshape(-1, 1) * inv, col.reshape(-1, 1) * inv], axis=-1)
    ang = np.tile(np.repeat(ang, 2, axis=1), (1, HEADS_PER_BLOCK))
    sign = np.where(np.arange(LANES) % 2 == 0, -1.0, 1.0)
    return jnp.asarray(np.cos(ang).astype(np.float32)), jnp.asarray((np.sin(ang) * sign).astype(np.float32))


def _out_mlp_kernel(x_ref, yh_ref, yr_ref, wo1_ref, wo2_ref, w1_ref, w2_ref, g2_ref, gf_ref, mod_ref, o_ref,
                    *, ff_chunk):
    mod = mod_ref[0]
    mix = (jnp.dot(yh_ref[0], wo1_ref[...], preferred_element_type=F32)
           + jnp.dot(yr_ref[0], wo2_ref[...], preferred_element_type=F32))
    x1 = x_ref[0] + mod[0:1] * mix
    h = (_rms(x1) * g2_ref[...] * (1.0 + mod[2:3]) + mod[1:2]).astype(BF16)
    acc = jnp.zeros_like(x1)
    for j in range(w1_ref.shape[1] // ff_chunk):
        cols = slice(j * ff_chunk, (j + 1) * ff_chunk)
        a = jnp.maximum(jnp.dot(h, w1_ref[:, cols], preferred_element_type=F32), 0.0)
        acc = acc + jnp.dot((a * a).astype(BF16), w2_ref[cols, :], preferred_element_type=F32)
    x2 = x1 + mod[3:4] * acc
    o_ref[0] = _rms(x2) * gf_ref[...]


def _out_mlp(x, yh, yr, wo1, wo2, w1, w2, g2, gf, mod, tm):
    B, L, D = x.shape
    dff = w1.shape[1]
    row = lambda n: pl.BlockSpec((1, tm, n), lambda b, i: (b, i, 0))
    wsp = lambda a: pl.BlockSpec(a.shape, lambda b, i: (0, 0), pipeline_mode=pl.Buffered(1))
    return pl.pallas_call(
        functools.partial(_out_mlp_kernel, ff_chunk=1024),
        grid=(B, L // tm),
        in_specs=[row(D), row(yh.shape[2]), row(yr.shape[2]), wsp(wo1), wsp(wo2), wsp(w1), wsp(w2),
                  _const_spec((1, D)), _const_spec((1, D)),
                  pl.BlockSpec((1, 4, D), lambda b, i: (b, 0, 0))],
        out_specs=row(D),
        out_shape=jax.ShapeDtypeStruct((B, L, D), F32),
        compiler_params=_cparams(("parallel", "parallel")),
        name="out_mlp",
    )(x, yh, yr, wo1, wo2, w1, w2, g2, gf, mod)


def kernel(x, c, ctx, c_ctx, w_ada, b_ada, norm1_g, w_in, hy_conv_w, hy_conv_b, hy_f_w1, hy_f_b1, hy_f_freq1,
           hy_f_w2, hy_f_b2, hy_f_freq2, hy_f_w3, hy_bias, ret_decay_logit, ret_gn_g, w_out, norm2_g, w_mlp1,
           w_mlp2, norm_f_g):
    B, L, D = x.shape
    depth = w_ada.shape[0]
    assert depth == 1, "single-layer block"
    l = 0
    rows = 16
    cvec = jnp.concatenate([c, c_ctx[None], jnp.zeros((rows - B - 1, D), F32)], axis=0)
    mod = _ada(cvec, w_ada[l], b_ada[l][None])
    mx = mod[:B].reshape(B, 6, D)
    mc = mod[B].reshape(6, D)

    lg = jax.nn.log_sigmoid(ret_decay_logit[l].astype(F32))
    lg_lanes = jnp.repeat(lg, RET_HEAD_DIM, axis=1).reshape(2, RET_BLOCKS, 1, LANES)
    lgf, lgb = lg_lanes[0], lg_lanes[1]

    w_in_b = w_in[l].astype(BF16)
    g1 = norm1_g[l][None]
    vg, x0c, u_ret = _inproj(x, g1, mx[:, 0:1], mx[:, 1:2], w_in_b, hy_conv_w[l].astype(F32),
                             hy_conv_b[l][None].astype(F32), _rope_tables(L), tm=1024)
    kc0 = HY_PROJ + RET_WIDTH
    s0f, s0b = _ctx_states(ctx, g1, mc[0][None, None], mc[1][None, None], w_in_b[:, kc0:kc0 + RET_WIDTH],
                           w_in_b[:, kc0 + RET_WIDTH:kc0 + 2 * RET_WIDTH], lgf, lgb)

    f1, finv, gmat, ginv, nk1 = _dft_tables(L)
    f1, finv = f1.astype(BF16), finv.astype(BF16)
    taps, asum = _filters(L, hy_f_w1[l], hy_f_b1[l], hy_f_freq1[l], hy_f_w2[l], hy_f_b2[l], hy_f_freq2[l],
                          hy_f_w3[l])
    tview = lambda a: a.reshape(a.shape[0], L // DFT_N2, DFT_N2, HY_WIDTH)
    bk = _dft_c(_dft_a(tview(vg), f1), _dft_a(tview(taps), f1), gmat, ginv, asum, nk1)
    y_hy = _idft_a(bk, finv, tview(vg), tview(x0c), hy_bias[l][None].astype(F32)).reshape(B, L, HY_WIDTH)

    y_ret = _retention(u_ret, lgf, lgb, s0f, s0b, ret_gn_g[l][None].astype(F32))

    w_out_b = w_out[l].astype(BF16)
    mod2 = mx[:, 2:6]
    return _out_mlp(x, y_hy, y_ret, w_out_b[:HY_WIDTH], w_out_b[HY_WIDTH:], w_mlp1[l].astype(BF16),
                    w_mlp2[l].astype(BF16), norm2_g[l][None], norm_f_g[None], mod2, tm=1024)
```

```python
import functools
import math

import numpy as np
import jax
import jax.numpy as jnp
from jax import lax
from jax.experimental import pallas as pl
from jax.experimental.pallas import tpu as pltpu

F32 = jnp.float32
BF16 = jnp.bfloat16
HIGHEST = lax.Precision.HIGHEST

GRID_W = 64
HY_WIDTH = 512
RET_WIDTH = 512
RET_HEADS = 8
RET_HEAD_DIM = RET_WIDTH // RET_HEADS
HY_PROJ = 3 * HY_WIDTH
HY_BANDS = 16
HY_FILT_HID = 64
HY_DECAY_TARGET = 1e-2
HY_FAST_PCT = 0.3
HY_SLOW_PCT = 1.5
ROPE_BASE = 10000.0
ROPE_PAIRS_AXIS = RET_HEAD_DIM // 4
NORM_EPS = 1e-6

LANES = 128
HEADS_PER_BLOCK = LANES // RET_HEAD_DIM
RET_BLOCKS = RET_WIDTH // LANES
RET_CHUNK = 256
DFT_N2 = 128
DFT_ROWS = 16
DFT_BLOCK_ROWS = 32
DFT_KHALF = 72
DFTC_LANES = 2048
VMEM_LIMIT = 56 * 1024 * 1024


def _cparams(sem):
    return pltpu.CompilerParams(dimension_semantics=sem, vmem_limit_bytes=VMEM_LIMIT)


def _const_spec(shape):
    nd = len(shape)
    return pl.BlockSpec(shape, lambda *_: (0,) * nd)


def _silu(x):
    return x * jax.nn.sigmoid(x)


def _rms(x):
    return x * lax.rsqrt(jnp.mean(x * x, axis=-1, keepdims=True) + NORM_EPS)


def _ada_kernel(c_ref, w_ref, b_ref, o_ref):
    s = _silu(c_ref[...])
    o_ref[...] = jnp.dot(s, w_ref[...], preferred_element_type=F32, precision=HIGHEST) + b_ref[...]


def _ada(cvec, w, b):
    R, D = cvec.shape
    N = w.shape[1]
    return pl.pallas_call(
        _ada_kernel,
        grid=(N // D,),
        in_specs=[_const_spec((R, D)), pl.BlockSpec((D, D), lambda j: (0, j)),
                  pl.BlockSpec((1, D), lambda j: (0, j))],
        out_specs=pl.BlockSpec((R, D), lambda j: (0, j)),
        out_shape=jax.ShapeDtypeStruct((R, N), F32),
        compiler_params=_cparams(("arbitrary",)),
        name="ada",
    )(cvec, w, b)


INPROJ_NCHUNK = 1024
INPROJ_SUBTILES = 1


HALO = 16


def _norm_mod(x, g_ref, sh_ref, sc_ref):
    h = _rms(x) * g_ref[...]
    return (h * (1.0 + sc_ref[0]) + sh_ref[0]).astype(BF16)


def _inproj_kernel(x_ref, xp_ref, xn_ref, g_ref, sh_ref, sc_ref, w_ref, cw_ref, cb_ref, cos_ref, sin_ref,
                   vg_ref, x0_ref, ur_ref):
    i = pl.program_id(1)
    tm = x_ref.shape[1]
    ts = tm // INPROJ_SUBTILES
    ext = ts + 2 * HALO
    xa = jnp.concatenate([xp_ref[0, 0], x_ref[0], xn_ref[0, 0]], axis=0)
    row = lax.broadcasted_iota(jnp.int32, (ext, 1), 0)
    even = (lax.broadcasted_iota(jnp.int32, (1, LANES), 1) % 2) == 0
    for s in range(INPROJ_SUBTILES):
        rows = slice(s * ts, (s + 1) * ts)
        hb = _norm_mod(xa[s * ts:s * ts + ext], g_ref, sh_ref, sc_ref)
        if s == 0:
            hb = jnp.where((row < HALO) & (i == 0), jnp.zeros_like(hb), hb)
        if s == INPROJ_SUBTILES - 1:
            hb = jnp.where((row >= ts + HALO) & (i == pl.num_programs(1) - 1), jnp.zeros_like(hb), hb)
        conv = []
        for c0 in range(0, HY_PROJ, HY_WIDTH):
            cols = slice(c0, c0 + HY_WIDTH)
            r = jnp.dot(hb, w_ref[:, cols], preferred_element_type=F32)
            cw = cw_ref[:, cols]
            y = pltpu.roll(r, 1, 0) * cw[0:1] + r * cw[1:2] + pltpu.roll(r, ext - 1, 0) * cw[2:3] + cb_ref[:, cols]
            conv.append(y[HALO:HALO + ts])
        x0_ref[0, rows] = conv[0].astype(BF16)
        vg_ref[0, rows] = (conv[2] * conv[1]).astype(BF16)

        hm = hb[HALO:HALO + ts]
        for c0 in range(HY_PROJ, w_ref.shape[1], INPROJ_NCHUNK):
            r = jnp.dot(hm, w_ref[:, c0:c0 + INPROJ_NCHUNK], preferred_element_type=F32)
            parts = []
            for b0 in range(0, INPROJ_NCHUNK, LANES):
                t = r[:, b0:b0 + LANES]
                if c0 + b0 < HY_PROJ + 2 * RET_WIDTH:
                    sw = jnp.where(even, pltpu.roll(t, LANES - 1, 1), pltpu.roll(t, 1, 1))
                    t = t * cos_ref[rows] + sw * sin_ref[rows]
                    if c0 + b0 >= HY_PROJ + RET_WIDTH:
                        t = t * (RET_HEAD_DIM ** -0.5)
                parts.append(t)
            ur_ref[0, rows, c0 - HY_PROJ:c0 - HY_PROJ + INPROJ_NCHUNK] = jnp.concatenate(parts, axis=1).astype(BF16)


def _inproj(x, g, shift, scale, w, conv_w, conv_b, rope, tm):
    B, L, D = x.shape
    N = w.shape[1]
    nh = tm // HALO
    nrow = L // HALO
    xh = x.reshape(B, nrow, HALO, D)
    vec = pl.BlockSpec((1, 1, D), lambda b, i: (b, 0, 0))
    row = lambda n: pl.BlockSpec((1, tm, n), lambda b, i: (b, i, 0))
    tab = pl.BlockSpec((tm, LANES), lambda b, i: (i, 0))
    return pl.pallas_call(
        _inproj_kernel,
        grid=(B, L // tm),
        in_specs=[row(D),
                  pl.BlockSpec((1, 1, HALO, D), lambda b, i: (b, jnp.maximum(i * nh - 1, 0), 0, 0)),
                  pl.BlockSpec((1, 1, HALO, D), lambda b, i: (b, jnp.minimum((i + 1) * nh, nrow - 1), 0, 0)),
                  _const_spec((1, D)), vec, vec,
                  pl.BlockSpec((D, N), lambda b, i: (0, 0), pipeline_mode=pl.Buffered(1)),
                  _const_spec(conv_w.shape), _const_spec(conv_b.shape), tab, tab],
        out_specs=[row(HY_WIDTH), row(HY_WIDTH), row(N - HY_PROJ)],
        out_shape=[jax.ShapeDtypeStruct((B, L, HY_WIDTH), BF16), jax.ShapeDtypeStruct((B, L, HY_WIDTH), BF16),
                   jax.ShapeDtypeStruct((B, L, N - HY_PROJ), BF16)],
        compiler_params=_cparams(("parallel", "parallel")),
        name="inproj",
    )(x, xh, xh, g, shift, scale, w, conv_w, conv_b, *rope)


def _block_diag_mask():
    r = lax.broadcasted_iota(jnp.int32, (LANES, LANES), 0) // RET_HEAD_DIM
    c = lax.broadcasted_iota(jnp.int32, (LANES, LANES), 1) // RET_HEAD_DIM
    return r == c


def _kt_v(k, v):
    return lax.dot_general(k, v, (((0,), (0,)), ((), ())), preferred_element_type=F32)


def _ctx_state_kernel(x_ref, g_ref, sh_ref, sc_ref, wk_ref, wv_ref, lgf_ref, lgb_ref, sf_ref, sb_ref):
    hb = _norm_mod(x_ref[0], g_ref, sh_ref, sc_ref)
    k = jnp.dot(hb, wk_ref[...], preferred_element_type=F32) * (RET_HEAD_DIM ** -0.5)
    v = jnp.dot(hb, wv_ref[...], preferred_element_type=F32).astype(BF16)
    lc = k.shape[0]
    m = lax.broadcasted_iota(jnp.int32, (lc, LANES), 0).astype(F32)
    bd = _block_diag_mask()
    for p in range(RET_BLOCKS):
        lanes = slice(p * LANES, (p + 1) * LANES)
        wf = jnp.exp(lgf_ref[p] * (lc - 1.0 - m))
        wb = jnp.exp(lgb_ref[p] * m)
        sf_ref[0, p] = jnp.where(bd, _kt_v((k[:, lanes] * wf).astype(BF16), v[:, lanes]), 0.0)
        sb_ref[0, p] = jnp.where(bd, _kt_v((k[:, lanes] * wb).astype(BF16), v[:, lanes]), 0.0)


def _ctx_states(ctx, g, shift, scale, wk, wv, lgf, lgb):
    B, lc, D = ctx.shape
    st = jax.ShapeDtypeStruct((B, RET_BLOCKS, LANES, LANES), F32)
    sspec = pl.BlockSpec((1, RET_BLOCKS, LANES, LANES), lambda b: (b, 0, 0, 0))
    return pl.pallas_call(
        _ctx_state_kernel,
        grid=(B,),
        in_specs=[pl.BlockSpec((1, lc, D), lambda b: (b, 0, 0)), _const_spec((1, D)),
                  _const_spec(shift.shape), _const_spec(scale.shape), _const_spec(wk.shape), _const_spec(wv.shape),
                  _const_spec(lgf.shape), _const_spec(lgb.shape)],
        out_specs=[sspec, sspec],
        out_shape=[st, st],
        compiler_params=_cparams(("parallel",)),
        name="ctx_state",
    )(ctx, g, shift, scale, wk, wv, lgf, lgb)


FILT_ROWS = 1024
FILT_GROUPS = LANES // HY_BANDS
FILT_HALVES = LANES // HY_FILT_HID


def _filt_kernel(bands_ref, w1t_ref, w1c_ref, w1s_ref, b1_ref, fr1_ref, w2_ref, b2_ref, fr2_ref,
                 w3_ref, dl_ref, h_ref, asum_ref, *, seq_len):
    i = pl.program_id(0)
    tl = FILT_ROWS
    gr = tl // FILT_GROUPS
    hr = tl // FILT_HALVES
    lane = lax.broadcasted_iota(jnp.int32, (1, LANES), 1)
    base = i * tl
    dot = functools.partial(jnp.dot, preferred_element_type=F32, precision=HIGHEST)

    n8 = (lax.broadcasted_iota(jnp.int32, (gr, 1), 0) + (lane // HY_BANDS) * gr + base).astype(F32)
    arg = bands_ref[...] * ((2.0 * math.pi / seq_len) * n8)
    cz, sz = jnp.cos(arg), jnp.sin(arg)
    nh = (lax.broadcasted_iota(jnp.int32, (hr, 1), 0) + (lane // HY_FILT_HID) * hr + base).astype(F32)
    blocks = [dot(cz, w1c_ref[r]) - dot(sz, w1s_ref[r]) for r in range(hr // gr)]
    pre = jnp.concatenate(blocks, axis=0) + (nh / (seq_len - 1.0)) * w1t_ref[...] + b1_ref[...]
    h = jnp.sin(fr1_ref[...] * pre)
    h = jnp.sin(fr2_ref[...] * (dot(h, w2_ref[...]) + b2_ref[...])).astype(BF16)

    @pl.when(i == 0)
    def _():
        asum_ref[...] = jnp.zeros_like(asum_ref)

    for p in range(FILT_HALVES):
        n = (lax.broadcasted_iota(jnp.int32, (hr, 1), 0) + p * hr + base).astype(F32)
        t = n / (seq_len - 1.0)
        taps = jnp.dot(h, w3_ref[p], preferred_element_type=F32) * jnp.exp(-t * dl_ref[...])
        asum_ref[...] += jnp.sum(jnp.abs(taps), axis=0, keepdims=True)
        rows = slice(p * hr, (p + 1) * hr)
        h_ref[0, rows] = taps[:, :HY_WIDTH].astype(BF16)
        h_ref[1, rows] = jnp.where(n == 0.0, 0.0, taps[:, HY_WIDTH:]).astype(BF16)


def _filters(seq_len, w1, b1, fr1, w2, b2, fr2, w3):
    hid = HY_FILT_HID
    f32 = lambda a: a.astype(F32)
    bands = np.tile(np.linspace(1e-4, HY_BANDS - 1, HY_BANDS, dtype=np.float32), FILT_GROUPS)[None]
    deltas = np.abs(np.linspace(math.log(HY_DECAY_TARGET) / HY_SLOW_PCT,
                                math.log(HY_DECAY_TARGET) / HY_FAST_PCT, HY_WIDTH, dtype=np.float32))
    dl = np.tile(deltas, 2)[None, :]
    halves = lambda v: jnp.tile(f32(v)[None], (1, FILT_HALVES))

    nb = FILT_GROUPS // FILT_HALVES
    sel = np.zeros((nb, FILT_GROUPS, FILT_HALVES), np.float32)
    for r in range(nb):
        for p in range(FILT_HALVES):
            sel[r, p * nb + r, p] = 1.0
    eye = np.eye(FILT_HALVES, dtype=np.float32)

    def place_l1(wb):
        return (sel[:, :, None, :, None] * f32(wb)[None, None, :, None, :]).reshape(nb, LANES, LANES)

    w2d = (eye[:, None, :, None] * f32(w2)[None, :, None, :]).reshape(LANES, LANES)
    w3p = (eye[:, :, None, None] * f32(w3)[None, None]).reshape(FILT_HALVES, LANES, 2 * HY_WIDTH)
    args = (jnp.asarray(bands), halves(w1[0]), place_l1(w1[1:1 + HY_BANDS]), place_l1(w1[1 + HY_BANDS:]),
            halves(b1), halves(fr1), w2d, halves(b2), halves(fr2), w3p.astype(BF16), jnp.asarray(dl))
    return pl.pallas_call(
        functools.partial(_filt_kernel, seq_len=seq_len),
        grid=(seq_len // FILT_ROWS,),
        in_specs=[_const_spec(a.shape) for a in args],
        out_specs=[pl.BlockSpec((2, FILT_ROWS, HY_WIDTH), lambda i: (0, i, 0)),
                   _const_spec((1, 2 * HY_WIDTH))],
        out_shape=[jax.ShapeDtypeStruct((2, seq_len, HY_WIDTH), BF16),
                   jax.ShapeDtypeStruct((1, 2 * HY_WIDTH), F32)],
        compiler_params=_cparams(("arbitrary",)),
        name="filt",
    )(*args)


def _dft_tables(seq_len):
    n = 2 * seq_len
    n1_full = n // DFT_N2
    n1_used = n1_full // 2
    nk1 = n1_full // 2 + 1
    k1 = np.arange(nk1)[:, None]
    n1 = np.arange(n1_used)[None, :]
    th = 2.0 * np.pi * ((k1 * n1) % n1_full) / n1_full
    ck = np.full((nk1, 1), 2.0)
    ck[0, 0] = 1.0
    ck[-1, 0] = 1.0
    assert nk1 <= DFT_KHALF
    f1 = np.zeros((2 * DFT_KHALF, n1_used))
    f1[:nk1] = np.cos(th)
    f1[DFT_KHALF:DFT_KHALF + nk1] = -np.sin(th)
    finv = np.zeros((n1_used, 2 * DFT_KHALF))
    finv[:, 0:2 * nk1:2] = (ck * np.cos(th) / n).T
    finv[:, 1:2 * nk1:2] = (-ck * np.sin(th) / n).T
    k1 = np.arange(nk1)[:, None, None]
    k2 = np.arange(DFT_N2)[None, :, None]
    n2 = np.arange(DFT_N2)[None, None, :]
    ph = 2.0 * np.pi * ((n1_full * n2 * k2 + n2 * k1) % n) / n
    gr, gi = np.cos(ph), -np.sin(ph)
    g = np.concatenate([np.concatenate([gr, -gi], 2), np.concatenate([gi, gr], 2)], 1)
    grt, git = np.swapaxes(gr, 1, 2), -np.swapaxes(gi, 1, 2)
    ginv = np.concatenate([np.concatenate([grt, -git], 2), np.concatenate([git, grt], 2)], 1)
    f = lambda a: jnp.asarray(a.astype(np.float32))
    return f(f1), f(finv), f(g), f(ginv), nk1


def _rows2d(a):
    return a.reshape(a.shape[0] * a.shape[1], a.shape[2])


def _dft_a_kernel(x_ref, f_ref, o_ref):
    n1u = x_ref.shape[1]
    f = f_ref[...]
    for r0 in range(0, x_ref.shape[2], DFT_ROWS):
        grp = slice(r0, r0 + DFT_ROWS)
        xs = _rows2d(jnp.swapaxes(x_ref[0, :, grp, :], 0, 1))
        res = [jnp.dot(f, xs[r * n1u:(r + 1) * n1u], preferred_element_type=F32).astype(BF16)
               for r in range(DFT_ROWS)]
        a = jnp.swapaxes(jnp.stack(res, axis=0), 0, 1)
        o_ref[:, 0, grp, :] = a[:DFT_KHALF]
        o_ref[:, 1, grp, :] = a[DFT_KHALF:]


def _dft_a(xv, f1):
    Bd, n1u, n2, C = xv.shape
    return pl.pallas_call(
        _dft_a_kernel,
        grid=(Bd, n2 // DFT_BLOCK_ROWS),
        in_specs=[pl.BlockSpec((1, n1u, DFT_BLOCK_ROWS, C), lambda b, j: (b, 0, j, 0)), _const_spec(f1.shape)],
        out_specs=pl.BlockSpec((DFT_KHALF, 2, DFT_BLOCK_ROWS, C), lambda b, j: (0, 0, j, b)),
        out_shape=jax.ShapeDtypeStruct((DFT_KHALF, 2, n2, Bd * C), BF16),
        compiler_params=_cparams(("parallel", "parallel")),
        name="dft_a",
    )(xv, f1)


def _dft_c_kernel(a_ref, h_ref, g_ref, gi_ref, asum_ref, o_ref):
    g = g_ref[0].astype(BF16)
    gi = gi_ref[0].astype(BF16)
    rows = g.shape[0]
    half = rows // 2
    C = HY_WIDTH
    xh = jnp.dot(g, h_ref[0].reshape(rows, h_ref.shape[3]), preferred_element_type=F32)
    asum = asum_ref[...]
    inv = 1.0 / (asum[:, :C] + asum[:, C:] + 1e-6)
    kr = (xh[:half, :C] + xh[:half, C:]) * inv
    ki = (xh[half:, :C] - xh[half:, C:]) * inv
    step = min(DFTC_LANES, a_ref.shape[3])
    for l0 in range(0, a_ref.shape[3], step):
        x = jnp.dot(g, a_ref[0, :, :, l0:l0 + step].reshape(rows, step), preferred_element_type=F32)
        ys = []
        for c0 in range(0, step, C):
            xr, xi = x[:half, c0:c0 + C], x[half:, c0:c0 + C]
            ys.append(jnp.concatenate([xr * kr - xi * ki, xr * ki + xi * kr], axis=0).astype(BF16))
        bk = jnp.dot(gi, jnp.concatenate(ys, axis=1), preferred_element_type=F32).astype(BF16)
        o_ref[0, :, :, l0:l0 + step] = bk.reshape(2, half, step)


def _dft_c(a, h, g, ginv, asum, nk1):
    _, _, n2, bc = a.shape
    dspec = lambda w: pl.BlockSpec((1, 2, n2, w), lambda k: (k, 0, 0, 0))
    gspec = pl.BlockSpec((1, 2 * n2, 2 * n2), lambda k: (k, 0, 0))
    return pl.pallas_call(
        _dft_c_kernel,
        grid=(nk1,),
        in_specs=[dspec(bc), dspec(h.shape[3]), gspec, gspec, _const_spec(asum.shape)],
        out_specs=dspec(bc),
        out_shape=jax.ShapeDtypeStruct((nk1, 2, n2, bc), BF16),
        compiler_params=_cparams(("parallel",)),
        name="dft_c",
    )(a, h, g, ginv, asum)


def _idft_a_kernel(b_ref, f_ref, vg_ref, x0_ref, bias_ref, o_ref):
    nk1, _, _, C = b_ref.shape
    n1u = vg_ref.shape[1]
    R = DFT_ROWS
    mp = f_ref.shape[1]
    f = f_ref[...]
    for r0 in range(0, b_ref.shape[2], R):
        grp = slice(r0, r0 + R)
        bv = b_ref[:, :, grp, :].reshape(2 * nk1, R, C)
        bv = jnp.concatenate([bv, jnp.zeros((mp - 2 * nk1, R, C), BF16)], axis=0)
        bs = _rows2d(jnp.swapaxes(bv, 0, 1))
        ys = [jnp.dot(f, bs[r * mp:(r + 1) * mp], preferred_element_type=F32).astype(BF16) for r in range(R)]
        y = _rows2d(jnp.swapaxes(jnp.stack(ys, axis=0), 0, 1)).astype(F32)
        y = y + _rows2d(vg_ref[0, :, grp, :]).astype(F32) * bias_ref[...]
        y = y * _rows2d(x0_ref[0, :, grp, :]).astype(F32)
        o_ref[0, :, grp, :] = y.astype(BF16).reshape(n1u, R, C)


def _idft_a(bk, finv, vgv, x0v, bias):
    B, n1u, n2, C = vgv.shape
    nk1 = bk.shape[0]
    dsp = pl.BlockSpec((1, n1u, DFT_BLOCK_ROWS, C), lambda b, j: (b, 0, j, 0))
    return pl.pallas_call(
        _idft_a_kernel,
        grid=(B, n2 // DFT_BLOCK_ROWS),
        in_specs=[pl.BlockSpec((nk1, 2, DFT_BLOCK_ROWS, C), lambda b, j: (0, 0, j, b)),
                  _const_spec(finv.shape), dsp, dsp, _const_spec(bias.shape)],
        out_specs=dsp,
        out_shape=jax.ShapeDtypeStruct((B, n1u, n2, C), BF16),
        compiler_params=_cparams(("parallel", "parallel")),
        name="idft_a",
    )(bk, finv, vgv, x0v, bias)


def _retention_kernel(q_ref, k_ref, v_ref, g_ref, lgf_ref, lgb_ref, s0f_ref, s0b_ref, gn_ref, o_ref,
                      inc_ref, st_ref, dm_ref, wt_ref):
    C = RET_CHUNK
    seq_len = q_ref.shape[1]
    nc = seq_len // C
    lgf = lgf_ref[0]
    lgb = lgb_ref[0]
    lane = lax.broadcasted_iota(jnp.int32, (1, LANES), 1)
    head_of_lane = lane // RET_HEAD_DIM
    bd = _block_diag_mask()

    ri = lax.broadcasted_iota(jnp.int32, (C, LANES), 0).astype(F32)
    wt_ref[0] = jnp.exp(lgf * (ri + 1.0)).astype(BF16)
    wt_ref[1] = jnp.exp(lgb * (C - ri)).astype(BF16)
    wt_ref[2] = jnp.exp(lgf * (C - 1.0 - ri)).astype(BF16)
    wt_ref[3] = jnp.exp(lgb * ri).astype(BF16)
    di = lax.broadcasted_iota(jnp.int32, (C, C), 0)
    dj = lax.broadcasted_iota(jnp.int32, (C, C), 1)
    dd = (di - dj).astype(F32)
    for h in range(HEADS_PER_BLOCK):
        lf = jnp.sum(jnp.where(lane == h * RET_HEAD_DIM, lgf, 0.0), axis=1, keepdims=True)
        lb = jnp.sum(jnp.where(lane == h * RET_HEAD_DIM, lgb, 0.0), axis=1, keepdims=True)
        dm_ref[h] = jnp.where(dd > 0, jnp.exp(lf * dd), jnp.where(dd < 0, jnp.exp(-lb * dd), 2.0)).astype(BF16)

    def chunk(c):
        return pl.ds(pl.multiple_of(c * C, C), C)

    def inc_body(c, carry):
        rows = chunk(c)
        k = k_ref[0, rows, :]
        kw = jnp.concatenate([k * wt_ref[2], k * wt_ref[3]], axis=1)
        t = _kt_v(kw, v_ref[0, rows, :])
        inc_ref[c, :LANES] = jnp.where(bd, t[:LANES], 0.0)
        inc_ref[c, LANES:] = jnp.where(bd, t[LANES:], 0.0)
        return carry

    lax.fori_loop(0, nc, inc_body, 0, unroll=True)

    gcf = jnp.exp(lgf * float(C))
    gcb = jnp.exp(lgb * float(C))

    def fscan(c, s):
        st_ref[c, :LANES] = s.astype(BF16)
        return s * gcf + inc_ref[c, :LANES]

    lax.fori_loop(0, nc, fscan, s0f_ref[0, 0])

    def bscan(i, s):
        c = nc - 1 - i
        st_ref[c, LANES:] = s.astype(BF16)
        return s * gcb + inc_ref[c, LANES:]

    lax.fori_loop(0, nc, bscan, s0b_ref[0, 0])

    gn = gn_ref[...]
    first = head_of_lane == 0
    inv_n = 1.0 / RET_HEAD_DIM

    def head_mean(a):
        tot = jnp.sum(a, axis=1, keepdims=True)
        s0 = jnp.sum(jnp.where(first, a, 0.0), axis=1, keepdims=True)
        return jnp.where(first, s0, tot - s0) * inv_n

    def out_body(c, carry):
        rows = chunk(c)
        q = q_ref[0, rows, :]
        k = k_ref[0, rows, :]
        v = v_ref[0, rows, :]
        qw = jnp.concatenate([q * wt_ref[0], q * wt_ref[1]], axis=1)
        o = jnp.dot(qw, st_ref[c], preferred_element_type=F32)
        oh = []
        for h in range(HEADS_PER_BLOCK):
            qm = jnp.where(head_of_lane == h, q, jnp.zeros_like(q))
            sc = lax.dot_general(qm, k, (((1,), (1,)), ((), ())), preferred_element_type=F32)
            oh.append(jnp.dot(sc.astype(BF16) * dm_ref[h], v, preferred_element_type=F32))
        o = o + jnp.where(first, oh[0], oh[1])
        d = o - head_mean(o)
        y = d * lax.rsqrt(head_mean(d * d) + NORM_EPS) * gn
        o_ref[0, rows, :] = (y * _silu(g_ref[0, rows, :].astype(F32))).astype(BF16)
        return carry

    lax.fori_loop(0, nc, out_body, 0, unroll=True)


def _retention(u, lgf, lgb, s0f, s0b, gn):
    B, L, _ = u.shape
    assert HEADS_PER_BLOCK == 2
    nc = L // RET_CHUNK
    col = lambda g: pl.BlockSpec((1, L, LANES), lambda b, p: (b, 0, g * RET_BLOCKS + p))
    lsp = pl.BlockSpec((1, 1, LANES), lambda b, p: (p, 0, 0))
    ssp = pl.BlockSpec((1, 1, LANES, LANES), lambda b, p: (b, p, 0, 0))
    return pl.pallas_call(
        _retention_kernel,
        grid=(B, RET_BLOCKS),
        in_specs=[col(0), col(1), col(2), col(3), lsp, lsp, ssp, ssp,
                  pl.BlockSpec((1, LANES), lambda b, p: (0, p))],
        out_specs=pl.BlockSpec((1, L, LANES), lambda b, p: (b, 0, p)),
        out_shape=jax.ShapeDtypeStruct((B, L, RET_WIDTH), BF16),
        scratch_shapes=[pltpu.VMEM((nc, 2 * LANES, LANES), F32),
                        pltpu.VMEM((nc, 2 * LANES, LANES), BF16),
                        pltpu.VMEM((HEADS_PER_BLOCK, RET_CHUNK, RET_CHUNK), BF16),
                        pltpu.VMEM((4, RET_CHUNK, LANES), BF16)],
        compiler_params=_cparams(("parallel", "parallel")),
        name="retention",
    )(u, u, u, u, lgf, lgb, s0f, s0b, gn)


def _rope_tables(seq_len):
    rows = seq_len // GRID_W
    r, col = np.meshgrid(np.arange(rows, dtype=np.float64), np.arange(GRID_W, dtype=np.float64), indexing="ij")
    inv = ROPE_BASE ** (-np.arange(ROPE_PAIRS_AXIS, dtype=np.float64) / ROPE_PAIRS_AXIS)
    ang = np.concatenate([r.reshape(-1, 1) * inv, col.reshape(-1, 1) * inv], axis=-1)
    ang = np.tile(np.repeat(ang, 2, axis=1), (1, HEADS_PER_BLOCK))
    sign = np.where(np.arange(LANES) % 2 == 0, -1.0, 1.0)
    return jnp.asarray(np.cos(ang).astype(np.float32)), jnp.asarray((np.sin(ang) * sign).astype(np.float32))


def _out_mlp_kernel(x_ref, yh_ref, yr_ref, wo1_ref, wo2_ref, w1_ref, w2_ref, g2_ref, gf_ref, mod_ref, o_ref,
                    *, ff_chunk):
    mod = mod_ref[0]
    mix = (jnp.dot(yh_ref[0], wo1_ref[...], preferred_element_type=F32)
           + jnp.dot(yr_ref[0], wo2_ref[...], preferred_element_type=F32))
    x1 = x_ref[0] + mod[0:1] * mix
    h = (_rms(x1) * g2_ref[...] * (1.0 + mod[2:3]) + mod[1:2]).astype(BF16)
    acc = jnp.zeros_like(x1)
    for j in range(w1_ref.shape[1] // ff_chunk):
        cols = slice(j * ff_chunk, (j + 1) * ff_chunk)
        a = jnp.maximum(jnp.dot(h, w1_ref[:, cols], preferred_element_type=F32), 0.0)
        acc = acc + jnp.dot((a * a).astype(BF16), w2_ref[cols, :], preferred_element_type=F32)
    x2 = x1 + mod[3:4] * acc
    o_ref[0] = _rms(x2) * gf_ref[...]


def _out_mlp(x, yh, yr, wo1, wo2, w1, w2, g2, gf, mod, tm):
    B, L, D = x.shape
    dff = w1.shape[1]
    row = lambda n: pl.BlockSpec((1, tm, n), lambda b, i: (b, i, 0))
    wsp = lambda a: pl.BlockSpec(a.shape, lambda b, i: (0, 0), pipeline_mode=pl.Buffered(1))
    return pl.pallas_call(
        functools.partial(_out_mlp_kernel, ff_chunk=1024),
        grid=(B, L // tm),
        in_specs=[row(D), row(yh.shape[2]), row(yr.shape[2]), wsp(wo1), wsp(wo2), wsp(w1), wsp(w2),
                  _const_spec((1, D)), _const_spec((1, D)),
                  pl.BlockSpec((1, 4, D), lambda b, i: (b, 0, 0))],
        out_specs=row(D),
        out_shape=jax.ShapeDtypeStruct((B, L, D), F32),
        compiler_params=_cparams(("parallel", "parallel")),
        name="out_mlp",
    )(x, yh, yr, wo1, wo2, w1, w2, g2, gf, mod)


def kernel(x, c, ctx, c_ctx, w_ada, b_ada, norm1_g, w_in, hy_conv_w, hy_conv_b, hy_f_w1, hy_f_b1, hy_f_freq1,
           hy_f_w2, hy_f_b2, hy_f_freq2, hy_f_w3, hy_bias, ret_decay_logit, ret_gn_g, w_out, norm2_g, w_mlp1,
           w_mlp2, norm_f_g):
    B, L, D = x.shape
    depth = w_ada.shape[0]
    assert depth == 1, "single-layer block"
    l = 0
    rows = 16
    cvec = jnp.concatenate([c, c_ctx[None], jnp.zeros((rows - B - 1, D), F32)], axis=0)
    mod = _ada(cvec, w_ada[l], b_ada[l][None])
    mx = mod[:B].reshape(B, 6, D)
    mc = mod[B].reshape(6, D)

    lg = jax.nn.log_sigmoid(ret_decay_logit[l].astype(F32))
    lg_lanes = jnp.repeat(lg, RET_HEAD_DIM, axis=1).reshape(2, RET_BLOCKS, 1, LANES)
    lgf, lgb = lg_lanes[0], lg_lanes[1]

    w_in_b = w_in[l].astype(BF16)
    g1 = norm1_g[l][None]
    vg, x0c, u_ret = _inproj(x, g1, mx[:, 0:1], mx[:, 1:2], w_in_b, hy_conv_w[l].astype(F32),
                             hy_conv_b[l][None].astype(F32), _rope_tables(L), tm=1024)
    kc0 = HY_PROJ + RET_WIDTH
    s0f, s0b = _ctx_states(ctx, g1, mc[0][None, None], mc[1][None, None], w_in_b[:, kc0:kc0 + RET_WIDTH],
                           w_in_b[:, kc0 + RET_WIDTH:kc0 + 2 * RET_WIDTH], lgf, lgb)

    f1, finv, gmat, ginv, nk1 = _dft_tables(L)
    f1, finv = f1.astype(BF16), finv.astype(BF16)
    taps, asum = _filters(L, hy_f_w1[l], hy_f_b1[l], hy_f_freq1[l], hy_f_w2[l], hy_f_b2[l], hy_f_freq2[l],
                          hy_f_w3[l])
    tview = lambda a: a.reshape(a.shape[0], L // DFT_N2, DFT_N2, HY_WIDTH)
    bk = _dft_c(_dft_a(tview(vg), f1), _dft_a(tview(taps), f1), gmat, ginv, asum, nk1)
    y_hy = _idft_a(bk, finv, tview(vg), tview(x0c), hy_bias[l][None].astype(F32)).reshape(B, L, HY_WIDTH)

    y_ret = _retention(u_ret, lgf, lgb, s0f, s0b, ret_gn_g[l][None].astype(F32))

    w_out_b = w_out[l].astype(BF16)
    mod2 = mx[:, 2:6]
    return _out_mlp(x, y_hy, y_ret, w_out_b[:HY_WIDTH], w_out_b[HY_WIDTH:], w_mlp1[l].astype(BF16),
                    w_mlp2[l].astype(BF16), norm2_g[l][None], norm_f_g[None], mod2, tm=1024)
```

```python
import functools
import math

import numpy as np
import jax
import jax.numpy as jnp
from jax import lax
from jax.experimental import pallas as pl
from jax.experimental.pallas import tpu as pltpu

F32 = jnp.float32
BF16 = jnp.bfloat16
HIGHEST = lax.Precision.HIGHEST

GRID_W = 64
HY_WIDTH = 512
RET_WIDTH = 512
RET_HEADS = 8
RET_HEAD_DIM = RET_WIDTH // RET_HEADS
HY_PROJ = 3 * HY_WIDTH
HY_BANDS = 16
HY_FILT_HID = 64
HY_DECAY_TARGET = 1e-2
HY_FAST_PCT = 0.3
HY_SLOW_PCT = 1.5
ROPE_BASE = 10000.0
ROPE_PAIRS_AXIS = RET_HEAD_DIM // 4
NORM_EPS = 1e-6

LANES = 128
HEADS_PER_BLOCK = LANES // RET_HEAD_DIM
RET_BLOCKS = RET_WIDTH // LANES
RET_CHUNK = 256
DFT_N2 = 128
DFT_ROWS = 16
DFT_BLOCK_ROWS = 64
DFT_KHALF = 72
DFTC_LANES = 2048
DFTC_K1 = 2
VMEM_LIMIT = 56 * 1024 * 1024


def _cparams(sem):
    return pltpu.CompilerParams(dimension_semantics=sem, vmem_limit_bytes=VMEM_LIMIT)


def _const_spec(shape):
    nd = len(shape)
    return pl.BlockSpec(shape, lambda *_: (0,) * nd)


def _silu(x):
    return x * jax.nn.sigmoid(x)


def _rms(x):
    return x * lax.rsqrt(jnp.mean(x * x, axis=-1, keepdims=True) + NORM_EPS)


def _ada_kernel(c_ref, w_ref, b_ref, o_ref):
    s = _silu(c_ref[...])
    o_ref[...] = jnp.dot(s, w_ref[...], preferred_element_type=F32, precision=HIGHEST) + b_ref[...]


def _ada(cvec, w, b):
    R, D = cvec.shape
    N = w.shape[1]
    return pl.pallas_call(
        _ada_kernel,
        grid=(N // D,),
        in_specs=[_const_spec((R, D)), pl.BlockSpec((D, D), lambda j: (0, j)),
                  pl.BlockSpec((1, D), lambda j: (0, j))],
        out_specs=pl.BlockSpec((R, D), lambda j: (0, j)),
        out_shape=jax.ShapeDtypeStruct((R, N), F32),
        compiler_params=_cparams(("arbitrary",)),
        name="ada",
    )(cvec, w, b)


INPROJ_NCHUNK = 1024
INPROJ_SUBTILES = 1


HALO = 16


def _norm_mod(x, g_ref, sh_ref, sc_ref):
    h = _rms(x) * g_ref[...]
    return (h * (1.0 + sc_ref[0]) + sh_ref[0]).astype(BF16)


def _inproj_kernel(x_ref, xp_ref, xn_ref, g_ref, sh_ref, sc_ref, w_ref, cw_ref, cb_ref, cos_ref, sin_ref,
                   vg_ref, x0_ref, ur_ref):
    i = pl.program_id(1)
    tm = x_ref.shape[1]
    ts = tm // INPROJ_SUBTILES
    ext = ts + 2 * HALO
    xa = jnp.concatenate([xp_ref[0, 0], x_ref[0], xn_ref[0, 0]], axis=0)
    row = lax.broadcasted_iota(jnp.int32, (ext, 1), 0)
    even = (lax.broadcasted_iota(jnp.int32, (1, LANES), 1) % 2) == 0
    for s in range(INPROJ_SUBTILES):
        rows = slice(s * ts, (s + 1) * ts)
        hb = _norm_mod(xa[s * ts:s * ts + ext], g_ref, sh_ref, sc_ref)
        if s == 0:
            hb = jnp.where((row < HALO) & (i == 0), jnp.zeros_like(hb), hb)
        if s == INPROJ_SUBTILES - 1:
            hb = jnp.where((row >= ts + HALO) & (i == pl.num_programs(1) - 1), jnp.zeros_like(hb), hb)
        conv = []
        for c0 in range(0, HY_PROJ, HY_WIDTH):
            cols = slice(c0, c0 + HY_WIDTH)
            r = jnp.dot(hb, w_ref[:, cols], preferred_element_type=F32)
            cw = cw_ref[:, cols]
            y = pltpu.roll(r, 1, 0) * cw[0:1] + r * cw[1:2] + pltpu.roll(r, ext - 1, 0) * cw[2:3] + cb_ref[:, cols]
            conv.append(y[HALO:HALO + ts])
        x0_ref[0, rows] = conv[0].astype(BF16)
        vg_ref[0, rows] = (conv[2] * conv[1]).astype(BF16)

        hm = hb[HALO:HALO + ts]
        for c0 in range(HY_PROJ, w_ref.shape[1], INPROJ_NCHUNK):
            r = jnp.dot(hm, w_ref[:, c0:c0 + INPROJ_NCHUNK], preferred_element_type=F32)
            parts = []
            for b0 in range(0, INPROJ_NCHUNK, LANES):
                t = r[:, b0:b0 + LANES]
                if c0 + b0 < HY_PROJ + 2 * RET_WIDTH:
                    sw = jnp.where(even, pltpu.roll(t, LANES - 1, 1), pltpu.roll(t, 1, 1))
                    t = t * cos_ref[rows] + sw * sin_ref[rows]
                    if c0 + b0 >= HY_PROJ + RET_WIDTH:
                        t = t * (RET_HEAD_DIM ** -0.5)
                parts.append(t)
            ur_ref[0, rows, c0 - HY_PROJ:c0 - HY_PROJ + INPROJ_NCHUNK] = jnp.concatenate(parts, axis=1).astype(BF16)


def _inproj(x, g, shift, scale, w, conv_w, conv_b, rope, tm):
    B, L, D = x.shape
    N = w.shape[1]
    nh = tm // HALO
    nrow = L // HALO
    xh = x.reshape(B, nrow, HALO, D)
    vec = pl.BlockSpec((1, 1, D), lambda b, i: (b, 0, 0))
    row = lambda n: pl.BlockSpec((1, tm, n), lambda b, i: (b, i, 0))
    tab = pl.BlockSpec((tm, LANES), lambda b, i: (i, 0))
    return pl.pallas_call(
        _inproj_kernel,
        grid=(B, L // tm),
        in_specs=[row(D),
                  pl.BlockSpec((1, 1, HALO, D), lambda b, i: (b, jnp.maximum(i * nh - 1, 0), 0, 0)),
                  pl.BlockSpec((1, 1, HALO, D), lambda b, i: (b, jnp.minimum((i + 1) * nh, nrow - 1), 0, 0)),
                  _const_spec((1, D)), vec, vec,
                  pl.BlockSpec((D, N), lambda b, i: (0, 0), pipeline_mode=pl.Buffered(1)),
                  _const_spec(conv_w.shape), _const_spec(conv_b.shape), tab, tab],
        out_specs=[row(HY_WIDTH), row(HY_WIDTH), row(N - HY_PROJ)],
        out_shape=[jax.ShapeDtypeStruct((B, L, HY_WIDTH), BF16), jax.ShapeDtypeStruct((B, L, HY_WIDTH), BF16),
                   jax.ShapeDtypeStruct((B, L, N - HY_PROJ), BF16)],
        compiler_params=_cparams(("parallel", "parallel")),
        name="inproj",
    )(x, xh, xh, g, shift, scale, w, conv_w, conv_b, *rope)


def _block_diag_mask():
    r = lax.broadcasted_iota(jnp.int32, (LANES, LANES), 0) // RET_HEAD_DIM
    c = lax.broadcasted_iota(jnp.int32, (LANES, LANES), 1) // RET_HEAD_DIM
    return r == c


def _kt_v(k, v):
    return lax.dot_general(k, v, (((0,), (0,)), ((), ())), preferred_element_type=F32)


def _ctx_state_kernel(x_ref, g_ref, sh_ref, sc_ref, wk_ref, wv_ref, lgf_ref, lgb_ref, sf_ref, sb_ref):
    hb = _norm_mod(x_ref[0], g_ref, sh_ref, sc_ref)
    k = jnp.dot(hb, wk_ref[...], preferred_element_type=F32) * (RET_HEAD_DIM ** -0.5)
    v = jnp.dot(hb, wv_ref[...], preferred_element_type=F32).astype(BF16)
    lc = k.shape[0]
    m = lax.broadcasted_iota(jnp.int32, (lc, LANES), 0).astype(F32)
    bd = _block_diag_mask()
    for p in range(RET_BLOCKS):
        lanes = slice(p * LANES, (p + 1) * LANES)
        wf = jnp.exp(lgf_ref[p] * (lc - 1.0 - m))
        wb = jnp.exp(lgb_ref[p] * m)
        sf_ref[0, p] = jnp.where(bd, _kt_v((k[:, lanes] * wf).astype(BF16), v[:, lanes]), 0.0)
        sb_ref[0, p] = jnp.where(bd, _kt_v((k[:, lanes] * wb).astype(BF16), v[:, lanes]), 0.0)


def _ctx_states(ctx, g, shift, scale, wk, wv, lgf, lgb):
    B, lc, D = ctx.shape
    st = jax.ShapeDtypeStruct((B, RET_BLOCKS, LANES, LANES), F32)
    sspec = pl.BlockSpec((1, RET_BLOCKS, LANES, LANES), lambda b: (b, 0, 0, 0))
    return pl.pallas_call(
        _ctx_state_kernel,
        grid=(B,),
        in_specs=[pl.BlockSpec((1, lc, D), lambda b: (b, 0, 0)), _const_spec((1, D)),
                  _const_spec(shift.shape), _const_spec(scale.shape), _const_spec(wk.shape), _const_spec(wv.shape),
                  _const_spec(lgf.shape), _const_spec(lgb.shape)],
        out_specs=[sspec, sspec],
        out_shape=[st, st],
        compiler_params=_cparams(("parallel",)),
        name="ctx_state",
    )(ctx, g, shift, scale, wk, wv, lgf, lgb)


FILT_ROWS = 1024
FILT_GROUPS = LANES // HY_BANDS
FILT_HALVES = LANES // HY_FILT_HID


def _filt_kernel(bands_ref, w1t_ref, w1c_ref, w1s_ref, b1_ref, fr1_ref, w2_ref, b2_ref, fr2_ref,
                 w3_ref, dl_ref, h_ref, asum_ref, *, seq_len):
    i = pl.program_id(0)
    tl = FILT_ROWS
    gr = tl // FILT_GROUPS
    hr = tl // FILT_HALVES
    lane = lax.broadcasted_iota(jnp.int32, (1, LANES), 1)
    base = i * tl
    dot = functools.partial(jnp.dot, preferred_element_type=F32, precision=HIGHEST)

    n8 = (lax.broadcasted_iota(jnp.int32, (gr, 1), 0) + (lane // HY_BANDS) * gr + base).astype(F32)
    arg = bands_ref[...] * ((2.0 * math.pi / seq_len) * n8)
    cz, sz = jnp.cos(arg), jnp.sin(arg)
    nh = (lax.broadcasted_iota(jnp.int32, (hr, 1), 0) + (lane // HY_FILT_HID) * hr + base).astype(F32)
    blocks = [dot(cz, w1c_ref[r]) - dot(sz, w1s_ref[r]) for r in range(hr // gr)]
    pre = jnp.concatenate(blocks, axis=0) + (nh / (seq_len - 1.0)) * w1t_ref[...] + b1_ref[...]
    h = jnp.sin(fr1_ref[...] * pre)
    h = jnp.sin(fr2_ref[...] * (dot(h, w2_ref[...]) + b2_ref[...])).astype(BF16)

    @pl.when(i == 0)
    def _():
        asum_ref[...] = jnp.zeros_like(asum_ref)

    for p in range(FILT_HALVES):
        n = (lax.broadcasted_iota(jnp.int32, (hr, 1), 0) + p * hr + base).astype(F32)
        t = n / (seq_len - 1.0)
        taps = jnp.dot(h, w3_ref[p], preferred_element_type=F32) * jnp.exp(-t * dl_ref[...])
        asum_ref[...] += jnp.sum(jnp.abs(taps), axis=0, keepdims=True)
        rows = slice(p * hr, (p + 1) * hr)
        h_ref[0, rows] = taps[:, :HY_WIDTH].astype(BF16)
        h_ref[1, rows] = jnp.where(n == 0.0, 0.0, taps[:, HY_WIDTH:]).astype(BF16)


def _filters(seq_len, w1, b1, fr1, w2, b2, fr2, w3):
    hid = HY_FILT_HID
    f32 = lambda a: a.astype(F32)
    bands = np.tile(np.linspace(1e-4, HY_BANDS - 1, HY_BANDS, dtype=np.float32), FILT_GROUPS)[None]
    deltas = np.abs(np.linspace(math.log(HY_DECAY_TARGET) / HY_SLOW_PCT,
                                math.log(HY_DECAY_TARGET) / HY_FAST_PCT, HY_WIDTH, dtype=np.float32))
    dl = np.tile(deltas, 2)[None, :]
    halves = lambda v: jnp.tile(f32(v)[None], (1, FILT_HALVES))

    nb = FILT_GROUPS // FILT_HALVES
    sel = np.zeros((nb, FILT_GROUPS, FILT_HALVES), np.float32)
    for r in range(nb):
        for p in range(FILT_HALVES):
            sel[r, p * nb + r, p] = 1.0
    eye = np.eye(FILT_HALVES, dtype=np.float32)

    def place_l1(wb):
        return (sel[:, :, None, :, None] * f32(wb)[None, None, :, None, :]).reshape(nb, LANES, LANES)

    w2d = (eye[:, None, :, None] * f32(w2)[None, :, None, :]).reshape(LANES, LANES)
    w3p = (eye[:, :, None, None] * f32(w3)[None, None]).reshape(FILT_HALVES, LANES, 2 * HY_WIDTH)
    args = (jnp.asarray(bands), halves(w1[0]), place_l1(w1[1:1 + HY_BANDS]), place_l1(w1[1 + HY_BANDS:]),
            halves(b1), halves(fr1), w2d, halves(b2), halves(fr2), w3p.astype(BF16), jnp.asarray(dl))
    return pl.pallas_call(
        functools.partial(_filt_kernel, seq_len=seq_len),
        grid=(seq_len // FILT_ROWS,),
        in_specs=[_const_spec(a.shape) for a in args],
        out_specs=[pl.BlockSpec((2, FILT_ROWS, HY_WIDTH), lambda i: (0, i, 0)),
                   _const_spec((1, 2 * HY_WIDTH))],
        out_shape=[jax.ShapeDtypeStruct((2, seq_len, HY_WIDTH), BF16),
                   jax.ShapeDtypeStruct((1, 2 * HY_WIDTH), F32)],
        compiler_params=_cparams(("arbitrary",)),
        name="filt",
    )(*args)


def _dft_tables(seq_len):
    n = 2 * seq_len
    n1_full = n // DFT_N2
    n1_used = n1_full // 2
    nk1 = n1_full // 2 + 1
    k1 = np.arange(nk1)[:, None]
    n1 = np.arange(n1_used)[None, :]
    th = 2.0 * np.pi * ((k1 * n1) % n1_full) / n1_full
    ck = np.full((nk1, 1), 2.0)
    ck[0, 0] = 1.0
    ck[-1, 0] = 1.0
    assert nk1 <= DFT_KHALF
    f1 = np.zeros((2 * DFT_KHALF, n1_used))
    f1[:nk1] = np.cos(th)
    f1[DFT_KHALF:DFT_KHALF + nk1] = -np.sin(th)
    finv = np.zeros((n1_used, 2 * DFT_KHALF))
    finv[:, 0:2 * nk1:2] = (ck * np.cos(th) / n).T
    finv[:, 1:2 * nk1:2] = (-ck * np.sin(th) / n).T
    k1 = np.arange(nk1)[:, None, None]
    k2 = np.arange(DFT_N2)[None, :, None]
    n2 = np.arange(DFT_N2)[None, None, :]
    ph = 2.0 * np.pi * ((n1_full * n2 * k2 + n2 * k1) % n) / n
    gr, gi = np.cos(ph), -np.sin(ph)
    g = np.concatenate([np.concatenate([gr, -gi], 2), np.concatenate([gi, gr], 2)], 1)
    grt, git = np.swapaxes(gr, 1, 2), -np.swapaxes(gi, 1, 2)
    ginv = np.concatenate([np.concatenate([grt, -git], 2), np.concatenate([git, grt], 2)], 1)
    npad = -nk1 % DFTC_K1
    assert nk1 + npad <= DFT_KHALF
    g, ginv = (np.concatenate([m, np.zeros((npad,) + m.shape[1:])], 0) for m in (g, ginv))
    f = lambda a: jnp.asarray(a.astype(np.float32))
    return f(f1), f(finv), f(g), f(ginv)


def _rows2d(a):
    return a.reshape(a.shape[0] * a.shape[1], a.shape[2])


def _dft_a_kernel(x_ref, f_ref, o_ref):
    n1u = x_ref.shape[1]
    f = f_ref[...]
    for r0 in range(0, x_ref.shape[2], DFT_ROWS):
        grp = slice(r0, r0 + DFT_ROWS)
        xs = _rows2d(jnp.swapaxes(x_ref[0, :, grp, :], 0, 1))
        res = [jnp.dot(f, xs[r * n1u:(r + 1) * n1u], preferred_element_type=F32).astype(BF16)
               for r in range(DFT_ROWS)]
        a = jnp.swapaxes(jnp.stack(res, axis=0), 0, 1)
        o_ref[:, 0, grp, :] = a[:DFT_KHALF]
        o_ref[:, 1, grp, :] = a[DFT_KHALF:]


def _dft_a(xv, f1):
    Bd, n1u, n2, C = xv.shape
    return pl.pallas_call(
        _dft_a_kernel,
        grid=(Bd, n2 // DFT_BLOCK_ROWS),
        in_specs=[pl.BlockSpec((1, n1u, DFT_BLOCK_ROWS, C), lambda b, j: (b, 0, j, 0)), _const_spec(f1.shape)],
        out_specs=pl.BlockSpec((DFT_KHALF, 2, DFT_BLOCK_ROWS, C), lambda b, j: (0, 0, j, b)),
        out_shape=jax.ShapeDtypeStruct((DFT_KHALF, 2, n2, Bd * C), BF16),
        compiler_params=_cparams(("parallel", "parallel")),
        name="dft_a",
    )(xv, f1)


def _dft_c_kernel(a_ref, h_ref, g_ref, gi_ref, asum_ref, o_ref):
    C = HY_WIDTH
    asum = asum_ref[...]
    inv = 1.0 / (asum[:, :C] + asum[:, C:] + 1e-6)
    step = min(DFTC_LANES, a_ref.shape[3])
    for t in range(a_ref.shape[0]):
        g = g_ref[t].astype(BF16)
        gi = gi_ref[t].astype(BF16)
        rows = g.shape[0]
        half = rows // 2
        xh = jnp.dot(g, h_ref[t].reshape(rows, h_ref.shape[3]), preferred_element_type=F32)
        kr = (xh[:half, :C] + xh[:half, C:]) * inv
        ki = (xh[half:, :C] - xh[half:, C:]) * inv
        for l0 in range(0, a_ref.shape[3], step):
            x = jnp.dot(g, a_ref[t, :, :, l0:l0 + step].reshape(rows, step), preferred_element_type=F32)
            ys = []
            for c0 in range(0, step, C):
                xr, xi = x[:half, c0:c0 + C], x[half:, c0:c0 + C]
                ys.append(jnp.concatenate([xr * kr - xi * ki, xr * ki + xi * kr], axis=0).astype(BF16))
            bk = jnp.dot(gi, jnp.concatenate(ys, axis=1), preferred_element_type=F32).astype(BF16)
            o_ref[t, :, :, l0:l0 + step] = bk.reshape(2, half, step)


def _dft_c(a, h, g, ginv, asum):
    _, _, n2, bc = a.shape
    nk = g.shape[0]
    dspec = lambda w: pl.BlockSpec((DFTC_K1, 2, n2, w), lambda k: (k, 0, 0, 0))
    gspec = pl.BlockSpec((DFTC_K1, 2 * n2, 2 * n2), lambda k: (k, 0, 0))
    return pl.pallas_call(
        _dft_c_kernel,
        grid=(nk // DFTC_K1,),
        in_specs=[dspec(bc), dspec(h.shape[3]), gspec, gspec, _const_spec(asum.shape)],
        out_specs=dspec(bc),
        out_shape=jax.ShapeDtypeStruct((nk, 2, n2, bc), BF16),
        compiler_params=_cparams(("parallel",)),
        name="dft_c",
    )(a, h, g, ginv, asum)


def _idft_a_kernel(b_ref, f_ref, vg_ref, x0_ref, bias_ref, o_ref):
    nk1, _, _, C = b_ref.shape
    n1u = vg_ref.shape[1]
    R = DFT_ROWS
    mp = f_ref.shape[1]
    f = f_ref[...]
    for r0 in range(0, b_ref.shape[2], R):
        grp = slice(r0, r0 + R)
        bv = b_ref[:, :, grp, :].reshape(2 * nk1, R, C)
        bv = jnp.concatenate([bv, jnp.zeros((mp - 2 * nk1, R, C), BF16)], axis=0)
        bs = _rows2d(jnp.swapaxes(bv, 0, 1))
        ys = [jnp.dot(f, bs[r * mp:(r + 1) * mp], preferred_element_type=F32).astype(BF16) for r in range(R)]
        y = _rows2d(jnp.swapaxes(jnp.stack(ys, axis=0), 0, 1)).astype(F32)
        y = y + _rows2d(vg_ref[0, :, grp, :]).astype(F32) * bias_ref[...]
        y = y * _rows2d(x0_ref[0, :, grp, :]).astype(F32)
        o_ref[0, :, grp, :] = y.astype(BF16).reshape(n1u, R, C)


def _idft_a(bk, finv, vgv, x0v, bias):
    B, n1u, n2, C = vgv.shape
    nk1 = bk.shape[0]
    dsp = pl.BlockSpec((1, n1u, DFT_BLOCK_ROWS, C), lambda b, j: (b, 0, j, 0))
    return pl.pallas_call(
        _idft_a_kernel,
        grid=(B, n2 // DFT_BLOCK_ROWS),
        in_specs=[pl.BlockSpec((nk1, 2, DFT_BLOCK_ROWS, C), lambda b, j: (0, 0, j, b)),
                  _const_spec(finv.shape), dsp, dsp, _const_spec(bias.shape)],
        out_specs=dsp,
        out_shape=jax.ShapeDtypeStruct((B, n1u, n2, C), BF16),
        compiler_params=_cparams(("parallel", "parallel")),
        name="idft_a",
    )(bk, finv, vgv, x0v, bias)


def _retention_kernel(q_ref, k_ref, v_ref, g_ref, lgf_ref, lgb_ref, s0f_ref, s0b_ref, gn_ref, o_ref,
                      inc_ref, st_ref, dm_ref, wt_ref):
    C = RET_CHUNK
    seq_len = q_ref.shape[1]
    nc = seq_len // C
    lgf = lgf_ref[0]
    lgb = lgb_ref[0]
    lane = lax.broadcasted_iota(jnp.int32, (1, LANES), 1)
    head_of_lane = lane // RET_HEAD_DIM
    bd = _block_diag_mask()

    ri = lax.broadcasted_iota(jnp.int32, (C, LANES), 0).astype(F32)
    wt_ref[0] = jnp.exp(lgf * (ri + 1.0)).astype(BF16)
    wt_ref[1] = jnp.exp(lgb * (C - ri)).astype(BF16)
    wt_ref[2] = jnp.exp(lgf * (C - 1.0 - ri)).astype(BF16)
    wt_ref[3] = jnp.exp(lgb * ri).astype(BF16)
    di = lax.broadcasted_iota(jnp.int32, (C, C), 0)
    dj = lax.broadcasted_iota(jnp.int32, (C, C), 1)
    dd = (di - dj).astype(F32)
    for h in range(HEADS_PER_BLOCK):
        lf = jnp.sum(jnp.where(lane == h * RET_HEAD_DIM, lgf, 0.0), axis=1, keepdims=True)
        lb = jnp.sum(jnp.where(lane == h * RET_HEAD_DIM, lgb, 0.0), axis=1, keepdims=True)
        dm_ref[h] = jnp.where(dd > 0, jnp.exp(lf * dd), jnp.where(dd < 0, jnp.exp(-lb * dd), 2.0)).astype(BF16)

    def chunk(c):
        return pl.ds(pl.multiple_of(c * C, C), C)

    def inc_body(c, carry):
        rows = chunk(c)
        k = k_ref[0, rows, :]
        kw = jnp.concatenate([k * wt_ref[2], k * wt_ref[3]], axis=1)
        t = _kt_v(kw, v_ref[0, rows, :])
        inc_ref[c, :LANES] = jnp.where(bd, t[:LANES], 0.0)
        inc_ref[c, LANES:] = jnp.where(bd, t[LANES:], 0.0)
        return carry

    lax.fori_loop(0, nc, inc_body, 0, unroll=True)

    gcf = jnp.exp(lgf * float(C))
    gcb = jnp.exp(lgb * float(C))

    def fscan(c, s):
        st_ref[c, :LANES] = s.astype(BF16)
        return s * gcf + inc_ref[c, :LANES]

    lax.fori_loop(0, nc, fscan, s0f_ref[0, 0])

    def bscan(i, s):
        c = nc - 1 - i
        st_ref[c, LANES:] = s.astype(BF16)
        return s * gcb + inc_ref[c, LANES:]

    lax.fori_loop(0, nc, bscan, s0b_ref[0, 0])

    gn = gn_ref[...]
    first = head_of_lane == 0
    inv_n = 1.0 / RET_HEAD_DIM

    def head_mean(a):
        tot = jnp.sum(a, axis=1, keepdims=True)
        s0 = jnp.sum(jnp.where(first, a, 0.0), axis=1, keepdims=True)
        return jnp.where(first, s0, tot - s0) * inv_n

    def out_body(c, carry):
        rows = chunk(c)
        q = q_ref[0, rows, :]
        k = k_ref[0, rows, :]
        v = v_ref[0, rows, :]
        qw = jnp.concatenate([q * wt_ref[0], q * wt_ref[1]], axis=1)
        o = jnp.dot(qw, st_ref[c], preferred_element_type=F32)
        oh = []
        for h in range(HEADS_PER_BLOCK):
            qm = jnp.where(head_of_lane == h, q, jnp.zeros_like(q))
            sc = lax.dot_general(qm, k, (((1,), (1,)), ((), ())), preferred_element_type=F32)
            oh.append(jnp.dot(sc.astype(BF16) * dm_ref[h], v, preferred_element_type=F32))
        o = o + jnp.where(first, oh[0], oh[1])
        d = o - head_mean(o)
        y = d * lax.rsqrt(head_mean(d * d) + NORM_EPS) * gn
        o_ref[0, rows, :] = (y * _silu(g_ref[0, rows, :].astype(F32))).astype(BF16)
        return carry

    lax.fori_loop(0, nc, out_body, 0, unroll=True)


def _retention(u, lgf, lgb, s0f, s0b, gn):
    B, L, _ = u.shape
    assert HEADS_PER_BLOCK == 2
    nc = L // RET_CHUNK
    col = lambda g: pl.BlockSpec((1, L, LANES), lambda b, p: (b, 0, g * RET_BLOCKS + p))
    lsp = pl.BlockSpec((1, 1, LANES), lambda b, p: (p, 0, 0))
    ssp = pl.BlockSpec((1, 1, LANES, LANES), lambda b, p: (b, p, 0, 0))
    return pl.pallas_call(
        _retention_kernel,
        grid=(B, RET_BLOCKS),
        in_specs=[col(0), col(1), col(2), col(3), lsp, lsp, ssp, ssp,
                  pl.BlockSpec((1, LANES), lambda b, p: (0, p))],
        out_specs=pl.BlockSpec((1, L, LANES), lambda b, p: (b, 0, p)),
        out_shape=jax.ShapeDtypeStruct((B, L, RET_WIDTH), BF16),
        scratch_shapes=[pltpu.VMEM((nc, 2 * LANES, LANES), F32),
                        pltpu.VMEM((nc, 2 * LANES, LANES), BF16),
                        pltpu.VMEM((HEADS_PER_BLOCK, RET_CHUNK, RET_CHUNK), BF16),
                        pltpu.VMEM((4, RET_CHUNK, LANES), BF16)],
        compiler_params=_cparams(("parallel", "parallel")),
        name="retention",
    )(u, u, u, u, lgf, lgb, s0f, s0b, gn)


def _rope_tables(seq_len):
    rows = seq_len // GRID_W
    r, col = np.meshgrid(np.arange(rows, dtype=np.float64), np.arange(GRID_W, dtype=np.float64), indexing="ij")
    inv = ROPE_BASE ** (-np.arange(ROPE_PAIRS_AXIS, dtype=np.float64) / ROPE_PAIRS_AXIS)
    ang = np.concatenate([r.reshape(-1, 1) * inv, col.reshape(-1, 1) * inv], axis=-1)
    ang = np.tile(np.repeat(ang, 2, axis=1), (1, HEADS_PER_BLOCK))
    sign = np.where(np.arange(LANES) % 2 == 0, -1.0, 1.0)
    return jnp.asarray(np.cos(ang).astype(np.float32)), jnp.asarray((np.sin(ang) * sign).astype(np.float32))


def _out_mlp_kernel(x_ref, yh_ref, yr_ref, wo1_ref, wo2_ref, w1_ref, w2_ref, g2_ref, gf_ref, mod_ref, o_ref,
                    *, ff_chunk):
    mod = mod_ref[0]
    mix = (jnp.dot(yh_ref[0], wo1_ref[...], preferred_element_type=F32)
           + jnp.dot(yr_ref[0], wo2_ref[...], preferred_element_type=F32))
    x1 = x_ref[0] + mod[0:1] * mix
    h = (_rms(x1) * g2_ref[...] * (1.0 + mod[2:3]) + mod[1:2]).astype(BF16)
    acc = jnp.zeros_like(x1)
    for j in range(w1_ref.shape[1] // ff_chunk):
        cols = slice(j * ff_chunk, (j + 1) * ff_chunk)
        a = jnp.maximum(jnp.dot(h, w1_ref[:, cols], preferred_element_type=F32), 0.0)
        acc = acc + jnp.dot((a * a).astype(BF16), w2_ref[cols, :], preferred_element_type=F32)
    x2 = x1 + mod[3:4] * acc
    o_ref[0] = _rms(x2) * gf_ref[...]


def _out_mlp(x, yh, yr, wo1, wo2, w1, w2, g2, gf, mod, tm):
    B, L, D = x.shape
    dff = w1.shape[1]
    row = lambda n: pl.BlockSpec((1, tm, n), lambda b, i: (b, i, 0))
    wsp = lambda a: pl.BlockSpec(a.shape, lambda b, i: (0, 0), pipeline_mode=pl.Buffered(1))
    return pl.pallas_call(
        functools.partial(_out_mlp_kernel, ff_chunk=2048),
        grid=(B, L // tm),
        in_specs=[row(D), row(yh.shape[2]), row(yr.shape[2]), wsp(wo1), wsp(wo2), wsp(w1), wsp(w2),
                  _const_spec((1, D)), _const_spec((1, D)),
                  pl.BlockSpec((1, 4, D), lambda b, i: (b, 0, 0))],
        out_specs=row(D),
        out_shape=jax.ShapeDtypeStruct((B, L, D), F32),
        compiler_params=_cparams(("parallel", "parallel")),
        name="out_mlp",
    )(x, yh, yr, wo1, wo2, w1, w2, g2, gf, mod)


def kernel(x, c, ctx, c_ctx, w_ada, b_ada, norm1_g, w_in, hy_conv_w, hy_conv_b, hy_f_w1, hy_f_b1, hy_f_freq1,
           hy_f_w2, hy_f_b2, hy_f_freq2, hy_f_w3, hy_bias, ret_decay_logit, ret_gn_g, w_out, norm2_g, w_mlp1,
           w_mlp2, norm_f_g):
    B, L, D = x.shape
    depth = w_ada.shape[0]
    assert depth == 1, "single-layer block"
    l = 0
    rows = 16
    cvec = jnp.concatenate([c, c_ctx[None], jnp.zeros((rows - B - 1, D), F32)], axis=0)
    mod = _ada(cvec, w_ada[l], b_ada[l][None])
    mx = mod[:B].reshape(B, 6, D)
    mc = mod[B].reshape(6, D)

    lg = jax.nn.log_sigmoid(ret_decay_logit[l].astype(F32))
    lg_lanes = jnp.repeat(lg, RET_HEAD_DIM, axis=1).reshape(2, RET_BLOCKS, 1, LANES)
    lgf, lgb = lg_lanes[0], lg_lanes[1]

    w_in_b = w_in[l].astype(BF16)
    g1 = norm1_g[l][None]
    vg, x0c, u_ret = _inproj(x, g1, mx[:, 0:1], mx[:, 1:2], w_in_b, hy_conv_w[l].astype(F32),
                             hy_conv_b[l][None].astype(F32), _rope_tables(L), tm=1024)
    kc0 = HY_PROJ + RET_WIDTH
    s0f, s0b = _ctx_states(ctx, g1, mc[0][None, None], mc[1][None, None], w_in_b[:, kc0:kc0 + RET_WIDTH],
                           w_in_b[:, kc0 + RET_WIDTH:kc0 + 2 * RET_WIDTH], lgf, lgb)

    f1, finv, gmat, ginv = _dft_tables(L)
    f1, finv = f1.astype(BF16), finv.astype(BF16)
    taps, asum = _filters(L, hy_f_w1[l], hy_f_b1[l], hy_f_freq1[l], hy_f_w2[l], hy_f_b2[l], hy_f_freq2[l],
                          hy_f_w3[l])
    tview = lambda a: a.reshape(a.shape[0], L // DFT_N2, DFT_N2, HY_WIDTH)
    bk = _dft_c(_dft_a(tview(vg), f1), _dft_a(tview(taps), f1), gmat, ginv, asum)
    y_hy = _idft_a(bk, finv, tview(vg), tview(x0c), hy_bias[l][None].astype(F32)).reshape(B, L, HY_WIDTH)

    y_ret = _retention(u_ret, lgf, lgb, s0f, s0b, ret_gn_g[l][None].astype(F32))

    w_out_b = w_out[l].astype(BF16)
    mod2 = mx[:, 2:6]
    return _out_mlp(x, y_hy, y_ret, w_out_b[:HY_WIDTH], w_out_b[HY_WIDTH:], w_mlp1[l].astype(BF16),
                    w_mlp2[l].astype(BF16), norm2_g[l][None], norm_f_g[None], mod2, tm=1024)
```

```python
import functools
import math

import numpy as np
import jax
import jax.numpy as jnp
from jax import lax
from jax.experimental import pallas as pl
from jax.experimental.pallas import tpu as pltpu

F32 = jnp.float32
BF16 = jnp.bfloat16
HIGHEST = lax.Precision.HIGHEST

GRID_W = 64
HY_WIDTH = 512
RET_WIDTH = 512
RET_HEADS = 8
RET_HEAD_DIM = RET_WIDTH // RET_HEADS
HY_PROJ = 3 * HY_WIDTH
HY_BANDS = 16
HY_FILT_HID = 64
HY_DECAY_TARGET = 1e-2
HY_FAST_PCT = 0.3
HY_SLOW_PCT = 1.5
ROPE_BASE = 10000.0
ROPE_PAIRS_AXIS = RET_HEAD_DIM // 4
NORM_EPS = 1e-6

LANES = 128
HEADS_PER_BLOCK = LANES // RET_HEAD_DIM
RET_BLOCKS = RET_WIDTH // LANES
RET_CHUNK = 256
DFT_N2 = 128
DFT_ROWS = 16
DFT_BLOCK_ROWS = 64
DFT_KHALF = 72
DFTC_LANES = 2048
DFTC_K1 = 2
VMEM_LIMIT = 56 * 1024 * 1024


def _cparams(sem):
    return pltpu.CompilerParams(dimension_semantics=sem, vmem_limit_bytes=VMEM_LIMIT)


def _const_spec(shape):
    nd = len(shape)
    return pl.BlockSpec(shape, lambda *_: (0,) * nd)


def _silu(x):
    return x * jax.nn.sigmoid(x)


def _rms(x):
    return x * lax.rsqrt(jnp.mean(x * x, axis=-1, keepdims=True) + NORM_EPS)


def _ada_kernel(c_ref, w_ref, b_ref, o_ref):
    s = _silu(c_ref[...])
    o_ref[...] = jnp.dot(s, w_ref[...], preferred_element_type=F32, precision=HIGHEST) + b_ref[...]


def _ada(cvec, w, b):
    R, D = cvec.shape
    N = w.shape[1]
    return pl.pallas_call(
        _ada_kernel,
        grid=(N // D,),
        in_specs=[_const_spec((R, D)), pl.BlockSpec((D, D), lambda j: (0, j)),
                  pl.BlockSpec((1, D), lambda j: (0, j))],
        out_specs=pl.BlockSpec((R, D), lambda j: (0, j)),
        out_shape=jax.ShapeDtypeStruct((R, N), F32),
        compiler_params=_cparams(("arbitrary",)),
        name="ada",
    )(cvec, w, b)


INPROJ_ROWS = 1024
INPROJ_NCHUNK = 1024
HALO = 16


def _norm_mod(x, g_ref, sh_ref, sc_ref):
    h = _rms(x) * g_ref[...]
    return (h * (1.0 + sc_ref[0]) + sh_ref[0]).astype(BF16)


def _inproj_kernel(x_ref, xp_ref, xn_ref, g_ref, sh_ref, sc_ref, w_ref, cw_ref, cb_ref, cos_ref, sin_ref,
                   vg_ref, x0_ref, ur_ref):
    i = pl.program_id(1)
    tm = x_ref.shape[1]
    ext = tm + 2 * HALO
    hb = _norm_mod(jnp.concatenate([xp_ref[0, 0], x_ref[0], xn_ref[0, 0]], axis=0), g_ref, sh_ref, sc_ref)
    row = lax.broadcasted_iota(jnp.int32, (ext, 1), 0)
    outside = ((row < HALO) & (i == 0)) | ((row >= tm + HALO) & (i == pl.num_programs(1) - 1))
    hb = jnp.where(outside, jnp.zeros_like(hb), hb)
    conv = []
    for c0 in range(0, HY_PROJ, HY_WIDTH):
        cols = slice(c0, c0 + HY_WIDTH)
        r = jnp.dot(hb, w_ref[:, cols], preferred_element_type=F32)
        cw = cw_ref[:, cols]
        y = pltpu.roll(r, 1, 0) * cw[0:1] + r * cw[1:2] + pltpu.roll(r, ext - 1, 0) * cw[2:3] + cb_ref[:, cols]
        conv.append(y[HALO:HALO + tm])
    x0_ref[0] = conv[0].astype(BF16)
    vg_ref[0] = (conv[2] * conv[1]).astype(BF16)

    hm = hb[HALO:HALO + tm]
    even = (lax.broadcasted_iota(jnp.int32, (1, LANES), 1) % 2) == 0
    for c0 in range(HY_PROJ, w_ref.shape[1], INPROJ_NCHUNK):
        r = jnp.dot(hm, w_ref[:, c0:c0 + INPROJ_NCHUNK], preferred_element_type=F32)
        parts = []
        for b0 in range(0, INPROJ_NCHUNK, LANES):
            t = r[:, b0:b0 + LANES]
            if c0 + b0 < HY_PROJ + 2 * RET_WIDTH:
                sw = jnp.where(even, pltpu.roll(t, LANES - 1, 1), pltpu.roll(t, 1, 1))
                t = t * cos_ref[...] + sw * sin_ref[...]
                if c0 + b0 >= HY_PROJ + RET_WIDTH:
                    t = t * (RET_HEAD_DIM ** -0.5)
            parts.append(t)
        ur_ref[0, :, c0 - HY_PROJ:c0 - HY_PROJ + INPROJ_NCHUNK] = jnp.concatenate(parts, axis=1).astype(BF16)


def _inproj(x, g, shift, scale, w, conv_w, conv_b, rope, tm):
    B, L, D = x.shape
    N = w.shape[1]
    nh = tm // HALO
    nrow = L // HALO
    xh = x.reshape(B, nrow, HALO, D)
    vec = pl.BlockSpec((1, 1, D), lambda b, i: (b, 0, 0))
    row = lambda n: pl.BlockSpec((1, tm, n), lambda b, i: (b, i, 0))
    tab = pl.BlockSpec((tm, LANES), lambda b, i: (i, 0))
    return pl.pallas_call(
        _inproj_kernel,
        grid=(B, L // tm),
        in_specs=[row(D),
                  pl.BlockSpec((1, 1, HALO, D), lambda b, i: (b, jnp.maximum(i * nh - 1, 0), 0, 0)),
                  pl.BlockSpec((1, 1, HALO, D), lambda b, i: (b, jnp.minimum((i + 1) * nh, nrow - 1), 0, 0)),
                  _const_spec((1, D)), vec, vec,
                  pl.BlockSpec((D, N), lambda b, i: (0, 0), pipeline_mode=pl.Buffered(1)),
                  _const_spec(conv_w.shape), _const_spec(conv_b.shape), tab, tab],
        out_specs=[row(HY_WIDTH), row(HY_WIDTH), row(N - HY_PROJ)],
        out_shape=[jax.ShapeDtypeStruct((B, L, HY_WIDTH), BF16), jax.ShapeDtypeStruct((B, L, HY_WIDTH), BF16),
                   jax.ShapeDtypeStruct((B, L, N - HY_PROJ), BF16)],
        compiler_params=_cparams(("parallel", "parallel")),
        name="inproj",
    )(x, xh, xh, g, shift, scale, w, conv_w, conv_b, *rope)


def _block_diag_mask():
    r = lax.broadcasted_iota(jnp.int32, (LANES, LANES), 0) // RET_HEAD_DIM
    c = lax.broadcasted_iota(jnp.int32, (LANES, LANES), 1) // RET_HEAD_DIM
    return r == c


def _kt_v(k, v):
    return lax.dot_general(k, v, (((0,), (0,)), ((), ())), preferred_element_type=F32)


def _ctx_state_kernel(x_ref, g_ref, sh_ref, sc_ref, wk_ref, wv_ref, lgf_ref, lgb_ref, sf_ref, sb_ref):
    hb = _norm_mod(x_ref[0], g_ref, sh_ref, sc_ref)
    k = jnp.dot(hb, wk_ref[...], preferred_element_type=F32) * (RET_HEAD_DIM ** -0.5)
    v = jnp.dot(hb, wv_ref[...], preferred_element_type=F32).astype(BF16)
    lc = k.shape[0]
    m = lax.broadcasted_iota(jnp.int32, (lc, LANES), 0).astype(F32)
    bd = _block_diag_mask()
    for p in range(RET_BLOCKS):
        lanes = slice(p * LANES, (p + 1) * LANES)
        wf = jnp.exp(lgf_ref[p] * (lc - 1.0 - m))
        wb = jnp.exp(lgb_ref[p] * m)
        sf_ref[0, p] = jnp.where(bd, _kt_v((k[:, lanes] * wf).astype(BF16), v[:, lanes]), 0.0)
        sb_ref[0, p] = jnp.where(bd, _kt_v((k[:, lanes] * wb).astype(BF16), v[:, lanes]), 0.0)


def _ctx_states(ctx, g, shift, scale, wk, wv, lgf, lgb):
    B, lc, D = ctx.shape
    st = jax.ShapeDtypeStruct((B, RET_BLOCKS, LANES, LANES), F32)
    sspec = pl.BlockSpec((1, RET_BLOCKS, LANES, LANES), lambda b: (b, 0, 0, 0))
    return pl.pallas_call(
        _ctx_state_kernel,
        grid=(B,),
        in_specs=[pl.BlockSpec((1, lc, D), lambda b: (b, 0, 0)), _const_spec((1, D)),
                  _const_spec(shift.shape), _const_spec(scale.shape), _const_spec(wk.shape), _const_spec(wv.shape),
                  _const_spec(lgf.shape), _const_spec(lgb.shape)],
        out_specs=[sspec, sspec],
        out_shape=[st, st],
        compiler_params=_cparams(("parallel",)),
        name="ctx_state",
    )(ctx, g, shift, scale, wk, wv, lgf, lgb)


FILT_ROWS = 1024
FILT_GROUPS = LANES // HY_BANDS
FILT_HALVES = LANES // HY_FILT_HID


def _filt_kernel(bands_ref, w1t_ref, w1c_ref, w1s_ref, b1_ref, fr1_ref, w2_ref, b2_ref, fr2_ref,
                 w3_ref, dl_ref, h_ref, asum_ref, *, seq_len):
    i = pl.program_id(0)
    tl = FILT_ROWS
    gr = tl // FILT_GROUPS
    hr = tl // FILT_HALVES
    lane = lax.broadcasted_iota(jnp.int32, (1, LANES), 1)
    base = i * tl
    dot = functools.partial(jnp.dot, preferred_element_type=F32, precision=HIGHEST)

    n8 = (lax.broadcasted_iota(jnp.int32, (gr, 1), 0) + (lane // HY_BANDS) * gr + base).astype(F32)
    arg = bands_ref[...] * ((2.0 * math.pi / seq_len) * n8)
    cz, sz = jnp.cos(arg), jnp.sin(arg)
    nh = (lax.broadcasted_iota(jnp.int32, (hr, 1), 0) + (lane // HY_FILT_HID) * hr + base).astype(F32)
    blocks = [dot(cz, w1c_ref[r]) - dot(sz, w1s_ref[r]) for r in range(hr // gr)]
    pre = jnp.concatenate(blocks, axis=0) + (nh / (seq_len - 1.0)) * w1t_ref[...] + b1_ref[...]
    h = jnp.sin(fr1_ref[...] * pre)
    h = jnp.sin(fr2_ref[...] * (dot(h, w2_ref[...]) + b2_ref[...])).astype(BF16)

    @pl.when(i == 0)
    def _():
        asum_ref[...] = jnp.zeros_like(asum_ref)

    for p in range(FILT_HALVES):
        n = (lax.broadcasted_iota(jnp.int32, (hr, 1), 0) + p * hr + base).astype(F32)
        t = n / (seq_len - 1.0)
        taps = jnp.dot(h, w3_ref[p], preferred_element_type=F32) * jnp.exp(-t * dl_ref[...])
        asum_ref[...] += jnp.sum(jnp.abs(taps), axis=0, keepdims=True)
        rows = slice(p * hr, (p + 1) * hr)
        h_ref[0, rows] = taps[:, :HY_WIDTH].astype(BF16)
        h_ref[1, rows] = jnp.where(n == 0.0, 0.0, taps[:, HY_WIDTH:]).astype(BF16)


def _filters(seq_len, w1, b1, fr1, w2, b2, fr2, w3):
    hid = HY_FILT_HID
    f32 = lambda a: a.astype(F32)
    bands = np.tile(np.linspace(1e-4, HY_BANDS - 1, HY_BANDS, dtype=np.float32), FILT_GROUPS)[None]
    deltas = np.abs(np.linspace(math.log(HY_DECAY_TARGET) / HY_SLOW_PCT,
                                math.log(HY_DECAY_TARGET) / HY_FAST_PCT, HY_WIDTH, dtype=np.float32))
    dl = np.tile(deltas, 2)[None, :]
    halves = lambda v: jnp.tile(f32(v)[None], (1, FILT_HALVES))

    nb = FILT_GROUPS // FILT_HALVES
    sel = np.zeros((nb, FILT_GROUPS, FILT_HALVES), np.float32)
    for r in range(nb):
        for p in range(FILT_HALVES):
            sel[r, p * nb + r, p] = 1.0
    eye = np.eye(FILT_HALVES, dtype=np.float32)

    def place_l1(wb):
        return (sel[:, :, None, :, None] * f32(wb)[None, None, :, None, :]).reshape(nb, LANES, LANES)

    w2d = (eye[:, None, :, None] * f32(w2)[None, :, None, :]).reshape(LANES, LANES)
    w3p = (eye[:, :, None, None] * f32(w3)[None, None]).reshape(FILT_HALVES, LANES, 2 * HY_WIDTH)
    args = (jnp.asarray(bands), halves(w1[0]), place_l1(w1[1:1 + HY_BANDS]), place_l1(w1[1 + HY_BANDS:]),
            halves(b1), halves(fr1), w2d, halves(b2), halves(fr2), w3p.astype(BF16), jnp.asarray(dl))
    return pl.pallas_call(
        functools.partial(_filt_kernel, seq_len=seq_len),
        grid=(seq_len // FILT_ROWS,),
        in_specs=[_const_spec(a.shape) for a in args],
        out_specs=[pl.BlockSpec((2, FILT_ROWS, HY_WIDTH), lambda i: (0, i, 0)),
                   _const_spec((1, 2 * HY_WIDTH))],
        out_shape=[jax.ShapeDtypeStruct((2, seq_len, HY_WIDTH), BF16),
                   jax.ShapeDtypeStruct((1, 2 * HY_WIDTH), F32)],
        compiler_params=_cparams(("arbitrary",)),
        name="filt",
    )(*args)


def _dft_tables(seq_len):
    n = 2 * seq_len
    n1_full = n // DFT_N2
    n1_used = n1_full // 2
    nk1 = n1_full // 2 + 1
    k1 = np.arange(nk1)[:, None]
    n1 = np.arange(n1_used)[None, :]
    th = 2.0 * np.pi * ((k1 * n1) % n1_full) / n1_full
    ck = np.full((nk1, 1), 2.0)
    ck[0, 0] = 1.0
    ck[-1, 0] = 1.0
    assert nk1 <= DFT_KHALF
    f1 = np.zeros((2 * DFT_KHALF, n1_used))
    f1[:nk1] = np.cos(th)
    f1[DFT_KHALF:DFT_KHALF + nk1] = -np.sin(th)
    finv = np.zeros((n1_used, 2 * DFT_KHALF))
    finv[:, 0:2 * nk1:2] = (ck * np.cos(th) / n).T
    finv[:, 1:2 * nk1:2] = (-ck * np.sin(th) / n).T
    k1 = np.arange(nk1)[:, None, None]
    k2 = np.arange(DFT_N2)[None, :, None]
    n2 = np.arange(DFT_N2)[None, None, :]
    ph = 2.0 * np.pi * ((n1_full * n2 * k2 + n2 * k1) % n) / n
    gr, gi = np.cos(ph), -np.sin(ph)
    g = np.concatenate([np.concatenate([gr, -gi], 2), np.concatenate([gi, gr], 2)], 1)
    grt, git = np.swapaxes(gr, 1, 2), -np.swapaxes(gi, 1, 2)
    ginv = np.concatenate([np.concatenate([grt, -git], 2), np.concatenate([git, grt], 2)], 1)
    npad = -nk1 % DFTC_K1
    assert nk1 + npad <= DFT_KHALF
    g, ginv = (np.concatenate([m, np.zeros((npad,) + m.shape[1:])], 0) for m in (g, ginv))
    f = lambda a: jnp.asarray(a.astype(np.float32))
    return f(f1), f(finv), f(g), f(ginv)


def _rows2d(a):
    return a.reshape(a.shape[0] * a.shape[1], a.shape[2])


def _dft_a_kernel(x_ref, f_ref, o_ref):
    n1u = x_ref.shape[1]
    f = f_ref[...]
    for r0 in range(0, x_ref.shape[2], DFT_ROWS):
        grp = slice(r0, r0 + DFT_ROWS)
        xs = _rows2d(jnp.swapaxes(x_ref[0, :, grp, :], 0, 1))
        res = [jnp.dot(f, xs[r * n1u:(r + 1) * n1u], preferred_element_type=F32).astype(BF16)
               for r in range(DFT_ROWS)]
        a = jnp.swapaxes(jnp.stack(res, axis=0), 0, 1)
        o_ref[:, 0, grp, :] = a[:DFT_KHALF]
        o_ref[:, 1, grp, :] = a[DFT_KHALF:]


def _dft_a(xv, f1):
    Bd, n1u, n2, C = xv.shape
    return pl.pallas_call(
        _dft_a_kernel,
        grid=(Bd, n2 // DFT_BLOCK_ROWS),
        in_specs=[pl.BlockSpec((1, n1u, DFT_BLOCK_ROWS, C), lambda b, j: (b, 0, j, 0)), _const_spec(f1.shape)],
        out_specs=pl.BlockSpec((DFT_KHALF, 2, DFT_BLOCK_ROWS, C), lambda b, j: (0, 0, j, b)),
        out_shape=jax.ShapeDtypeStruct((DFT_KHALF, 2, n2, Bd * C), BF16),
        compiler_params=_cparams(("parallel", "parallel")),
        name="dft_a",
    )(xv, f1)


def _dft_c_kernel(a_ref, h_ref, g_ref, gi_ref, asum_ref, o_ref):
    C = HY_WIDTH
    asum = asum_ref[...]
    inv = 1.0 / (asum[:, :C] + asum[:, C:] + 1e-6)
    step = min(DFTC_LANES, a_ref.shape[3])
    for t in range(a_ref.shape[0]):
        g = g_ref[t].astype(BF16)
        gi = gi_ref[t].astype(BF16)
        rows = g.shape[0]
        half = rows // 2
        xh = jnp.dot(g, h_ref[t].reshape(rows, h_ref.shape[3]), preferred_element_type=F32)
        kr = (xh[:half, :C] + xh[:half, C:]) * inv
        ki = (xh[half:, :C] - xh[half:, C:]) * inv
        for l0 in range(0, a_ref.shape[3], step):
            x = jnp.dot(g, a_ref[t, :, :, l0:l0 + step].reshape(rows, step), preferred_element_type=F32)
            ys = []
            for c0 in range(0, step, C):
                xr, xi = x[:half, c0:c0 + C], x[half:, c0:c0 + C]
                ys.append(jnp.concatenate([xr * kr - xi * ki, xr * ki + xi * kr], axis=0).astype(BF16))
            bk = jnp.dot(gi, jnp.concatenate(ys, axis=1), preferred_element_type=F32).astype(BF16)
            o_ref[t, :, :, l0:l0 + step] = bk.reshape(2, half, step)


def _dft_c(a, h, g, ginv, asum):
    _, _, n2, bc = a.shape
    nk = g.shape[0]
    dspec = lambda w: pl.BlockSpec((DFTC_K1, 2, n2, w), lambda k: (k, 0, 0, 0))
    gspec = pl.BlockSpec((DFTC_K1, 2 * n2, 2 * n2), lambda k: (k, 0, 0))
    return pl.pallas_call(
        _dft_c_kernel,
        grid=(nk // DFTC_K1,),
        in_specs=[dspec(bc), dspec(h.shape[3]), gspec, gspec, _const_spec(asum.shape)],
        out_specs=dspec(bc),
        out_shape=jax.ShapeDtypeStruct((nk, 2, n2, bc), BF16),
        compiler_params=_cparams(("parallel",)),
        name="dft_c",
    )(a, h, g, ginv, asum)


def _idft_a_kernel(b_ref, f_ref, vg_ref, x0_ref, bias_ref, o_ref):
    nk1, _, _, C = b_ref.shape
    n1u = vg_ref.shape[1]
    R = DFT_ROWS
    mp = f_ref.shape[1]
    f = f_ref[...]
    for r0 in range(0, b_ref.shape[2], R):
        grp = slice(r0, r0 + R)
        bv = b_ref[:, :, grp, :].reshape(2 * nk1, R, C)
        bv = jnp.concatenate([bv, jnp.zeros((mp - 2 * nk1, R, C), BF16)], axis=0)
        bs = _rows2d(jnp.swapaxes(bv, 0, 1))
        ys = [jnp.dot(f, bs[r * mp:(r + 1) * mp], preferred_element_type=F32).astype(BF16) for r in range(R)]
        y = _rows2d(jnp.swapaxes(jnp.stack(ys, axis=0), 0, 1)).astype(F32)
        y = y + _rows2d(vg_ref[0, :, grp, :]).astype(F32) * bias_ref[...]
        y = y * _rows2d(x0_ref[0, :, grp, :]).astype(F32)
        o_ref[0, :, grp, :] = y.astype(BF16).reshape(n1u, R, C)


def _idft_a(bk, finv, vgv, x0v, bias):
    B, n1u, n2, C = vgv.shape
    nk1 = bk.shape[0]
    dsp = pl.BlockSpec((1, n1u, DFT_BLOCK_ROWS, C), lambda b, j: (b, 0, j, 0))
    return pl.pallas_call(
        _idft_a_kernel,
        grid=(B, n2 // DFT_BLOCK_ROWS),
        in_specs=[pl.BlockSpec((nk1, 2, DFT_BLOCK_ROWS, C), lambda b, j: (0, 0, j, b)),
                  _const_spec(finv.shape), dsp, dsp, _const_spec(bias.shape)],
        out_specs=dsp,
        out_shape=jax.ShapeDtypeStruct((B, n1u, n2, C), BF16),
        compiler_params=_cparams(("parallel", "parallel")),
        name="idft_a",
    )(bk, finv, vgv, x0v, bias)


def _retention_kernel(q_ref, k_ref, v_ref, g_ref, lgf_ref, lgb_ref, s0f_ref, s0b_ref, gn_ref, o_ref,
                      inc_ref, st_ref, dm_ref, wt_ref):
    C = RET_CHUNK
    seq_len = q_ref.shape[1]
    nc = seq_len // C
    lgf = lgf_ref[0]
    lgb = lgb_ref[0]
    lane = lax.broadcasted_iota(jnp.int32, (1, LANES), 1)
    head_of_lane = lane // RET_HEAD_DIM
    bd = _block_diag_mask()

    ri = lax.broadcasted_iota(jnp.int32, (C, LANES), 0).astype(F32)
    wt_ref[0] = jnp.exp(lgf * (ri + 1.0)).astype(BF16)
    wt_ref[1] = jnp.exp(lgb * (C - ri)).astype(BF16)
    wt_ref[2] = jnp.exp(lgf * (C - 1.0 - ri)).astype(BF16)
    wt_ref[3] = jnp.exp(lgb * ri).astype(BF16)
    di = lax.broadcasted_iota(jnp.int32, (C, C), 0)
    dj = lax.broadcasted_iota(jnp.int32, (C, C), 1)
    dd = (di - dj).astype(F32)
    for h in range(HEADS_PER_BLOCK):
        lf = jnp.sum(jnp.where(lane == h * RET_HEAD_DIM, lgf, 0.0), axis=1, keepdims=True)
        lb = jnp.sum(jnp.where(lane == h * RET_HEAD_DIM, lgb, 0.0), axis=1, keepdims=True)
        dm_ref[h] = jnp.where(dd > 0, jnp.exp(lf * dd), jnp.where(dd < 0, jnp.exp(-lb * dd), 2.0)).astype(BF16)

    def chunk(c):
        return pl.ds(pl.multiple_of(c * C, C), C)

    def inc_body(c, carry):
        rows = chunk(c)
        k = k_ref[0, rows, :]
        kw = jnp.concatenate([k * wt_ref[2], k * wt_ref[3]], axis=1)
        t = _kt_v(kw, v_ref[0, rows, :])
        inc_ref[c, :LANES] = jnp.where(bd, t[:LANES], 0.0)
        inc_ref[c, LANES:] = jnp.where(bd, t[LANES:], 0.0)
        return carry

    lax.fori_loop(0, nc, inc_body, 0, unroll=True)

    gcf = jnp.exp(lgf * float(C))
    gcb = jnp.exp(lgb * float(C))

    def fscan(c, s):
        st_ref[c, :LANES] = s.astype(BF16)
        return s * gcf + inc_ref[c, :LANES]

    lax.fori_loop(0, nc, fscan, s0f_ref[0, 0])

    def bscan(i, s):
        c = nc - 1 - i
        st_ref[c, LANES:] = s.astype(BF16)
        return s * gcb + inc_ref[c, LANES:]

    lax.fori_loop(0, nc, bscan, s0b_ref[0, 0])

    gn = gn_ref[...]
    first = head_of_lane == 0
    inv_n = 1.0 / RET_HEAD_DIM

    def head_mean(a):
        tot = jnp.sum(a, axis=1, keepdims=True)
        s0 = jnp.sum(jnp.where(first, a, 0.0), axis=1, keepdims=True)
        return jnp.where(first, s0, tot - s0) * inv_n

    def out_body(c, carry):
        rows = chunk(c)
        q = q_ref[0, rows, :]
        k = k_ref[0, rows, :]
        v = v_ref[0, rows, :]
        qw = jnp.concatenate([q * wt_ref[0], q * wt_ref[1]], axis=1)
        o = jnp.dot(qw, st_ref[c], preferred_element_type=F32)
        oh = []
        for h in range(HEADS_PER_BLOCK):
            qm = jnp.where(head_of_lane == h, q, jnp.zeros_like(q))
            sc = lax.dot_general(qm, k, (((1,), (1,)), ((), ())), preferred_element_type=F32)
            oh.append(jnp.dot(sc.astype(BF16) * dm_ref[h], v, preferred_element_type=F32))
        o = o + jnp.where(first, oh[0], oh[1])
        d = o - head_mean(o)
        y = d * lax.rsqrt(head_mean(d * d) + NORM_EPS) * gn
        o_ref[0, rows, :] = (y * _silu(g_ref[0, rows, :].astype(F32))).astype(BF16)
        return carry

    lax.fori_loop(0, nc, out_body, 0, unroll=True)


def _retention(u, lgf, lgb, s0f, s0b, gn):
    B, L, _ = u.shape
    assert HEADS_PER_BLOCK == 2
    nc = L // RET_CHUNK
    col = lambda g: pl.BlockSpec((1, L, LANES), lambda b, p: (b, 0, g * RET_BLOCKS + p))
    lsp = pl.BlockSpec((1, 1, LANES), lambda b, p: (p, 0, 0))
    ssp = pl.BlockSpec((1, 1, LANES, LANES), lambda b, p: (b, p, 0, 0))
    return pl.pallas_call(
        _retention_kernel,
        grid=(B, RET_BLOCKS),
        in_specs=[col(0), col(1), col(2), col(3), lsp, lsp, ssp, ssp,
                  pl.BlockSpec((1, LANES), lambda b, p: (0, p))],
        out_specs=pl.BlockSpec((1, L, LANES), lambda b, p: (b, 0, p)),
        out_shape=jax.ShapeDtypeStruct((B, L, RET_WIDTH), BF16),
        scratch_shapes=[pltpu.VMEM((nc, 2 * LANES, LANES), F32),
                        pltpu.VMEM((nc, 2 * LANES, LANES), BF16),
                        pltpu.VMEM((HEADS_PER_BLOCK, RET_CHUNK, RET_CHUNK), BF16),
                        pltpu.VMEM((4, RET_CHUNK, LANES), BF16)],
        compiler_params=_cparams(("parallel", "parallel")),
        name="retention",
    )(u, u, u, u, lgf, lgb, s0f, s0b, gn)


def _rope_tables(seq_len):
    rows = seq_len // GRID_W
    r, col = np.meshgrid(np.arange(rows, dtype=np.float64), np.arange(GRID_W, dtype=np.float64), indexing="ij")
    inv = ROPE_BASE ** (-np.arange(ROPE_PAIRS_AXIS, dtype=np.float64) / ROPE_PAIRS_AXIS)
    ang = np.concatenate([r.reshape(-1, 1) * inv, col.reshape(-1, 1) * inv], axis=-1)
    ang = np.tile(np.repeat(ang, 2, axis=1), (1, HEADS_PER_BLOCK))
    sign = np.where(np.arange(LANES) % 2 == 0, -1.0, 1.0)
    return jnp.asarray(np.cos(ang).astype(np.float32)), jnp.asarray((np.sin(ang) * sign).astype(np.float32))


OUT_ROWS = 1024
OUT_FF_CHUNK = 2048


def _out_mlp_kernel(x_ref, yh_ref, yr_ref, wo1_ref, wo2_ref, w1_ref, w2_ref, g2_ref, gf_ref, mod_ref, o_ref,
                    *, ff_chunk):
    mod = mod_ref[0]
    mix = (jnp.dot(yh_ref[0], wo1_ref[...], preferred_element_type=F32)
           + jnp.dot(yr_ref[0], wo2_ref[...], preferred_element_type=F32))
    x1 = x_ref[0] + mod[0:1] * mix
    h = (_rms(x1) * g2_ref[...] * (1.0 + mod[2:3]) + mod[1:2]).astype(BF16)
    acc = jnp.zeros_like(x1)
    for j in range(w1_ref.shape[1] // ff_chunk):
        cols = slice(j * ff_chunk, (j + 1) * ff_chunk)
        a = jnp.maximum(jnp.dot(h, w1_ref[:, cols], preferred_element_type=F32), 0.0)
        acc = acc + jnp.dot((a * a).astype(BF16), w2_ref[cols, :], preferred_element_type=F32)
    x2 = x1 + mod[3:4] * acc
    o_ref[0] = _rms(x2) * gf_ref[...]


def _out_mlp(x, yh, yr, wo1, wo2, w1, w2, g2, gf, mod, tm):
    B, L, D = x.shape
    dff = w1.shape[1]
    row = lambda n: pl.BlockSpec((1, tm, n), lambda b, i: (b, i, 0))
    wsp = lambda a: pl.BlockSpec(a.shape, lambda b, i: (0, 0), pipeline_mode=pl.Buffered(1))
    return pl.pallas_call(
        functools.partial(_out_mlp_kernel, ff_chunk=OUT_FF_CHUNK),
        grid=(B, L // tm),
        in_specs=[row(D), row(yh.shape[2]), row(yr.shape[2]), wsp(wo1), wsp(wo2), wsp(w1), wsp(w2),
                  _const_spec((1, D)), _const_spec((1, D)),
                  pl.BlockSpec((1, 4, D), lambda b, i: (b, 0, 0))],
        out_specs=row(D),
        out_shape=jax.ShapeDtypeStruct((B, L, D), F32),
        compiler_params=_cparams(("parallel", "parallel")),
        name="out_mlp",
    )(x, yh, yr, wo1, wo2, w1, w2, g2, gf, mod)


def kernel(x, c, ctx, c_ctx, w_ada, b_ada, norm1_g, w_in, hy_conv_w, hy_conv_b, hy_f_w1, hy_f_b1, hy_f_freq1,
           hy_f_w2, hy_f_b2, hy_f_freq2, hy_f_w3, hy_bias, ret_decay_logit, ret_gn_g, w_out, norm2_g, w_mlp1,
           w_mlp2, norm_f_g):
    B, L, D = x.shape
    depth = w_ada.shape[0]
    assert depth == 1, "single-layer block"
    l = 0
    rows = -(-(B + 1) // 8) * 8
    cvec = jnp.concatenate([c, c_ctx[None], jnp.zeros((rows - B - 1, D), F32)], axis=0)
    mod = _ada(cvec, w_ada[l], b_ada[l][None])
    mx = mod[:B].reshape(B, 6, D)
    mc = mod[B].reshape(6, D)

    lg = jax.nn.log_sigmoid(ret_decay_logit[l].astype(F32))
    lg_lanes = jnp.repeat(lg, RET_HEAD_DIM, axis=1).reshape(2, RET_BLOCKS, 1, LANES)
    lgf, lgb = lg_lanes[0], lg_lanes[1]

    w_in_b = w_in[l].astype(BF16)
    g1 = norm1_g[l][None]
    vg, x0c, u_ret = _inproj(x, g1, mx[:, 0:1], mx[:, 1:2], w_in_b, hy_conv_w[l].astype(F32),
                             hy_conv_b[l][None].astype(F32), _rope_tables(L), tm=INPROJ_ROWS)
    kc0 = HY_PROJ + RET_WIDTH
    s0f, s0b = _ctx_states(ctx, g1, mc[0][None, None], mc[1][None, None], w_in_b[:, kc0:kc0 + RET_WIDTH],
                           w_in_b[:, kc0 + RET_WIDTH:kc0 + 2 * RET_WIDTH], lgf, lgb)

    f1, finv, gmat, ginv = _dft_tables(L)
    f1, finv = f1.astype(BF16), finv.astype(BF16)
    taps, asum = _filters(L, hy_f_w1[l], hy_f_b1[l], hy_f_freq1[l], hy_f_w2[l], hy_f_b2[l], hy_f_freq2[l],
                          hy_f_w3[l])
    tview = lambda a: a.reshape(a.shape[0], L // DFT_N2, DFT_N2, HY_WIDTH)
    bk = _dft_c(_dft_a(tview(vg), f1), _dft_a(tview(taps), f1), gmat, ginv, asum)
    y_hy = _idft_a(bk, finv, tview(vg), tview(x0c), hy_bias[l][None].astype(F32)).reshape(B, L, HY_WIDTH)

    y_ret = _retention(u_ret, lgf, lgb, s0f, s0b, ret_gn_g[l][None].astype(F32))

    w_out_b = w_out[l].astype(BF16)
    mod2 = mx[:, 2:6]
    return _out_mlp(x, y_hy, y_ret, w_out_b[:HY_WIDTH], w_out_b[HY_WIDTH:], w_mlp1[l].astype(BF16),
                    w_mlp2[l].astype(BF16), norm2_g[l][None], norm_f_g[None], mod2, tm=OUT_ROWS)
```

```python
import functools
import math

import numpy as np
import jax
import jax.numpy as jnp
from jax import lax
from jax.experimental import pallas as pl
from jax.experimental.pallas import tpu as pltpu

F32 = jnp.float32
BF16 = jnp.bfloat16
HIGHEST = lax.Precision.HIGHEST

GRID_W = 64
HY_WIDTH = 512
RET_WIDTH = 512
RET_HEADS = 8
RET_HEAD_DIM = RET_WIDTH // RET_HEADS
HY_PROJ = 3 * HY_WIDTH
HY_BANDS = 16
HY_FILT_HID = 64
HY_DECAY_TARGET = 1e-2
HY_FAST_PCT = 0.3
HY_SLOW_PCT = 1.5
ROPE_BASE = 10000.0
ROPE_PAIRS_AXIS = RET_HEAD_DIM // 4
NORM_EPS = 1e-6

LANES = 128
HEADS_PER_BLOCK = LANES // RET_HEAD_DIM
RET_BLOCKS = RET_WIDTH // LANES
RET_CHUNK = 256
DFT_N2 = 128
DFT_ROWS = 16
DFT_BLOCK_ROWS = 64
DFT_KHALF = 72
DFTC_LANES = 2048
DFTC_K1 = 3
VMEM_LIMIT = 56 * 1024 * 1024


def _cparams(sem):
    return pltpu.CompilerParams(dimension_semantics=sem, vmem_limit_bytes=VMEM_LIMIT)


def _const_spec(shape):
    nd = len(shape)
    return pl.BlockSpec(shape, lambda *_: (0,) * nd)


def _silu(x):
    return x * jax.nn.sigmoid(x)


def _rms(x):
    return x * lax.rsqrt(jnp.mean(x * x, axis=-1, keepdims=True) + NORM_EPS)


def _ada_kernel(c_ref, w_ref, b_ref, o_ref):
    s = _silu(c_ref[...])
    o_ref[...] = jnp.dot(s, w_ref[...], preferred_element_type=F32, precision=HIGHEST) + b_ref[...]


def _ada(cvec, w, b):
    R, D = cvec.shape
    N = w.shape[1]
    return pl.pallas_call(
        _ada_kernel,
        grid=(N // D,),
        in_specs=[_const_spec((R, D)), pl.BlockSpec((D, D), lambda j: (0, j)),
                  pl.BlockSpec((1, D), lambda j: (0, j))],
        out_specs=pl.BlockSpec((R, D), lambda j: (0, j)),
        out_shape=jax.ShapeDtypeStruct((R, N), F32),
        compiler_params=_cparams(("arbitrary",)),
        name="ada",
    )(cvec, w, b)


INPROJ_ROWS = 1024
INPROJ_NCHUNK = 1024
HALO = 16


def _norm_mod(x, g_ref, sh_ref, sc_ref):
    h = _rms(x) * g_ref[...]
    return (h * (1.0 + sc_ref[0]) + sh_ref[0]).astype(BF16)


def _inproj_kernel(x_ref, xp_ref, xn_ref, g_ref, sh_ref, sc_ref, w_ref, cw_ref, cb_ref, cos_ref, sin_ref,
                   vg_ref, x0_ref, ur_ref):
    i = pl.program_id(1)
    tm = x_ref.shape[1]
    ext = tm + 2 * HALO
    hb = _norm_mod(jnp.concatenate([xp_ref[0, 0], x_ref[0], xn_ref[0, 0]], axis=0), g_ref, sh_ref, sc_ref)
    row = lax.broadcasted_iota(jnp.int32, (ext, 1), 0)
    outside = ((row < HALO) & (i == 0)) | ((row >= tm + HALO) & (i == pl.num_programs(1) - 1))
    hb = jnp.where(outside, jnp.zeros_like(hb), hb)
    conv = []
    for c0 in range(0, HY_PROJ, HY_WIDTH):
        cols = slice(c0, c0 + HY_WIDTH)
        r = jnp.dot(hb, w_ref[:, cols], preferred_element_type=F32)
        cw = cw_ref[:, cols]
        y = pltpu.roll(r, 1, 0) * cw[0:1] + r * cw[1:2] + pltpu.roll(r, ext - 1, 0) * cw[2:3] + cb_ref[:, cols]
        conv.append(y[HALO:HALO + tm])
    x0_ref[0] = conv[0].astype(BF16)
    vg_ref[0] = (conv[2] * conv[1]).astype(BF16)

    hm = hb[HALO:HALO + tm]
    even = (lax.broadcasted_iota(jnp.int32, (1, LANES), 1) % 2) == 0
    for c0 in range(HY_PROJ, w_ref.shape[1], INPROJ_NCHUNK):
        r = jnp.dot(hm, w_ref[:, c0:c0 + INPROJ_NCHUNK], preferred_element_type=F32)
        parts = []
        for b0 in range(0, INPROJ_NCHUNK, LANES):
            t = r[:, b0:b0 + LANES]
            if c0 + b0 < HY_PROJ + 2 * RET_WIDTH:
                sw = jnp.where(even, pltpu.roll(t, LANES - 1, 1), pltpu.roll(t, 1, 1))
                t = t * cos_ref[...] + sw * sin_ref[...]
                if c0 + b0 >= HY_PROJ + RET_WIDTH:
                    t = t * (RET_HEAD_DIM ** -0.5)
            parts.append(t)
        ur_ref[0, :, c0 - HY_PROJ:c0 - HY_PROJ + INPROJ_NCHUNK] = jnp.concatenate(parts, axis=1).astype(BF16)


def _inproj(x, g, shift, scale, w, conv_w, conv_b, rope, tm):
    B, L, D = x.shape
    N = w.shape[1]
    nh = tm // HALO
    nrow = L // HALO
    xh = x.reshape(B, nrow, HALO, D)
    vec = pl.BlockSpec((1, 1, D), lambda b, i: (b, 0, 0))
    row = lambda n: pl.BlockSpec((1, tm, n), lambda b, i: (b, i, 0))
    tab = pl.BlockSpec((tm, LANES), lambda b, i: (i, 0))
    return pl.pallas_call(
        _inproj_kernel,
        grid=(B, L // tm),
        in_specs=[row(D),
                  pl.BlockSpec((1, 1, HALO, D), lambda b, i: (b, jnp.maximum(i * nh - 1, 0), 0, 0)),
                  pl.BlockSpec((1, 1, HALO, D), lambda b, i: (b, jnp.minimum((i + 1) * nh, nrow - 1), 0, 0)),
                  _const_spec((1, D)), vec, vec,
                  pl.BlockSpec((D, N), lambda b, i: (0, 0), pipeline_mode=pl.Buffered(1)),
                  _const_spec(conv_w.shape), _const_spec(conv_b.shape), tab, tab],
        out_specs=[row(HY_WIDTH), row(HY_WIDTH), row(N - HY_PROJ)],
        out_shape=[jax.ShapeDtypeStruct((B, L, HY_WIDTH), BF16), jax.ShapeDtypeStruct((B, L, HY_WIDTH), BF16),
                   jax.ShapeDtypeStruct((B, L, N - HY_PROJ), BF16)],
        compiler_params=_cparams(("parallel", "parallel")),
        name="inproj",
    )(x, xh, xh, g, shift, scale, w, conv_w, conv_b, *rope)


def _block_diag_mask():
    r = lax.broadcasted_iota(jnp.int32, (LANES, LANES), 0) // RET_HEAD_DIM
    c = lax.broadcasted_iota(jnp.int32, (LANES, LANES), 1) // RET_HEAD_DIM
    return r == c


def _kt_v(k, v):
    return lax.dot_general(k, v, (((0,), (0,)), ((), ())), preferred_element_type=F32)


def _ctx_state_kernel(x_ref, g_ref, sh_ref, sc_ref, wk_ref, wv_ref, lgf_ref, lgb_ref, sf_ref, sb_ref):
    hb = _norm_mod(x_ref[0], g_ref, sh_ref, sc_ref)
    k = jnp.dot(hb, wk_ref[...], preferred_element_type=F32) * (RET_HEAD_DIM ** -0.5)
    v = jnp.dot(hb, wv_ref[...], preferred_element_type=F32).astype(BF16)
    lc = k.shape[0]
    m = lax.broadcasted_iota(jnp.int32, (lc, LANES), 0).astype(F32)
    bd = _block_diag_mask()
    for p in range(RET_BLOCKS):
        lanes = slice(p * LANES, (p + 1) * LANES)
        wf = jnp.exp(lgf_ref[p] * (lc - 1.0 - m))
        wb = jnp.exp(lgb_ref[p] * m)
        sf_ref[0, p] = jnp.where(bd, _kt_v((k[:, lanes] * wf).astype(BF16), v[:, lanes]), 0.0)
        sb_ref[0, p] = jnp.where(bd, _kt_v((k[:, lanes] * wb).astype(BF16), v[:, lanes]), 0.0)


def _ctx_states(ctx, g, shift, scale, wk, wv, lgf, lgb):
    B, lc, D = ctx.shape
    st = jax.ShapeDtypeStruct((B, RET_BLOCKS, LANES, LANES), F32)
    sspec = pl.BlockSpec((1, RET_BLOCKS, LANES, LANES), lambda b: (b, 0, 0, 0))
    return pl.pallas_call(
        _ctx_state_kernel,
        grid=(B,),
        in_specs=[pl.BlockSpec((1, lc, D), lambda b: (b, 0, 0)), _const_spec((1, D)),
                  _const_spec(shift.shape), _const_spec(scale.shape), _const_spec(wk.shape), _const_spec(wv.shape),
                  _const_spec(lgf.shape), _const_spec(lgb.shape)],
        out_specs=[sspec, sspec],
        out_shape=[st, st],
        compiler_params=_cparams(("parallel",)),
        name="ctx_state",
    )(ctx, g, shift, scale, wk, wv, lgf, lgb)


FILT_ROWS = 1024
FILT_GROUPS = LANES // HY_BANDS
FILT_HALVES = LANES // HY_FILT_HID


def _filt_kernel(bands_ref, w1t_ref, w1c_ref, w1s_ref, b1_ref, fr1_ref, w2_ref, b2_ref, fr2_ref,
                 w3_ref, dl_ref, h_ref, asum_ref, *, seq_len):
    i = pl.program_id(0)
    tl = FILT_ROWS
    gr = tl // FILT_GROUPS
    hr = tl // FILT_HALVES
    lane = lax.broadcasted_iota(jnp.int32, (1, LANES), 1)
    base = i * tl
    dot = functools.partial(jnp.dot, preferred_element_type=F32, precision=HIGHEST)

    n8 = (lax.broadcasted_iota(jnp.int32, (gr, 1), 0) + (lane // HY_BANDS) * gr + base).astype(F32)
    arg = bands_ref[...] * ((2.0 * math.pi / seq_len) * n8)
    cz, sz = jnp.cos(arg), jnp.sin(arg)
    nh = (lax.broadcasted_iota(jnp.int32, (hr, 1), 0) + (lane // HY_FILT_HID) * hr + base).astype(F32)
    blocks = [dot(cz, w1c_ref[r]) - dot(sz, w1s_ref[r]) for r in range(hr // gr)]
    pre = jnp.concatenate(blocks, axis=0) + (nh / (seq_len - 1.0)) * w1t_ref[...] + b1_ref[...]
    h = jnp.sin(fr1_ref[...] * pre)
    h = jnp.sin(fr2_ref[...] * (dot(h, w2_ref[...]) + b2_ref[...])).astype(BF16)

    @pl.when(i == 0)
    def _():
        asum_ref[...] = jnp.zeros_like(asum_ref)

    for p in range(FILT_HALVES):
        n = (lax.broadcasted_iota(jnp.int32, (hr, 1), 0) + p * hr + base).astype(F32)
        t = n / (seq_len - 1.0)
        taps = jnp.dot(h, w3_ref[p], preferred_element_type=F32) * jnp.exp(-t * dl_ref[...])
        asum_ref[...] += jnp.sum(jnp.abs(taps), axis=0, keepdims=True)
        rows = slice(p * hr, (p + 1) * hr)
        h_ref[0, rows] = taps[:, :HY_WIDTH].astype(BF16)
        h_ref[1, rows] = jnp.where(n == 0.0, 0.0, taps[:, HY_WIDTH:]).astype(BF16)


def _filters(seq_len, w1, b1, fr1, w2, b2, fr2, w3):
    hid = HY_FILT_HID
    f32 = lambda a: a.astype(F32)
    bands = np.tile(np.linspace(1e-4, HY_BANDS - 1, HY_BANDS, dtype=np.float32), FILT_GROUPS)[None]
    deltas = np.abs(np.linspace(math.log(HY_DECAY_TARGET) / HY_SLOW_PCT,
                                math.log(HY_DECAY_TARGET) / HY_FAST_PCT, HY_WIDTH, dtype=np.float32))
    dl = np.tile(deltas, 2)[None, :]
    halves = lambda v: jnp.tile(f32(v)[None], (1, FILT_HALVES))

    nb = FILT_GROUPS // FILT_HALVES
    sel = np.zeros((nb, FILT_GROUPS, FILT_HALVES), np.float32)
    for r in range(nb):
        for p in range(FILT_HALVES):
            sel[r, p * nb + r, p] = 1.0
    eye = np.eye(FILT_HALVES, dtype=np.float32)

    def place_l1(wb):
        return (sel[:, :, None, :, None] * f32(wb)[None, None, :, None, :]).reshape(nb, LANES, LANES)

    w2d = (eye[:, None, :, None] * f32(w2)[None, :, None, :]).reshape(LANES, LANES)
    w3p = (eye[:, :, None, None] * f32(w3)[None, None]).reshape(FILT_HALVES, LANES, 2 * HY_WIDTH)
    args = (jnp.asarray(bands), halves(w1[0]), place_l1(w1[1:1 + HY_BANDS]), place_l1(w1[1 + HY_BANDS:]),
            halves(b1), halves(fr1), w2d, halves(b2), halves(fr2), w3p.astype(BF16), jnp.asarray(dl))
    return pl.pallas_call(
        functools.partial(_filt_kernel, seq_len=seq_len),
        grid=(seq_len // FILT_ROWS,),
        in_specs=[_const_spec(a.shape) for a in args],
        out_specs=[pl.BlockSpec((2, FILT_ROWS, HY_WIDTH), lambda i: (0, i, 0)),
                   _const_spec((1, 2 * HY_WIDTH))],
        out_shape=[jax.ShapeDtypeStruct((2, seq_len, HY_WIDTH), BF16),
                   jax.ShapeDtypeStruct((1, 2 * HY_WIDTH), F32)],
        compiler_params=_cparams(("arbitrary",)),
        name="filt",
    )(*args)


def _dft_tables(seq_len):
    n = 2 * seq_len
    n1_full = n // DFT_N2
    n1_used = n1_full // 2
    nk1 = n1_full // 2 + 1
    k1 = np.arange(nk1)[:, None]
    n1 = np.arange(n1_used)[None, :]
    th = 2.0 * np.pi * ((k1 * n1) % n1_full) / n1_full
    ck = np.full((nk1, 1), 2.0)
    ck[0, 0] = 1.0
    ck[-1, 0] = 1.0
    assert nk1 <= DFT_KHALF
    f1 = np.zeros((2 * DFT_KHALF, n1_used))
    f1[:nk1] = np.cos(th)
    f1[DFT_KHALF:DFT_KHALF + nk1] = -np.sin(th)
    finv = np.zeros((n1_used, 2 * DFT_KHALF))
    finv[:, 0:2 * nk1:2] = (ck * np.cos(th) / n).T
    finv[:, 1:2 * nk1:2] = (-ck * np.sin(th) / n).T
    k1 = np.arange(nk1)[:, None, None]
    k2 = np.arange(DFT_N2)[None, :, None]
    n2 = np.arange(DFT_N2)[None, None, :]
    ph = 2.0 * np.pi * ((n1_full * n2 * k2 + n2 * k1) % n) / n
    gr, gi = np.cos(ph), -np.sin(ph)
    g = np.concatenate([np.concatenate([gr, -gi], 2), np.concatenate([gi, gr], 2)], 1)
    grt, git = np.swapaxes(gr, 1, 2), -np.swapaxes(gi, 1, 2)
    ginv = np.concatenate([np.concatenate([grt, -git], 2), np.concatenate([git, grt], 2)], 1)
    npad = -nk1 % DFTC_K1
    assert nk1 + npad <= DFT_KHALF
    g, ginv = (np.concatenate([m, np.zeros((npad,) + m.shape[1:])], 0) for m in (g, ginv))
    f = lambda a: jnp.asarray(a.astype(np.float32))
    return f(f1), f(finv), f(g), f(ginv)


def _rows2d(a):
    return a.reshape(a.shape[0] * a.shape[1], a.shape[2])


def _dft_a_kernel(x_ref, f_ref, o_ref):
    n1u = x_ref.shape[1]
    f = f_ref[...]
    for r0 in range(0, x_ref.shape[2], DFT_ROWS):
        grp = slice(r0, r0 + DFT_ROWS)
        xs = _rows2d(jnp.swapaxes(x_ref[0, :, grp, :], 0, 1))
        res = [jnp.dot(f, xs[r * n1u:(r + 1) * n1u], preferred_element_type=F32).astype(BF16)
               for r in range(DFT_ROWS)]
        a = jnp.swapaxes(jnp.stack(res, axis=0), 0, 1)
        o_ref[:, 0, grp, :] = a[:DFT_KHALF]
        o_ref[:, 1, grp, :] = a[DFT_KHALF:]


def _dft_a(xv, f1):
    Bd, n1u, n2, C = xv.shape
    return pl.pallas_call(
        _dft_a_kernel,
        grid=(Bd, n2 // DFT_BLOCK_ROWS),
        in_specs=[pl.BlockSpec((1, n1u, DFT_BLOCK_ROWS, C), lambda b, j: (b, 0, j, 0)), _const_spec(f1.shape)],
        out_specs=pl.BlockSpec((DFT_KHALF, 2, DFT_BLOCK_ROWS, C), lambda b, j: (0, 0, j, b)),
        out_shape=jax.ShapeDtypeStruct((DFT_KHALF, 2, n2, Bd * C), BF16),
        compiler_params=_cparams(("parallel", "parallel")),
        name="dft_a",
    )(xv, f1)


def _dft_c_kernel(a_ref, h_ref, g_ref, gi_ref, asum_ref, o_ref):
    C = HY_WIDTH
    asum = asum_ref[...]
    inv = 1.0 / (asum[:, :C] + asum[:, C:] + 1e-6)
    step = min(DFTC_LANES, a_ref.shape[3])
    for t in range(a_ref.shape[0]):
        g = g_ref[t].astype(BF16)
        gi = gi_ref[t].astype(BF16)
        rows = g.shape[0]
        half = rows // 2
        xh = jnp.dot(g, h_ref[t].reshape(rows, h_ref.shape[3]), preferred_element_type=F32)
        kr = (xh[:half, :C] + xh[:half, C:]) * inv
        ki = (xh[half:, :C] - xh[half:, C:]) * inv
        for l0 in range(0, a_ref.shape[3], step):
            x = jnp.dot(g, a_ref[t, :, :, l0:l0 + step].reshape(rows, step), preferred_element_type=F32)
            ys = []
            for c0 in range(0, step, C):
                xr, xi = x[:half, c0:c0 + C], x[half:, c0:c0 + C]
                ys.append(jnp.concatenate([xr * kr - xi * ki, xr * ki + xi * kr], axis=0).astype(BF16))
            bk = jnp.dot(gi, jnp.concatenate(ys, axis=1), preferred_element_type=F32).astype(BF16)
            o_ref[t, :, :, l0:l0 + step] = bk.reshape(2, half, step)


def _dft_c(a, h, g, ginv, asum):
    _, _, n2, bc = a.shape
    nk = g.shape[0]
    dspec = lambda w: pl.BlockSpec((DFTC_K1, 2, n2, w), lambda k: (k, 0, 0, 0))
    gspec = pl.BlockSpec((DFTC_K1, 2 * n2, 2 * n2), lambda k: (k, 0, 0))
    return pl.pallas_call(
        _dft_c_kernel,
        grid=(nk // DFTC_K1,),
        in_specs=[dspec(bc), dspec(h.shape[3]), gspec, gspec, _const_spec(asum.shape)],
        out_specs=dspec(bc),
        out_shape=jax.ShapeDtypeStruct((nk, 2, n2, bc), BF16),
        compiler_params=_cparams(("parallel",)),
        name="dft_c",
    )(a, h, g, ginv, asum)


def _idft_a_kernel(b_ref, f_ref, vg_ref, x0_ref, bias_ref, o_ref):
    nk1, _, _, C = b_ref.shape
    n1u = vg_ref.shape[1]
    R = DFT_ROWS
    mp = f_ref.shape[1]
    f = f_ref[...]
    for r0 in range(0, b_ref.shape[2], R):
        grp = slice(r0, r0 + R)
        bv = b_ref[:, :, grp, :].reshape(2 * nk1, R, C)
        bv = jnp.concatenate([bv, jnp.zeros((mp - 2 * nk1, R, C), BF16)], axis=0)
        bs = _rows2d(jnp.swapaxes(bv, 0, 1))
        ys = [jnp.dot(f, bs[r * mp:(r + 1) * mp], preferred_element_type=F32).astype(BF16) for r in range(R)]
        y = _rows2d(jnp.swapaxes(jnp.stack(ys, axis=0), 0, 1)).astype(F32)
        y = y + _rows2d(vg_ref[0, :, grp, :]).astype(F32) * bias_ref[...]
        y = y * _rows2d(x0_ref[0, :, grp, :]).astype(F32)
        o_ref[0, :, grp, :] = y.astype(BF16).reshape(n1u, R, C)


def _idft_a(bk, finv, vgv, x0v, bias):
    B, n1u, n2, C = vgv.shape
    nk1 = bk.shape[0]
    dsp = pl.BlockSpec((1, n1u, DFT_BLOCK_ROWS, C), lambda b, j: (b, 0, j, 0))
    return pl.pallas_call(
        _idft_a_kernel,
        grid=(B, n2 // DFT_BLOCK_ROWS),
        in_specs=[pl.BlockSpec((nk1, 2, DFT_BLOCK_ROWS, C), lambda b, j: (0, 0, j, b)),
                  _const_spec(finv.shape), dsp, dsp, _const_spec(bias.shape)],
        out_specs=dsp,
        out_shape=jax.ShapeDtypeStruct((B, n1u, n2, C), BF16),
        compiler_params=_cparams(("parallel", "parallel")),
        name="idft_a",
    )(bk, finv, vgv, x0v, bias)


def _retention_kernel(q_ref, k_ref, v_ref, g_ref, lgf_ref, lgb_ref, s0f_ref, s0b_ref, gn_ref, o_ref,
                      inc_ref, st_ref, dm_ref, wt_ref):
    C = RET_CHUNK
    seq_len = q_ref.shape[1]
    nc = seq_len // C
    lgf = lgf_ref[0]
    lgb = lgb_ref[0]
    lane = lax.broadcasted_iota(jnp.int32, (1, LANES), 1)
    head_of_lane = lane // RET_HEAD_DIM
    bd = _block_diag_mask()

    ri = lax.broadcasted_iota(jnp.int32, (C, LANES), 0).astype(F32)
    wt_ref[0] = jnp.exp(lgf * (ri + 1.0)).astype(BF16)
    wt_ref[1] = jnp.exp(lgb * (C - ri)).astype(BF16)
    wt_ref[2] = jnp.exp(lgf * (C - 1.0 - ri)).astype(BF16)
    wt_ref[3] = jnp.exp(lgb * ri).astype(BF16)
    di = lax.broadcasted_iota(jnp.int32, (C, C), 0)
    dj = lax.broadcasted_iota(jnp.int32, (C, C), 1)
    dd = (di - dj).astype(F32)
    for h in range(HEADS_PER_BLOCK):
        lf = jnp.sum(jnp.where(lane == h * RET_HEAD_DIM, lgf, 0.0), axis=1, keepdims=True)
        lb = jnp.sum(jnp.where(lane == h * RET_HEAD_DIM, lgb, 0.0), axis=1, keepdims=True)
        dm_ref[h] = jnp.where(dd > 0, jnp.exp(lf * dd), jnp.where(dd < 0, jnp.exp(-lb * dd), 2.0)).astype(BF16)

    def chunk(c):
        return pl.ds(pl.multiple_of(c * C, C), C)

    def inc_body(c, carry):
        rows = chunk(c)
        k = k_ref[0, rows, :]
        kw = jnp.concatenate([k * wt_ref[2], k * wt_ref[3]], axis=1)
        t = _kt_v(kw, v_ref[0, rows, :])
        inc_ref[c, :LANES] = jnp.where(bd, t[:LANES], 0.0)
        inc_ref[c, LANES:] = jnp.where(bd, t[LANES:], 0.0)
        return carry

    lax.fori_loop(0, nc, inc_body, 0, unroll=True)

    gcf = jnp.exp(lgf * float(C))
    gcb = jnp.exp(lgb * float(C))

    def fscan(c, s):
        st_ref[c, :LANES] = s.astype(BF16)
        return s * gcf + inc_ref[c, :LANES]

    lax.fori_loop(0, nc, fscan, s0f_ref[0, 0], unroll=True)

    def bscan(i, s):
        c = nc - 1 - i
        st_ref[c, LANES:] = s.astype(BF16)
        return s * gcb + inc_ref[c, LANES:]

    lax.fori_loop(0, nc, bscan, s0b_ref[0, 0], unroll=True)

    gn = gn_ref[...]
    first = head_of_lane == 0
    inv_n = 1.0 / RET_HEAD_DIM

    def head_mean(a):
        tot = jnp.sum(a, axis=1, keepdims=True)
        s0 = jnp.sum(jnp.where(first, a, 0.0), axis=1, keepdims=True)
        return jnp.where(first, s0, tot - s0) * inv_n

    def out_body(c, carry):
        rows = chunk(c)
        q = q_ref[0, rows, :]
        k = k_ref[0, rows, :]
        v = v_ref[0, rows, :]
        qw = jnp.concatenate([q * wt_ref[0], q * wt_ref[1]], axis=1)
        o = jnp.dot(qw, st_ref[c], preferred_element_type=F32)
        oh = []
        for h in range(HEADS_PER_BLOCK):
            qm = jnp.where(head_of_lane == h, q, jnp.zeros_like(q))
            sc = lax.dot_general(qm, k, (((1,), (1,)), ((), ())), preferred_element_type=F32)
            oh.append(jnp.dot(sc.astype(BF16) * dm_ref[h], v, preferred_element_type=F32))
        o = o + jnp.where(first, oh[0], oh[1])
        d = o - head_mean(o)
        y = d * lax.rsqrt(head_mean(d * d) + NORM_EPS) * gn
        o_ref[0, rows, :] = (y * _silu(g_ref[0, rows, :].astype(F32))).astype(BF16)
        return carry

    lax.fori_loop(0, nc, out_body, 0, unroll=True)


def _retention(u, lgf, lgb, s0f, s0b, gn):
    B, L, _ = u.shape
    assert HEADS_PER_BLOCK == 2
    nc = L // RET_CHUNK
    col = lambda g: pl.BlockSpec((1, L, LANES), lambda b, p: (b, 0, g * RET_BLOCKS + p))
    lsp = pl.BlockSpec((1, 1, LANES), lambda b, p: (p, 0, 0))
    ssp = pl.BlockSpec((1, 1, LANES, LANES), lambda b, p: (b, p, 0, 0))
    return pl.pallas_call(
        _retention_kernel,
        grid=(B, RET_BLOCKS),
        in_specs=[col(0), col(1), col(2), col(3), lsp, lsp, ssp, ssp,
                  pl.BlockSpec((1, LANES), lambda b, p: (0, p))],
        out_specs=pl.BlockSpec((1, L, LANES), lambda b, p: (b, 0, p)),
        out_shape=jax.ShapeDtypeStruct((B, L, RET_WIDTH), BF16),
        scratch_shapes=[pltpu.VMEM((nc, 2 * LANES, LANES), F32),
                        pltpu.VMEM((nc, 2 * LANES, LANES), BF16),
                        pltpu.VMEM((HEADS_PER_BLOCK, RET_CHUNK, RET_CHUNK), BF16),
                        pltpu.VMEM((4, RET_CHUNK, LANES), BF16)],
        compiler_params=_cparams(("parallel", "parallel")),
        name="retention",
    )(u, u, u, u, lgf, lgb, s0f, s0b, gn)


def _rope_tables(seq_len):
    rows = seq_len // GRID_W
    r, col = np.meshgrid(np.arange(rows, dtype=np.float64), np.arange(GRID_W, dtype=np.float64), indexing="ij")
    inv = ROPE_BASE ** (-np.arange(ROPE_PAIRS_AXIS, dtype=np.float64) / ROPE_PAIRS_AXIS)
    ang = np.concatenate([r.reshape(-1, 1) * inv, col.reshape(-1, 1) * inv], axis=-1)
    ang = np.tile(np.repeat(ang, 2, axis=1), (1, HEADS_PER_BLOCK))
    sign = np.where(np.arange(LANES) % 2 == 0, -1.0, 1.0)
    return jnp.asarray(np.cos(ang).astype(np.float32)), jnp.asarray((np.sin(ang) * sign).astype(np.float32))


OUT_ROWS = 1024
OUT_FF_CHUNK = 2048


def _out_mlp_kernel(x_ref, yh_ref, yr_ref, wo1_ref, wo2_ref, w1_ref, w2_ref, g2_ref, gf_ref, mod_ref, o_ref,
                    *, ff_chunk):
    mod = mod_ref[0]
    mix = (jnp.dot(yh_ref[0], wo1_ref[...], preferred_element_type=F32)
           + jnp.dot(yr_ref[0], wo2_ref[...], preferred_element_type=F32))
    x1 = x_ref[0] + mod[0:1] * mix
    h = (_rms(x1) * g2_ref[...] * (1.0 + mod[2:3]) + mod[1:2]).astype(BF16)
    acc = jnp.zeros_like(x1)
    for j in range(w1_ref.shape[1] // ff_chunk):
        cols = slice(j * ff_chunk, (j + 1) * ff_chunk)
        a = jnp.maximum(jnp.dot(h, w1_ref[:, cols], preferred_element_type=F32), 0.0)
        acc = acc + jnp.dot((a * a).astype(BF16), w2_ref[cols, :], preferred_element_type=F32)
    x2 = x1 + mod[3:4] * acc
    o_ref[0] = _rms(x2) * gf_ref[...]


def _out_mlp(x, yh, yr, wo1, wo2, w1, w2, g2, gf, mod, tm):
    B, L, D = x.shape
    dff = w1.shape[1]
    row = lambda n: pl.BlockSpec((1, tm, n), lambda b, i: (b, i, 0))
    wsp = lambda a: pl.BlockSpec(a.shape, lambda b, i: (0, 0), pipeline_mode=pl.Buffered(1))
    return pl.pallas_call(
        functools.partial(_out_mlp_kernel, ff_chunk=OUT_FF_CHUNK),
        grid=(B, L // tm),
        in_specs=[row(D), row(yh.shape[2]), row(yr.shape[2]), wsp(wo1), wsp(wo2), wsp(w1), wsp(w2),
                  _const_spec((1, D)), _const_spec((1, D)),
                  pl.BlockSpec((1, 4, D), lambda b, i: (b, 0, 0))],
        out_specs=row(D),
        out_shape=jax.ShapeDtypeStruct((B, L, D), F32),
        compiler_params=_cparams(("parallel", "parallel")),
        name="out_mlp",
    )(x, yh, yr, wo1, wo2, w1, w2, g2, gf, mod)


def kernel(x, c, ctx, c_ctx, w_ada, b_ada, norm1_g, w_in, hy_conv_w, hy_conv_b, hy_f_w1, hy_f_b1, hy_f_freq1,
           hy_f_w2, hy_f_b2, hy_f_freq2, hy_f_w3, hy_bias, ret_decay_logit, ret_gn_g, w_out, norm2_g, w_mlp1,
           w_mlp2, norm_f_g):
    B, L, D = x.shape
    depth = w_ada.shape[0]
    assert depth == 1, "single-layer block"
    l = 0
    rows = -(-(B + 1) // 8) * 8
    cvec = jnp.concatenate([c, c_ctx[None], jnp.zeros((rows - B - 1, D), F32)], axis=0)
    mod = _ada(cvec, w_ada[l], b_ada[l][None])
    mx = mod[:B].reshape(B, 6, D)
    mc = mod[B].reshape(6, D)

    lg = jax.nn.log_sigmoid(ret_decay_logit[l].astype(F32))
    lg_lanes = jnp.repeat(lg, RET_HEAD_DIM, axis=1).reshape(2, RET_BLOCKS, 1, LANES)
    lgf, lgb = lg_lanes[0], lg_lanes[1]

    w_in_b = w_in[l].astype(BF16)
    g1 = norm1_g[l][None]
    vg, x0c, u_ret = _inproj(x, g1, mx[:, 0:1], mx[:, 1:2], w_in_b, hy_conv_w[l].astype(F32),
                             hy_conv_b[l][None].astype(F32), _rope_tables(L), tm=INPROJ_ROWS)
    kc0 = HY_PROJ + RET_WIDTH
    s0f, s0b = _ctx_states(ctx, g1, mc[0][None, None], mc[1][None, None], w_in_b[:, kc0:kc0 + RET_WIDTH],
                           w_in_b[:, kc0 + RET_WIDTH:kc0 + 2 * RET_WIDTH], lgf, lgb)

    f1, finv, gmat, ginv = _dft_tables(L)
    f1, finv = f1.astype(BF16), finv.astype(BF16)
    taps, asum = _filters(L, hy_f_w1[l], hy_f_b1[l], hy_f_freq1[l], hy_f_w2[l], hy_f_b2[l], hy_f_freq2[l],
                          hy_f_w3[l])
    tview = lambda a: a.reshape(a.shape[0], L // DFT_N2, DFT_N2, HY_WIDTH)
    bk = _dft_c(_dft_a(tview(vg), f1), _dft_a(tview(taps), f1), gmat, ginv, asum)
    y_hy = _idft_a(bk, finv, tview(vg), tview(x0c), hy_bias[l][None].astype(F32)).reshape(B, L, HY_WIDTH)

    y_ret = _retention(u_ret, lgf, lgb, s0f, s0b, ret_gn_g[l][None].astype(F32))

    w_out_b = w_out[l].astype(BF16)
    mod2 = mx[:, 2:6]
    return _out_mlp(x, y_hy, y_ret, w_out_b[:HY_WIDTH], w_out_b[HY_WIDTH:], w_mlp1[l].astype(BF16),
                    w_mlp2[l].astype(BF16), norm2_g[l][None], norm_f_g[None], mod2, tm=OUT_ROWS)
```

```python
import functools
import math

import numpy as np
import jax
import jax.numpy as jnp
from jax import lax
from jax.experimental import pallas as pl
from jax.experimental.pallas import tpu as pltpu

F32 = jnp.float32
BF16 = jnp.bfloat16
HIGHEST = lax.Precision.HIGHEST

GRID_W = 64
HY_WIDTH = 512
RET_WIDTH = 512
RET_HEADS = 8
RET_HEAD_DIM = RET_WIDTH // RET_HEADS
HY_PROJ = 3 * HY_WIDTH
HY_BANDS = 16
HY_FILT_HID = 64
HY_DECAY_TARGET = 1e-2
HY_FAST_PCT = 0.3
HY_SLOW_PCT = 1.5
ROPE_BASE = 10000.0
ROPE_PAIRS_AXIS = RET_HEAD_DIM // 4
NORM_EPS = 1e-6

LANES = 128
HEADS_PER_BLOCK = LANES // RET_HEAD_DIM
RET_BLOCKS = RET_WIDTH // LANES
RET_CHUNK = 256
DFT_N2 = 128
DFT_ROWS = 16
DFT_BLOCK_ROWS = 64
DFT_KHALF = 72
DFTC_LANES = 2048
DFTC_K1 = 3
VMEM_LIMIT = 56 * 1024 * 1024


def _cparams(sem):
    return pltpu.CompilerParams(dimension_semantics=sem, vmem_limit_bytes=VMEM_LIMIT)


def _const_spec(shape):
    nd = len(shape)
    return pl.BlockSpec(shape, lambda *_: (0,) * nd)


def _silu(x):
    return x * jax.nn.sigmoid(x)


def _rms(x):
    return x * lax.rsqrt(jnp.mean(x * x, axis=-1, keepdims=True) + NORM_EPS)


def _ada_kernel(c_ref, w_ref, b_ref, o_ref):
    s = _silu(c_ref[...])
    o_ref[...] = jnp.dot(s, w_ref[...], preferred_element_type=F32, precision=HIGHEST) + b_ref[...]


def _ada(cvec, w, b):
    R, D = cvec.shape
    N = w.shape[1]
    return pl.pallas_call(
        _ada_kernel,
        grid=(N // D,),
        in_specs=[_const_spec((R, D)), pl.BlockSpec((D, D), lambda j: (0, j)),
                  pl.BlockSpec((1, D), lambda j: (0, j))],
        out_specs=pl.BlockSpec((R, D), lambda j: (0, j)),
        out_shape=jax.ShapeDtypeStruct((R, N), F32),
        compiler_params=_cparams(("arbitrary",)),
        name="ada",
    )(cvec, w, b)


INPROJ_ROWS = 1024
INPROJ_NCHUNK = 1024
HALO = 16


def _norm_mod(x, g_ref, sh_ref, sc_ref):
    gain = g_ref[...] * (1.0 + sc_ref[0])
    return (_rms(x) * gain + sh_ref[0]).astype(BF16)


def _inproj_kernel(x_ref, xp_ref, xn_ref, g_ref, sh_ref, sc_ref, w_ref, cw_ref, cb_ref, cos_ref, sin_ref,
                   vg_ref, x0_ref, ur_ref):
    i = pl.program_id(1)
    tm = x_ref.shape[1]
    ext = tm + 2 * HALO
    hb = _norm_mod(jnp.concatenate([xp_ref[0, 0], x_ref[0], xn_ref[0, 0]], axis=0), g_ref, sh_ref, sc_ref)
    row = lax.broadcasted_iota(jnp.int32, (ext, 1), 0)
    outside = ((row < HALO) & (i == 0)) | ((row >= tm + HALO) & (i == pl.num_programs(1) - 1))
    hb = jnp.where(outside, jnp.zeros_like(hb), hb)
    conv = []
    for c0 in range(0, HY_PROJ, HY_WIDTH):
        cols = slice(c0, c0 + HY_WIDTH)
        r = jnp.dot(hb, w_ref[:, cols], preferred_element_type=F32)
        cw = cw_ref[:, cols]
        y = pltpu.roll(r, 1, 0) * cw[0:1] + r * cw[1:2] + pltpu.roll(r, ext - 1, 0) * cw[2:3] + cb_ref[:, cols]
        conv.append(y[HALO:HALO + tm])
    x0_ref[0] = conv[0].astype(BF16)
    vg_ref[0] = (conv[2] * conv[1]).astype(BF16)

    hm = hb[HALO:HALO + tm]
    even = (lax.broadcasted_iota(jnp.int32, (1, LANES), 1) % 2) == 0
    for c0 in range(HY_PROJ, w_ref.shape[1], INPROJ_NCHUNK):
        r = jnp.dot(hm, w_ref[:, c0:c0 + INPROJ_NCHUNK], preferred_element_type=F32)
        parts = []
        for b0 in range(0, INPROJ_NCHUNK, LANES):
            t = r[:, b0:b0 + LANES]
            if c0 + b0 < HY_PROJ + 2 * RET_WIDTH:
                sw = jnp.where(even, pltpu.roll(t, LANES - 1, 1), pltpu.roll(t, 1, 1))
                t = t * cos_ref[...] + sw * sin_ref[...]
                if c0 + b0 >= HY_PROJ + RET_WIDTH:
                    t = t * (RET_HEAD_DIM ** -0.5)
            parts.append(t)
        ur_ref[0, :, c0 - HY_PROJ:c0 - HY_PROJ + INPROJ_NCHUNK] = jnp.concatenate(parts, axis=1).astype(BF16)


def _inproj(x, g, shift, scale, w, conv_w, conv_b, rope, tm):
    B, L, D = x.shape
    N = w.shape[1]
    nh = tm // HALO
    nrow = L // HALO
    xh = x.reshape(B, nrow, HALO, D)
    vec = pl.BlockSpec((1, 1, D), lambda b, i: (b, 0, 0))
    row = lambda n: pl.BlockSpec((1, tm, n), lambda b, i: (b, i, 0))
    tab = pl.BlockSpec((tm, LANES), lambda b, i: (i, 0))
    return pl.pallas_call(
        _inproj_kernel,
        grid=(B, L // tm),
        in_specs=[row(D),
                  pl.BlockSpec((1, 1, HALO, D), lambda b, i: (b, jnp.maximum(i * nh - 1, 0), 0, 0)),
                  pl.BlockSpec((1, 1, HALO, D), lambda b, i: (b, jnp.minimum((i + 1) * nh, nrow - 1), 0, 0)),
                  _const_spec((1, D)), vec, vec,
                  pl.BlockSpec((D, N), lambda b, i: (0, 0), pipeline_mode=pl.Buffered(1)),
                  _const_spec(conv_w.shape), _const_spec(conv_b.shape), tab, tab],
        out_specs=[row(HY_WIDTH), row(HY_WIDTH), row(N - HY_PROJ)],
        out_shape=[jax.ShapeDtypeStruct((B, L, HY_WIDTH), BF16), jax.ShapeDtypeStruct((B, L, HY_WIDTH), BF16),
                   jax.ShapeDtypeStruct((B, L, N - HY_PROJ), BF16)],
        compiler_params=_cparams(("parallel", "parallel")),
        name="inproj",
    )(x, xh, xh, g, shift, scale, w, conv_w, conv_b, *rope)


def _block_diag_mask():
    r = lax.broadcasted_iota(jnp.int32, (LANES, LANES), 0) // RET_HEAD_DIM
    c = lax.broadcasted_iota(jnp.int32, (LANES, LANES), 1) // RET_HEAD_DIM
    return r == c


def _kt_v(k, v):
    return lax.dot_general(k, v, (((0,), (0,)), ((), ())), preferred_element_type=F32)


def _ctx_state_kernel(x_ref, g_ref, sh_ref, sc_ref, wk_ref, wv_ref, lgf_ref, lgb_ref, sf_ref, sb_ref):
    hb = _norm_mod(x_ref[0], g_ref, sh_ref, sc_ref)
    k = jnp.dot(hb, wk_ref[...], preferred_element_type=F32) * (RET_HEAD_DIM ** -0.5)
    v = jnp.dot(hb, wv_ref[...], preferred_element_type=F32).astype(BF16)
    lc = k.shape[0]
    m = lax.broadcasted_iota(jnp.int32, (lc, LANES), 0).astype(F32)
    bd = _block_diag_mask()
    for p in range(RET_BLOCKS):
        lanes = slice(p * LANES, (p + 1) * LANES)
        wf = jnp.exp(lgf_ref[p] * (lc - 1.0 - m))
        wb = jnp.exp(lgb_ref[p] * m)
        sf_ref[0, p] = jnp.where(bd, _kt_v((k[:, lanes] * wf).astype(BF16), v[:, lanes]), 0.0)
        sb_ref[0, p] = jnp.where(bd, _kt_v((k[:, lanes] * wb).astype(BF16), v[:, lanes]), 0.0)


def _ctx_states(ctx, g, shift, scale, wk, wv, lgf, lgb):
    B, lc, D = ctx.shape
    st = jax.ShapeDtypeStruct((B, RET_BLOCKS, LANES, LANES), F32)
    sspec = pl.BlockSpec((1, RET_BLOCKS, LANES, LANES), lambda b: (b, 0, 0, 0))
    return pl.pallas_call(
        _ctx_state_kernel,
        grid=(B,),
        in_specs=[pl.BlockSpec((1, lc, D), lambda b: (b, 0, 0)), _const_spec((1, D)),
                  _const_spec(shift.shape), _const_spec(scale.shape), _const_spec(wk.shape), _const_spec(wv.shape),
                  _const_spec(lgf.shape), _const_spec(lgb.shape)],
        out_specs=[sspec, sspec],
        out_shape=[st, st],
        compiler_params=_cparams(("parallel",)),
        name="ctx_state",
    )(ctx, g, shift, scale, wk, wv, lgf, lgb)


FILT_ROWS = 1024
FILT_GROUPS = LANES // HY_BANDS
FILT_HALVES = LANES // HY_FILT_HID


def _filt_kernel(bands_ref, w1t_ref, w1c_ref, w1s_ref, b1_ref, fr1_ref, w2_ref, b2_ref, fr2_ref,
                 w3_ref, dl_ref, h_ref, asum_ref, *, seq_len):
    i = pl.program_id(0)
    tl = FILT_ROWS
    gr = tl // FILT_GROUPS
    hr = tl // FILT_HALVES
    lane = lax.broadcasted_iota(jnp.int32, (1, LANES), 1)
    base = i * tl
    dot = functools.partial(jnp.dot, preferred_element_type=F32, precision=HIGHEST)

    n8 = (lax.broadcasted_iota(jnp.int32, (gr, 1), 0) + (lane // HY_BANDS) * gr + base).astype(F32)
    arg = bands_ref[...] * ((2.0 * math.pi / seq_len) * n8)
    cz, sz = jnp.cos(arg), jnp.sin(arg)
    nh = (lax.broadcasted_iota(jnp.int32, (hr, 1), 0) + (lane // HY_FILT_HID) * hr + base).astype(F32)
    blocks = [dot(cz, w1c_ref[r]) - dot(sz, w1s_ref[r]) for r in range(hr // gr)]
    pre = jnp.concatenate(blocks, axis=0) + (nh / (seq_len - 1.0)) * w1t_ref[...] + b1_ref[...]
    h = jnp.sin(fr1_ref[...] * pre)
    h = jnp.sin(fr2_ref[...] * (dot(h, w2_ref[...]) + b2_ref[...])).astype(BF16)

    @pl.when(i == 0)
    def _():
        asum_ref[...] = jnp.zeros_like(asum_ref)

    for p in range(FILT_HALVES):
        n = (lax.broadcasted_iota(jnp.int32, (hr, 1), 0) + p * hr + base).astype(F32)
        t = n / (seq_len - 1.0)
        taps = jnp.dot(h, w3_ref[p], preferred_element_type=F32) * jnp.exp(-t * dl_ref[...])
        asum_ref[...] += jnp.sum(jnp.abs(taps), axis=0, keepdims=True)
        rows = slice(p * hr, (p + 1) * hr)
        h_ref[0, rows] = taps[:, :HY_WIDTH].astype(BF16)
        h_ref[1, rows] = jnp.where(n == 0.0, 0.0, taps[:, HY_WIDTH:]).astype(BF16)


def _filters(seq_len, w1, b1, fr1, w2, b2, fr2, w3):
    hid = HY_FILT_HID
    f32 = lambda a: a.astype(F32)
    bands = np.tile(np.linspace(1e-4, HY_BANDS - 1, HY_BANDS, dtype=np.float32), FILT_GROUPS)[None]
    deltas = np.abs(np.linspace(math.log(HY_DECAY_TARGET) / HY_SLOW_PCT,
                                math.log(HY_DECAY_TARGET) / HY_FAST_PCT, HY_WIDTH, dtype=np.float32))
    dl = np.tile(deltas, 2)[None, :]
    halves = lambda v: jnp.tile(f32(v)[None], (1, FILT_HALVES))

    nb = FILT_GROUPS // FILT_HALVES
    sel = np.zeros((nb, FILT_GROUPS, FILT_HALVES), np.float32)
    for r in range(nb):
        for p in range(FILT_HALVES):
            sel[r, p * nb + r, p] = 1.0
    eye = np.eye(FILT_HALVES, dtype=np.float32)

    def place_l1(wb):
        return (sel[:, :, None, :, None] * f32(wb)[None, None, :, None, :]).reshape(nb, LANES, LANES)

    w2d = (eye[:, None, :, None] * f32(w2)[None, :, None, :]).reshape(LANES, LANES)
    w3p = (eye[:, :, None, None] * f32(w3)[None, None]).reshape(FILT_HALVES, LANES, 2 * HY_WIDTH)
    args = (jnp.asarray(bands), halves(w1[0]), place_l1(w1[1:1 + HY_BANDS]), place_l1(w1[1 + HY_BANDS:]),
            halves(b1), halves(fr1), w2d, halves(b2), halves(fr2), w3p.astype(BF16), jnp.asarray(dl))
    return pl.pallas_call(
        functools.partial(_filt_kernel, seq_len=seq_len),
        grid=(seq_len // FILT_ROWS,),
        in_specs=[_const_spec(a.shape) for a in args],
        out_specs=[pl.BlockSpec((2, FILT_ROWS, HY_WIDTH), lambda i: (0, i, 0)),
                   _const_spec((1, 2 * HY_WIDTH))],
        out_shape=[jax.ShapeDtypeStruct((2, seq_len, HY_WIDTH), BF16),
                   jax.ShapeDtypeStruct((1, 2 * HY_WIDTH), F32)],
        compiler_params=_cparams(("arbitrary",)),
        name="filt",
    )(*args)


def _dft_tables(seq_len):
    n = 2 * seq_len
    n1_full = n // DFT_N2
    n1_used = n1_full // 2
    nk1 = n1_full // 2 + 1
    k1 = np.arange(nk1)[:, None]
    n1 = np.arange(n1_used)[None, :]
    th = 2.0 * np.pi * ((k1 * n1) % n1_full) / n1_full
    ck = np.full((nk1, 1), 2.0)
    ck[0, 0] = 1.0
    ck[-1, 0] = 1.0
    assert nk1 <= DFT_KHALF
    f1 = np.zeros((2 * DFT_KHALF, n1_used))
    f1[:nk1] = np.cos(th)
    f1[DFT_KHALF:DFT_KHALF + nk1] = -np.sin(th)
    finv = np.zeros((n1_used, 2 * DFT_KHALF))
    finv[:, 0:2 * nk1:2] = (ck * np.cos(th) / n).T
    finv[:, 1:2 * nk1:2] = (-ck * np.sin(th) / n).T
    k1 = np.arange(nk1)[:, None, None]
    k2 = np.arange(DFT_N2)[None, :, None]
    n2 = np.arange(DFT_N2)[None, None, :]
    ph = 2.0 * np.pi * ((n1_full * n2 * k2 + n2 * k1) % n) / n
    gr, gi = np.cos(ph), -np.sin(ph)
    g = np.concatenate([np.concatenate([gr, -gi], 2), np.concatenate([gi, gr], 2)], 1)
    grt, git = np.swapaxes(gr, 1, 2), -np.swapaxes(gi, 1, 2)
    ginv = np.concatenate([np.concatenate([grt, -git], 2), np.concatenate([git, grt], 2)], 1)
    npad = -nk1 % DFTC_K1
    assert nk1 + npad <= DFT_KHALF
    g, ginv = (np.concatenate([m, np.zeros((npad,) + m.shape[1:])], 0) for m in (g, ginv))
    f = lambda a: jnp.asarray(a.astype(np.float32))
    return f(f1), f(finv), f(g), f(ginv)


def _rows2d(a):
    return a.reshape(a.shape[0] * a.shape[1], a.shape[2])


def _dft_a_kernel(x_ref, f_ref, o_ref):
    n1u = x_ref.shape[1]
    f = f_ref[...]
    for r0 in range(0, x_ref.shape[2], DFT_ROWS):
        grp = slice(r0, r0 + DFT_ROWS)
        xs = _rows2d(jnp.swapaxes(x_ref[0, :, grp, :], 0, 1))
        res = [jnp.dot(f, xs[r * n1u:(r + 1) * n1u], preferred_element_type=F32).astype(BF16)
               for r in range(DFT_ROWS)]
        a = jnp.swapaxes(jnp.stack(res, axis=0), 0, 1)
        o_ref[:, 0, grp, :] = a[:DFT_KHALF]
        o_ref[:, 1, grp, :] = a[DFT_KHALF:]


def _dft_a(xv, f1):
    Bd, n1u, n2, C = xv.shape
    return pl.pallas_call(
        _dft_a_kernel,
        grid=(Bd, n2 // DFT_BLOCK_ROWS),
        in_specs=[pl.BlockSpec((1, n1u, DFT_BLOCK_ROWS, C), lambda b, j: (b, 0, j, 0)), _const_spec(f1.shape)],
        out_specs=pl.BlockSpec((DFT_KHALF, 2, DFT_BLOCK_ROWS, C), lambda b, j: (0, 0, j, b)),
        out_shape=jax.ShapeDtypeStruct((DFT_KHALF, 2, n2, Bd * C), BF16),
        compiler_params=_cparams(("parallel", "parallel")),
        name="dft_a",
    )(xv, f1)


def _dft_c_kernel(a_ref, h_ref, g_ref, gi_ref, asum_ref, o_ref):
    C = HY_WIDTH
    asum = asum_ref[...]
    inv = 1.0 / (asum[:, :C] + asum[:, C:] + 1e-6)
    step = min(DFTC_LANES, a_ref.shape[3])
    for t in range(a_ref.shape[0]):
        g = g_ref[t].astype(BF16)
        gi = gi_ref[t].astype(BF16)
        rows = g.shape[0]
        half = rows // 2
        xh = jnp.dot(g, h_ref[t].reshape(rows, h_ref.shape[3]), preferred_element_type=F32)
        kr = (xh[:half, :C] + xh[:half, C:]) * inv
        ki = (xh[half:, :C] - xh[half:, C:]) * inv
        for l0 in range(0, a_ref.shape[3], step):
            x = jnp.dot(g, a_ref[t, :, :, l0:l0 + step].reshape(rows, step), preferred_element_type=F32)
            ys = []
            for c0 in range(0, step, C):
                xr, xi = x[:half, c0:c0 + C], x[half:, c0:c0 + C]
                ys.append(jnp.concatenate([xr * kr - xi * ki, xr * ki + xi * kr], axis=0).astype(BF16))
            bk = jnp.dot(gi, jnp.concatenate(ys, axis=1), preferred_element_type=F32).astype(BF16)
            o_ref[t, :, :, l0:l0 + step] = bk.reshape(2, half, step)


def _dft_c(a, h, g, ginv, asum):
    _, _, n2, bc = a.shape
    nk = g.shape[0]
    dspec = lambda w: pl.BlockSpec((DFTC_K1, 2, n2, w), lambda k: (k, 0, 0, 0))
    gspec = pl.BlockSpec((DFTC_K1, 2 * n2, 2 * n2), lambda k: (k, 0, 0))
    return pl.pallas_call(
        _dft_c_kernel,
        grid=(nk // DFTC_K1,),
        in_specs=[dspec(bc), dspec(h.shape[3]), gspec, gspec, _const_spec(asum.shape)],
        out_specs=dspec(bc),
        out_shape=jax.ShapeDtypeStruct((nk, 2, n2, bc), BF16),
        compiler_params=_cparams(("parallel",)),
        name="dft_c",
    )(a, h, g, ginv, asum)


def _idft_a_kernel(b_ref, f_ref, vg_ref, x0_ref, bias_ref, o_ref):
    nk1, _, _, C = b_ref.shape
    n1u = vg_ref.shape[1]
    R = DFT_ROWS
    mp = f_ref.shape[1]
    f = f_ref[...]
    for r0 in range(0, b_ref.shape[2], R):
        grp = slice(r0, r0 + R)
        bv = b_ref[:, :, grp, :].reshape(2 * nk1, R, C)
        bv = jnp.concatenate([bv, jnp.zeros((mp - 2 * nk1, R, C), BF16)], axis=0)
        bs = _rows2d(jnp.swapaxes(bv, 0, 1))
        ys = [jnp.dot(f, bs[r * mp:(r + 1) * mp], preferred_element_type=F32).astype(BF16) for r in range(R)]
        y = _rows2d(jnp.swapaxes(jnp.stack(ys, axis=0), 0, 1)).astype(F32)
        y = y + _rows2d(vg_ref[0, :, grp, :]).astype(F32) * bias_ref[...]
        y = y * _rows2d(x0_ref[0, :, grp, :]).astype(F32)
        o_ref[0, :, grp, :] = y.astype(BF16).reshape(n1u, R, C)


def _idft_a(bk, finv, vgv, x0v, bias):
    B, n1u, n2, C = vgv.shape
    nk1 = bk.shape[0]
    dsp = pl.BlockSpec((1, n1u, DFT_BLOCK_ROWS, C), lambda b, j: (b, 0, j, 0))
    return pl.pallas_call(
        _idft_a_kernel,
        grid=(B, n2 // DFT_BLOCK_ROWS),
        in_specs=[pl.BlockSpec((nk1, 2, DFT_BLOCK_ROWS, C), lambda b, j: (0, 0, j, b)),
                  _const_spec(finv.shape), dsp, dsp, _const_spec(bias.shape)],
        out_specs=dsp,
        out_shape=jax.ShapeDtypeStruct((B, n1u, n2, C), BF16),
        compiler_params=_cparams(("parallel", "parallel")),
        name="idft_a",
    )(bk, finv, vgv, x0v, bias)


def _retention_kernel(q_ref, k_ref, v_ref, g_ref, lgf_ref, lgb_ref, s0f_ref, s0b_ref, gn_ref, o_ref,
                      inc_ref, st_ref, dm_ref, wt_ref):
    C = RET_CHUNK
    seq_len = q_ref.shape[1]
    nc = seq_len // C
    lgf = lgf_ref[0]
    lgb = lgb_ref[0]
    lane = lax.broadcasted_iota(jnp.int32, (1, LANES), 1)
    head_of_lane = lane // RET_HEAD_DIM
    bd = _block_diag_mask()

    ri = lax.broadcasted_iota(jnp.int32, (C, LANES), 0).astype(F32)
    wt_ref[0] = jnp.exp(lgf * (ri + 1.0)).astype(BF16)
    wt_ref[1] = jnp.exp(lgb * (C - ri)).astype(BF16)
    wt_ref[2] = jnp.exp(lgf * (C - 1.0 - ri)).astype(BF16)
    wt_ref[3] = jnp.exp(lgb * ri).astype(BF16)
    di = lax.broadcasted_iota(jnp.int32, (C, C), 0)
    dj = lax.broadcasted_iota(jnp.int32, (C, C), 1)
    dd = (di - dj).astype(F32)
    for h in range(HEADS_PER_BLOCK):
        lf = jnp.sum(jnp.where(lane == h * RET_HEAD_DIM, lgf, 0.0), axis=1, keepdims=True)
        lb = jnp.sum(jnp.where(lane == h * RET_HEAD_DIM, lgb, 0.0), axis=1, keepdims=True)
        dm_ref[h] = jnp.where(dd > 0, jnp.exp(lf * dd), jnp.where(dd < 0, jnp.exp(-lb * dd), 2.0)).astype(BF16)

    def chunk(c):
        return pl.ds(pl.multiple_of(c * C, C), C)

    def inc_body(c, carry):
        rows = chunk(c)
        k = k_ref[0, rows, :]
        kw = jnp.concatenate([k * wt_ref[2], k * wt_ref[3]], axis=1)
        t = _kt_v(kw, v_ref[0, rows, :])
        inc_ref[c, :LANES] = jnp.where(bd, t[:LANES], 0.0)
        inc_ref[c, LANES:] = jnp.where(bd, t[LANES:], 0.0)
        return carry

    lax.fori_loop(0, nc, inc_body, 0, unroll=True)

    gcf = jnp.exp(lgf * float(C))
    gcb = jnp.exp(lgb * float(C))

    def fscan(c, s):
        st_ref[c, :LANES] = s.astype(BF16)
        return s * gcf + inc_ref[c, :LANES]

    lax.fori_loop(0, nc, fscan, s0f_ref[0, 0], unroll=True)

    def bscan(i, s):
        c = nc - 1 - i
        st_ref[c, LANES:] = s.astype(BF16)
        return s * gcb + inc_ref[c, LANES:]

    lax.fori_loop(0, nc, bscan, s0b_ref[0, 0], unroll=True)

    gn = gn_ref[...]
    first = head_of_lane == 0
    inv_n = 1.0 / RET_HEAD_DIM

    def head_mean(a):
        tot = jnp.sum(a, axis=1, keepdims=True)
        s0 = jnp.sum(jnp.where(first, a, 0.0), axis=1, keepdims=True)
        return jnp.where(first, s0, tot - s0) * inv_n

    def out_body(c, carry):
        rows = chunk(c)
        q = q_ref[0, rows, :]
        k = k_ref[0, rows, :]
        v = v_ref[0, rows, :]
        qw = jnp.concatenate([q * wt_ref[0], q * wt_ref[1]], axis=1)
        o = jnp.dot(qw, st_ref[c], preferred_element_type=F32)
        oh = []
        for h in range(HEADS_PER_BLOCK):
            qm = jnp.where(head_of_lane == h, q, jnp.zeros_like(q))
            sc = lax.dot_general(qm, k, (((1,), (1,)), ((), ())), preferred_element_type=F32)
            oh.append(jnp.dot(sc.astype(BF16) * dm_ref[h], v, preferred_element_type=F32))
        o = o + jnp.where(first, oh[0], oh[1])
        d = o - head_mean(o)
        y = d * lax.rsqrt(head_mean(d * d) + NORM_EPS) * gn
        o_ref[0, rows, :] = (y * _silu(g_ref[0, rows, :].astype(F32))).astype(BF16)
        return carry

    lax.fori_loop(0, nc, out_body, 0, unroll=True)


def _retention(u, lgf, lgb, s0f, s0b, gn):
    B, L, _ = u.shape
    assert HEADS_PER_BLOCK == 2
    nc = L // RET_CHUNK
    col = lambda g: pl.BlockSpec((1, L, LANES), lambda b, p: (b, 0, g * RET_BLOCKS + p))
    lsp = pl.BlockSpec((1, 1, LANES), lambda b, p: (p, 0, 0))
    ssp = pl.BlockSpec((1, 1, LANES, LANES), lambda b, p: (b, p, 0, 0))
    return pl.pallas_call(
        _retention_kernel,
        grid=(B, RET_BLOCKS),
        in_specs=[col(0), col(1), col(2), col(3), lsp, lsp, ssp, ssp,
                  pl.BlockSpec((1, LANES), lambda b, p: (0, p))],
        out_specs=pl.BlockSpec((1, L, LANES), lambda b, p: (b, 0, p)),
        out_shape=jax.ShapeDtypeStruct((B, L, RET_WIDTH), BF16),
        scratch_shapes=[pltpu.VMEM((nc, 2 * LANES, LANES), F32),
                        pltpu.VMEM((nc, 2 * LANES, LANES), BF16),
                        pltpu.VMEM((HEADS_PER_BLOCK, RET_CHUNK, RET_CHUNK), BF16),
                        pltpu.VMEM((4, RET_CHUNK, LANES), BF16)],
        compiler_params=_cparams(("parallel", "parallel")),
        name="retention",
    )(u, u, u, u, lgf, lgb, s0f, s0b, gn)


def _rope_tables(seq_len):
    rows = seq_len // GRID_W
    r, col = np.meshgrid(np.arange(rows, dtype=np.float64), np.arange(GRID_W, dtype=np.float64), indexing="ij")
    inv = ROPE_BASE ** (-np.arange(ROPE_PAIRS_AXIS, dtype=np.float64) / ROPE_PAIRS_AXIS)
    ang = np.concatenate([r.reshape(-1, 1) * inv, col.reshape(-1, 1) * inv], axis=-1)
    ang = np.tile(np.repeat(ang, 2, axis=1), (1, HEADS_PER_BLOCK))
    sign = np.where(np.arange(LANES) % 2 == 0, -1.0, 1.0)
    return jnp.asarray(np.cos(ang).astype(np.float32)), jnp.asarray((np.sin(ang) * sign).astype(np.float32))


OUT_ROWS = 1024
OUT_FF_CHUNK = 2048


def _out_mlp_kernel(x_ref, yh_ref, yr_ref, wo1_ref, wo2_ref, w1_ref, w2_ref, g2_ref, gf_ref, mod_ref, o_ref,
                    *, ff_chunk):
    mod = mod_ref[0]
    mix = (jnp.dot(yh_ref[0], wo1_ref[...], preferred_element_type=F32)
           + jnp.dot(yr_ref[0], wo2_ref[...], preferred_element_type=F32))
    x1 = x_ref[0] + mod[0:1] * mix
    h = (_rms(x1) * (g2_ref[...] * (1.0 + mod[2:3])) + mod[1:2]).astype(BF16)
    acc = jnp.zeros_like(x1)
    for j in range(w1_ref.shape[1] // ff_chunk):
        cols = slice(j * ff_chunk, (j + 1) * ff_chunk)
        a = jnp.maximum(jnp.dot(h, w1_ref[:, cols], preferred_element_type=F32), 0.0)
        acc = acc + jnp.dot((a * a).astype(BF16), w2_ref[cols, :], preferred_element_type=F32)
    x2 = x1 + mod[3:4] * acc
    o_ref[0] = _rms(x2) * gf_ref[...]


def _out_mlp(x, yh, yr, wo1, wo2, w1, w2, g2, gf, mod, tm):
    B, L, D = x.shape
    dff = w1.shape[1]
    row = lambda n: pl.BlockSpec((1, tm, n), lambda b, i: (b, i, 0))
    wsp = lambda a: pl.BlockSpec(a.shape, lambda b, i: (0, 0), pipeline_mode=pl.Buffered(1))
    return pl.pallas_call(
        functools.partial(_out_mlp_kernel, ff_chunk=OUT_FF_CHUNK),
        grid=(B, L // tm),
        in_specs=[row(D), row(yh.shape[2]), row(yr.shape[2]), wsp(wo1), wsp(wo2), wsp(w1), wsp(w2),
                  _const_spec((1, D)), _const_spec((1, D)),
                  pl.BlockSpec((1, 4, D), lambda b, i: (b, 0, 0))],
        out_specs=row(D),
        out_shape=jax.ShapeDtypeStruct((B, L, D), F32),
        compiler_params=_cparams(("parallel", "parallel")),
        name="out_mlp",
    )(x, yh, yr, wo1, wo2, w1, w2, g2, gf, mod)


def kernel(x, c, ctx, c_ctx, w_ada, b_ada, norm1_g, w_in, hy_conv_w, hy_conv_b, hy_f_w1, hy_f_b1, hy_f_freq1,
           hy_f_w2, hy_f_b2, hy_f_freq2, hy_f_w3, hy_bias, ret_decay_logit, ret_gn_g, w_out, norm2_g, w_mlp1,
           w_mlp2, norm_f_g):
    B, L, D = x.shape
    depth = w_ada.shape[0]
    assert depth == 1, "single-layer block"
    l = 0
    rows = -(-(B + 1) // 8) * 8
    cvec = jnp.concatenate([c, c_ctx[None], jnp.zeros((rows - B - 1, D), F32)], axis=0)
    mod = _ada(cvec, w_ada[l], b_ada[l][None])
    mx = mod[:B].reshape(B, 6, D)
    mc = mod[B].reshape(6, D)

    lg = jax.nn.log_sigmoid(ret_decay_logit[l].astype(F32))
    lg_lanes = jnp.repeat(lg, RET_HEAD_DIM, axis=1).reshape(2, RET_BLOCKS, 1, LANES)
    lgf, lgb = lg_lanes[0], lg_lanes[1]

    w_in_b = w_in[l].astype(BF16)
    g1 = norm1_g[l][None]
    vg, x0c, u_ret = _inproj(x, g1, mx[:, 0:1], mx[:, 1:2], w_in_b, hy_conv_w[l].astype(F32),
                             hy_conv_b[l][None].astype(F32), _rope_tables(L), tm=INPROJ_ROWS)
    kc0 = HY_PROJ + RET_WIDTH
    s0f, s0b = _ctx_states(ctx, g1, mc[0][None, None], mc[1][None, None], w_in_b[:, kc0:kc0 + RET_WIDTH],
                           w_in_b[:, kc0 + RET_WIDTH:kc0 + 2 * RET_WIDTH], lgf, lgb)

    f1, finv, gmat, ginv = _dft_tables(L)
    f1, finv = f1.astype(BF16), finv.astype(BF16)
    taps, asum = _filters(L, hy_f_w1[l], hy_f_b1[l], hy_f_freq1[l], hy_f_w2[l], hy_f_b2[l], hy_f_freq2[l],
                          hy_f_w3[l])
    tview = lambda a: a.reshape(a.shape[0], L // DFT_N2, DFT_N2, HY_WIDTH)
    bk = _dft_c(_dft_a(tview(vg), f1), _dft_a(tview(taps), f1), gmat, ginv, asum)
    y_hy = _idft_a(bk, finv, tview(vg), tview(x0c), hy_bias[l][None].astype(F32)).reshape(B, L, HY_WIDTH)

    y_ret = _retention(u_ret, lgf, lgb, s0f, s0b, ret_gn_g[l][None].astype(F32))

    w_out_b = w_out[l].astype(BF16)
    mod2 = mx[:, 2:6]
    return _out_mlp(x, y_hy, y_ret, w_out_b[:HY_WIDTH], w_out_b[HY_WIDTH:], w_mlp1[l].astype(BF16),
                    w_mlp2[l].astype(BF16), norm2_g[l][None], norm_f_g[None], mod2, tm=OUT_ROWS)
```
